```python
import functools
import jax, jax.numpy as jnp
from jax import lax
import numpy as np

D_MODEL = 1024
BATCH = 2
SEQ = 8192
DEPTH = 4
DEC_BATCH = 128
DEC_SEQ = 4
PAST_LEN = 2048
PAGE_SIZE = 128

N_META = 16
CONV_W = 4
EPS = 1e-6
D_RNN = D_MODEL
N_RNN_BLOCKS = 16
RNN_BLOCK = D_RNN // N_RNN_BLOCKS
LRU_C = 8.0
D_SSM = D_MODEL
SSM_HEAD_DIM = 64
N_SSM_HEADS = D_SSM // SSM_HEAD_DIM
N_SSM_GROUPS = 4
HEADS_PER_GROUP = N_SSM_HEADS // N_SSM_GROUPS
SSM_STATE = 128
SSM_CHUNK = 128
D_XBC = D_SSM + 2 * N_SSM_GROUPS * SSM_STATE
HEAD_DIM = 64
N_HEADS = D_MODEL // HEAD_DIM
N_KV_HEADS = 4
Q_PER_KV = N_HEADS // N_KV_HEADS
N_IDX_HEADS = 8
IDX_DIM = 64
TOP_K_MAX = 256
TOP_K_FRACTION = 4
Q_BLOCK = 128
ROPE_THETA = 10000.0
D_FF = 4 * D_MODEL
N_BRANCH = 3
SPLITS = (D_RNN, D_RNN, D_SSM, D_XBC, N_SSM_HEADS, N_HEADS * HEAD_DIM, N_KV_HEADS * HEAD_DIM,
          N_KV_HEADS * HEAD_DIM, N_IDX_HEADS * IDX_DIM, IDX_DIM, N_IDX_HEADS, N_BRANCH * D_MODEL)
D_IN = sum(SPLITS)
SPLIT_POINTS = tuple(int(s) for s in np.cumsum(SPLITS)[:-1])

kernel_name = "hybrid_lru_ssd_dsa_decoder_step"


def rmsnorm(x, g):
    xf = x.astype(jnp.float32)
    y = xf * lax.rsqrt(jnp.mean(xf * xf, axis=-1, keepdims=True) + EPS)
    return (y * g.astype(jnp.float32)).astype(x.dtype)


def rope(x, pos):
    half = x.shape[-1] // 2
    freq = ROPE_THETA ** (-jnp.arange(half, dtype=jnp.float32) / half)
    ang = pos.astype(jnp.float32)[:, None] * freq[None, :]
    cos = jnp.cos(ang)[:, None, :]
    sin = jnp.sin(ang)[:, None, :]
    xf = x.astype(jnp.float32)
    x1, x2 = xf[..., :half], xf[..., half:]
    return jnp.concatenate([x1 * cos - x2 * sin, x1 * sin + x2 * cos], axis=-1).astype(x.dtype)


def causal_conv(u, buf, w, b):
    t = u.shape[1]
    ext = jnp.concatenate([buf.astype(u.dtype), u], axis=1)
    out = b + sum(w[j] * ext[:, j:j + t] for j in range(CONV_W))
    return out, ext[:, -(CONV_W - 1):]


def rglru(u, h0, w_a, b_a, w_x, b_x, lam):
    bsz, t = u.shape[:2]
    ub = u.reshape(bsz, t, N_RNN_BLOCKS, RNN_BLOCK)
    r = jax.nn.sigmoid(jnp.einsum('btni,nij->btnj', ub, w_a).reshape(bsz, t, D_RNN) + b_a)
    i = jax.nn.sigmoid(jnp.einsum('btni,nij->btnj', ub, w_x).reshape(bsz, t, D_RNN) + b_x)
    log_a = (-LRU_C * r * jax.nn.softplus(-lam)).astype(jnp.float32)
    a = jnp.exp(log_a)
    b = jnp.sqrt(-jnp.expm1(2.0 * log_a)) * (i * u).astype(jnp.float32)
    b = b.at[:, 0].add(a[:, 0] * h0.astype(jnp.float32))

    def combine(left, right):
        al, bl = left
        ar, br = right
        return al * ar, ar * bl + br

    _, h = lax.associative_scan(combine, (a, b), axis=1)
    return h.astype(u.dtype), h[:, -1]


def ssd_scan(x, dt, a, bm, cm, h0, chunk):
    bsz, t = x.shape[:2]
    nc = t // chunk
    xc = x.reshape(bsz, nc, chunk, N_SSM_GROUPS, HEADS_PER_GROUP, SSM_HEAD_DIM)
    dtc = dt.reshape(bsz, nc, chunk, N_SSM_GROUPS, HEADS_PER_GROUP)
    bc = bm.reshape(bsz, nc, chunk, N_SSM_GROUPS, SSM_STATE)
    cc = cm.reshape(bsz, nc, chunk, N_SSM_GROUPS, SSM_STATE)
    cum = jnp.cumsum(dtc * a, axis=2)
    seg = cum[:, :, :, None] - cum[:, :, None, :]
    causal = jnp.tril(jnp.ones((chunk, chunk), dtype=bool))[:, :, None, None]
    decay = jnp.exp(jnp.where(causal, seg, -jnp.inf))
    cb = jnp.einsum('bcign,bcjgn->bcijg', cc, bc)
    w_intra = cb[..., None] * decay * dtc[:, :, None]
    y_intra = jnp.einsum('bcijgh,bcjghp->bcighp', w_intra, xc)
    decay_end = jnp.exp(cum[:, :, -1:] - cum) * dtc
    chunk_states = jnp.einsum('bcjgn,bcjgh,bcjghp->bcghpn', bc, decay_end, xc)
    chunk_decay = jnp.exp(cum[:, :, -1])

    def step(h, inp):
        st, dec = inp
        return dec[..., None, None] * h + st, h

    h_last, h_in = lax.scan(step, h0, (jnp.swapaxes(chunk_states, 0, 1), jnp.swapaxes(chunk_decay, 0, 1)))
    h_in = jnp.swapaxes(h_in, 0, 1)
    y_inter = jnp.einsum('bcign,bcghpn->bcighp', cc, h_in) * jnp.exp(cum)[..., None]
    y = (y_intra + y_inter).reshape(bsz, t, N_SSM_GROUPS, HEADS_PER_GROUP, SSM_HEAD_DIM)
    return y, h_last


def mamba_mixer(z, xbc, dt_raw, conv_buf, h0, conv_w, conv_b, dt_bias, a_log, d_skip, norm_g, lead):
    bsz, t = z.shape[:2]
    xbc, conv_new = causal_conv(xbc, conv_buf, conv_w, conv_b)
    xbc = jax.nn.silu(xbc)
    gn = N_SSM_GROUPS * SSM_STATE
    xs = xbc[..., :D_SSM].reshape(bsz, t, N_SSM_GROUPS, HEADS_PER_GROUP, SSM_HEAD_DIM)
    bm = xbc[..., D_SSM:D_SSM + gn].reshape(bsz, t, N_SSM_GROUPS, SSM_STATE)
    cm = xbc[..., D_SSM + gn:].reshape(bsz, t, N_SSM_GROUPS, SSM_STATE)
    dt = jax.nn.softplus(dt_raw.astype(jnp.float32) + dt_bias.astype(jnp.float32))
    dt = dt.reshape(bsz, t, N_SSM_GROUPS, HEADS_PER_GROUP)
    a = -jnp.exp(a_log.astype(jnp.float32)).reshape(N_SSM_GROUPS, HEADS_PER_GROUP)
    h = h0.astype(jnp.float32).reshape(bsz, N_SSM_GROUPS, HEADS_PER_GROUP, SSM_HEAD_DIM, SSM_STATE)
    if lead > 0:
        y_lead, h = ssd_scan(xs[:, :lead], dt[:, :lead], a, bm[:, :lead], cm[:, :lead], h, lead)
        y_rest, h = ssd_scan(xs[:, lead:], dt[:, lead:], a, bm[:, lead:], cm[:, lead:], h, SSM_CHUNK)
        y = jnp.concatenate([y_lead, y_rest], axis=1)
    else:
        y, h = ssd_scan(xs, dt, a, bm, cm, h, t)
    y = y + d_skip.reshape(N_SSM_GROUPS, HEADS_PER_GROUP)[:, :, None] * xs
    y = rmsnorm(y.reshape(bsz, t, D_SSM) * jax.nn.silu(z), norm_g)
    return y, conv_new, h.reshape(bsz, N_SSM_HEADS, SSM_HEAD_DIM, SSM_STATE)


def index_scores(qi, w, kidx, qpos):
    s = jnp.einsum('bqhd,bsd->bqhs', qi.astype(jnp.float32), kidx.astype(jnp.float32))
    score = jnp.einsum('bqh,bqhs->bqs', w.astype(jnp.float32), jax.nn.relu(s))
    admissible = jnp.arange(kidx.shape[1])[None, :] <= qpos[:, None]
    return jnp.where(admissible[None], score, -jnp.inf)


def sparse_attend(q, k_sel, v_sel, valid):
    bsz, nq = q.shape[:2]
    qg = q.reshape(bsz, nq, N_KV_HEADS, Q_PER_KV, HEAD_DIM)
    logits = jnp.einsum('bqkgd,bqnkd->bqkgn', qg, k_sel).astype(jnp.float32) * (HEAD_DIM ** -0.5)
    logits = jnp.where(valid[:, :, None, None, :], logits, -jnp.inf)
    p = jax.nn.softmax(logits, axis=-1).astype(v_sel.dtype)
    o = jnp.einsum('bqkgn,bqnkd->bqkgd', p, v_sel)
    return o.reshape(bsz, nq, N_HEADS * HEAD_DIM)


_gather_rows = jax.vmap(lambda a, i: a[i])


def dsa_prompt(q, k, v, qi, kidx, w, k_top):
    bsz, t = q.shape[:2]
    nb = -(-t // Q_BLOCK)
    pad = nb * Q_BLOCK - t

    def blocks(arr):
        arr = jnp.pad(arr, [(0, 0), (0, pad)] + [(0, 0)] * (arr.ndim - 2))
        return jnp.swapaxes(arr.reshape((bsz, nb, Q_BLOCK) + arr.shape[2:]), 0, 1)

    qpos = jnp.arange(nb * Q_BLOCK).reshape(nb, Q_BLOCK)

    def one_block(args):
        qb, qib, wb, pb = args
        vals, idx = lax.top_k(index_scores(qib, wb, kidx, pb), k_top)
        return sparse_attend(qb, _gather_rows(k, idx), _gather_rows(v, idx), jnp.isfinite(vals))

    out = lax.map(one_block, (blocks(q), blocks(qi), blocks(w), qpos))
    return jnp.swapaxes(out, 0, 1).reshape(bsz, nb * Q_BLOCK, N_HEADS * HEAD_DIM)[:, :t]


def dsa_sample(q, k, v, qi, kidx, w, cache_k, cache_v, cache_kidx, page_table, layer, k_top):
    dbsz, s = q.shape[:2]
    past = page_table.shape[1] * PAGE_SIZE
    kidx_past = cache_kidx[layer, page_table].reshape(dbsz, past, IDX_DIM)
    kidx_all = jnp.concatenate([kidx_past.astype(kidx.dtype), kidx], axis=1)
    qpos = past + jnp.arange(s)
    vals, idx = lax.top_k(index_scores(qi, w, kidx_all, qpos), k_top)
    in_past = (idx < past)[..., None, None]
    pidx = jnp.minimum(idx, past - 1)
    phys = _gather_rows(page_table, pidx // PAGE_SIZE)
    off = pidx % PAGE_SIZE
    nidx = jnp.clip(idx - past, 0, s - 1)
    k_sel = jnp.where(in_past, cache_k[layer, phys, off].astype(k.dtype), _gather_rows(k, nidx))
    v_sel = jnp.where(in_past, cache_v[layer, phys, off].astype(v.dtype), _gather_rows(v, nidx))
    return sparse_attend(q, k_sel, v_sel, jnp.isfinite(vals))


def layer_forward(x, p, pos, conv_lru, h_lru, conv_ssm, h_ssm, attn_fn, lead):
    bsz, t = x.shape[:2]
    h = rmsnorm(x, p['norm1'])
    proj = h @ p['w_in']
    (lru_x, lru_g, ssm_z, ssm_xbc, ssm_dt, q, k, v, qi, kidx, widx, gates) = jnp.split(proj, SPLIT_POINTS, axis=-1)
    u, conv_lru_new = causal_conv(lru_x, conv_lru, p['lru_conv_w'], p['lru_conv_b'])
    hs, h_lru_new = rglru(u, h_lru, p['lru_wa'], p['lru_ba'], p['lru_wx'], p['lru_bx'], p['lru_lambda'])
    y_a = hs * jax.nn.gelu(lru_g)
    y_b, conv_ssm_new, h_ssm_new = mamba_mixer(ssm_z, ssm_xbc, ssm_dt, conv_ssm, h_ssm, p['ssm_conv_w'],
                                               p['ssm_conv_b'], p['ssm_dt_bias'], p['ssm_a_log'],
                                               p['ssm_d'], p['ssm_norm'], lead)
    q = rope(q.reshape(bsz, t, N_HEADS, HEAD_DIM), pos)
    k = rope(k.reshape(bsz, t, N_KV_HEADS, HEAD_DIM), pos)
    v = v.reshape(bsz, t, N_KV_HEADS, HEAD_DIM)
    qi = rope(qi.reshape(bsz, t, N_IDX_HEADS, IDX_DIM), pos)
    kidx = rope(kidx[:, :, None, :], pos)[:, :, 0, :]
    w = widx * ((N_IDX_HEADS * IDX_DIM) ** -0.5)
    y_c = attn_fn(q, k, v, qi, kidx, w)
    g = jax.nn.sigmoid(gates).reshape(bsz, t, N_BRANCH, D_MODEL)
    merged = (g[:, :, 0] * (y_a @ p['p_lru']) + g[:, :, 1] * (y_b @ p['p_ssm'])
              + g[:, :, 2] * (y_c @ p['p_attn']))
    x = x + merged @ p['w_o']
    h2 = rmsnorm(x, p['norm2'])
    x = x + jnp.square(jax.nn.relu(h2 @ p['w_up'])) @ p['w_down']
    return x, (k, v, kidx, h_ssm_new, conv_ssm_new, h_lru_new, conv_lru_new)


def setup_inputs(seed: int = 0) -> dict:
    key = jax.random.key(seed)
    ks = iter(jax.random.split(key, 48))
    nrm = lambda shape, scale=1.0: scale * jax.random.normal(next(ks), shape, jnp.float32)
    n_pages = PAST_LEN // PAGE_SIZE
    n_pool = (DEC_BATCH * n_pages * 5) // 4
    perm = jax.random.permutation(next(ks), n_pool)
    page_table = perm[:DEC_BATCH * n_pages].reshape(DEC_BATCH, n_pages).astype(jnp.int32)
    u_lam = jax.random.uniform(next(ks), (DEPTH, D_RNN), jnp.float32, 0.9, 0.999)
    dt0 = jnp.exp(jax.random.uniform(next(ks), (DEPTH, N_SSM_HEADS), jnp.float32, np.log(1e-3), np.log(1e-1)))
    a0 = jax.random.uniform(next(ks), (DEPTH, N_SSM_HEADS), jnp.float32, 1.0, 16.0)
    return {
        'x_prompt': nrm((BATCH, SEQ, D_MODEL)),
        'x_sample': nrm((DEC_BATCH, DEC_SEQ, D_MODEL)),
        'cache_k': nrm((DEPTH, n_pool, PAGE_SIZE, N_KV_HEADS, HEAD_DIM)),
        'cache_v': nrm((DEPTH, n_pool, PAGE_SIZE, N_KV_HEADS, HEAD_DIM)),
        'cache_kidx': nrm((DEPTH, n_pool, PAGE_SIZE, IDX_DIM)),
        'state_ssm': nrm((DEPTH, DEC_BATCH, N_SSM_HEADS, SSM_HEAD_DIM, SSM_STATE), 0.1),
        'state_ssm_conv': nrm((DEPTH, DEC_BATCH, CONV_W - 1, D_XBC)),
        'state_lru_h': nrm((DEPTH, DEC_BATCH, D_RNN), 0.5),
        'state_lru_conv': nrm((DEPTH, DEC_BATCH, CONV_W - 1, D_RNN)),
        'page_table': page_table,
        'meta_tokens': nrm((N_META, D_MODEL)),
        'norm1': 1.0 + nrm((DEPTH, D_MODEL), 0.01),
        'w_in': nrm((DEPTH, D_MODEL, D_IN), D_MODEL ** -0.5),
        'lru_conv_w': nrm((DEPTH, CONV_W, D_RNN), CONV_W ** -0.5),
        'lru_conv_b': nrm((DEPTH, D_RNN), 0.01),
        'lru_wa': nrm((DEPTH, N_RNN_BLOCKS, RNN_BLOCK, RNN_BLOCK), RNN_BLOCK ** -0.5),
        'lru_ba': nrm((DEPTH, D_RNN), 0.01),
        'lru_wx': nrm((DEPTH, N_RNN_BLOCKS, RNN_BLOCK, RNN_BLOCK), RNN_BLOCK ** -0.5),
        'lru_bx': nrm((DEPTH, D_RNN), 0.01),
        'lru_lambda': jnp.log(u_lam) - jnp.log1p(-u_lam),
        'ssm_conv_w': nrm((DEPTH, CONV_W, D_XBC), CONV_W ** -0.5),
        'ssm_conv_b': nrm((DEPTH, D_XBC), 0.01),
        'ssm_dt_bias': dt0 + jnp.log(-jnp.expm1(-dt0)),
        'ssm_a_log': jnp.log(a0),
        'ssm_d': 1.0 + nrm((DEPTH, N_SSM_HEADS), 0.1),
        'ssm_norm': 1.0 + nrm((DEPTH, D_SSM), 0.01),
        'p_lru': nrm((DEPTH, D_RNN, D_MODEL), D_RNN ** -0.5),
        'p_ssm': nrm((DEPTH, D_SSM, D_MODEL), D_SSM ** -0.5),
        'p_attn': nrm((DEPTH, N_HEADS * HEAD_DIM, D_MODEL), (N_HEADS * HEAD_DIM) ** -0.5),
        'w_o': nrm((DEPTH, D_MODEL, D_MODEL), D_MODEL ** -0.5),
        'norm2': 1.0 + nrm((DEPTH, D_MODEL), 0.01),
        'w_up': nrm((DEPTH, D_MODEL, D_FF), D_MODEL ** -0.5),
        'w_down': nrm((DEPTH, D_FF, D_MODEL), D_FF ** -0.5),
        'final_norm': 1.0 + nrm((D_MODEL,), 0.01),
    }


def _stack_layers(rows):
    return [jnp.stack(a) for a in zip(*rows)]


def reference(x_prompt, x_sample, cache_k, cache_v, cache_kidx, state_ssm, state_ssm_conv, state_lru_h,
              state_lru_conv, page_table, meta_tokens, norm1, w_in, lru_conv_w, lru_conv_b, lru_wa, lru_ba,
              lru_wx, lru_bx, lru_lambda, ssm_conv_w, ssm_conv_b, ssm_dt_bias, ssm_a_log, ssm_d, ssm_norm,
              p_lru, p_ssm, p_attn, w_o, norm2, w_up, w_down, final_norm):
    bsz, seq = x_prompt.shape[:2]
    dbsz, dseq = x_sample.shape[:2]
    past = page_table.shape[1] * PAGE_SIZE
    k_top_prompt = min(TOP_K_MAX, seq // TOP_K_FRACTION)
    k_top_sample = min(TOP_K_MAX, (past + dseq) // TOP_K_FRACTION)
    xp = jnp.concatenate([jnp.broadcast_to(meta_tokens[None].astype(x_prompt.dtype), (bsz, N_META, D_MODEL)),
                          x_prompt], axis=1)
    t_prompt = seq + N_META
    pos_prompt = jnp.arange(t_prompt)
    pos_sample = past + jnp.arange(dseq)
    xs = x_sample
    prompt_rows, sample_rows = [], []
    for l in range(DEPTH):
        p = {'norm1': norm1[l], 'w_in': w_in[l], 'lru_conv_w': lru_conv_w[l], 'lru_conv_b': lru_conv_b[l],
             'lru_wa': lru_wa[l], 'lru_ba': lru_ba[l], 'lru_wx': lru_wx[l], 'lru_bx': lru_bx[l],
             'lru_lambda': lru_lambda[l], 'ssm_conv_w': ssm_conv_w[l], 'ssm_conv_b': ssm_conv_b[l],
             'ssm_dt_bias': ssm_dt_bias[l], 'ssm_a_log': ssm_a_log[l], 'ssm_d': ssm_d[l],
             'ssm_norm': ssm_norm[l], 'p_lru': p_lru[l], 'p_ssm': p_ssm[l], 'p_attn': p_attn[l],
             'w_o': w_o[l], 'norm2': norm2[l], 'w_up': w_up[l], 'w_down': w_down[l]}
        xp, rows_p = layer_forward(
            xp, p, pos_prompt,
            jnp.zeros((bsz, CONV_W - 1, D_RNN), xp.dtype), jnp.zeros((bsz, D_RNN), jnp.float32),
            jnp.zeros((bsz, CONV_W - 1, D_XBC), xp.dtype),
            jnp.zeros((bsz, N_SSM_HEADS, SSM_HEAD_DIM, SSM_STATE), jnp.float32),
            functools.partial(dsa_prompt, k_top=k_top_prompt), N_META)
        prompt_rows.append(rows_p)
        attn_s = functools.partial(dsa_sample, cache_k=cache_k, cache_v=cache_v, cache_kidx=cache_kidx,
                                   page_table=page_table, layer=l, k_top=k_top_sample)
        xs, rows_s = layer_forward(xs, p, pos_sample, state_lru_conv[l], state_lru_h[l], state_ssm_conv[l],
                                   state_ssm[l], attn_s, 0)
        sample_rows.append(rows_s)
    y_prompt = rmsnorm(xp, final_norm)[:, N_META:]
    y_sample = rmsnorm(xs, final_norm)
    k_p, v_p, kidx_p, ssm_p, ssm_conv_p, lru_h_p, lru_conv_p = _stack_layers(prompt_rows)
    k_s, v_s, kidx_s, ssm_s, ssm_conv_s, lru_h_s, lru_conv_s = _stack_layers(sample_rows)
    return (y_prompt, y_sample, k_p, v_p, kidx_p, ssm_p, ssm_conv_p, lru_h_p, lru_conv_p,
            k_s, v_s, kidx_s, ssm_s, ssm_conv_s, lru_h_s, lru_conv_s)
```

```python
import functools
import math

import numpy as np
import jax
import jax.numpy as jnp
from jax import lax
from jax.experimental import pallas as pl
from jax.experimental.pallas import tpu as pltpu

F32 = jnp.float32
BF16 = jnp.bfloat16
I32 = jnp.int32

N_META = 16
CONV_W = 4
EPS = 1e-6
LRU_C = 8.0
N_RNN_BLOCKS = 16
SSM_HEAD_DIM = 64
N_SSM_GROUPS = 4
SSM_STATE = 128
SSM_CHUNK = 128
HEAD_DIM = 64
N_KV_HEADS = 4
N_IDX_HEADS = 8
IDX_DIM = 64
TOP_K_MAX = 256
TOP_K_FRACTION = 4
ROPE_THETA = 10000.0
PAGE_SIZE = 128
N_BRANCH = 3

LANES = 128
ROW_TILE = 128
DENSE_TM = 512
VMEM_LIMIT = 56 * 1024 * 1024
INT_MIN = -(2 ** 31)
NEG_BIG = -1e30


def _cparams(sem):
    return pltpu.CompilerParams(dimension_semantics=sem, vmem_limit_bytes=VMEM_LIMIT)


def _rmsnorm_rows(x, g):
    return x * lax.rsqrt(jnp.mean(x * x, axis=-1, keepdims=True) + EPS) * g


def _norm_matmul_kernel(x_ref, g_ref, w_ref, o_ref, h_scr):
    @pl.when(pl.program_id(1) == 0)
    def _():
        h_scr[...] = _rmsnorm_rows(x_ref[...], g_ref[...]).astype(BF16)

    o_ref[...] = jnp.dot(h_scr[...], w_ref[...], preferred_element_type=F32)


def norm_matmul(x, g, w, tm, tn):
    m, d = x.shape
    n = w.shape[1]
    return pl.pallas_call(
        _norm_matmul_kernel,
        grid=(m // tm, n // tn),
        in_specs=[pl.BlockSpec((tm, d), lambda i, j: (i, 0)),
                  pl.BlockSpec((1, d), lambda i, j: (0, 0)),
                  pl.BlockSpec((d, tn), lambda i, j: (0, j))],
        out_specs=pl.BlockSpec((tm, tn), lambda i, j: (i, j)),
        out_shape=jax.ShapeDtypeStruct((m, n), F32),
        scratch_shapes=[pltpu.VMEM((tm, d), BF16)],
        compiler_params=_cparams(("parallel", "arbitrary")),
        name="inproj_main",
    )(x, g, w)


def _attn_proj_kernel(x_ref, g_ref, w_ref, cos_ref, sin_ref,
                      q_ref, qi_ref, k_ref, v_ref, small_ref, kb_ref, vb_ref, kidxb_ref,
                      *, n_heads, n_idx, n_kv):
    h = _rmsnorm_rows(x_ref[...], g_ref[...]).astype(BF16)
    acc = jnp.dot(h, w_ref[...], preferred_element_type=F32)
    cos = cos_ref[...]
    sin = sin_ref[...]
    lane = lax.broadcasted_iota(I32, cos.shape, 1)
    lo_half = (lane & (HEAD_DIM // 2)) == 0

    def rope(xb, c, s):
        partner = jnp.where(lo_half, pltpu.roll(xb, LANES - HEAD_DIM // 2, 1), pltpu.roll(xb, HEAD_DIM // 2, 1))
        return xb * c + partner * s

    col = 0
    scale = HEAD_DIM ** -0.5
    for p in range(n_heads // 2):
        blk = rope(acc[:, col:col + LANES], cos, sin) * scale
        q_ref[2 * p] = blk[:, :HEAD_DIM].astype(BF16)
        q_ref[2 * p + 1] = blk[:, HEAD_DIM:].astype(BF16)
        col += LANES
    for p in range(n_idx // 2):
        blk = rope(acc[:, col:col + LANES], cos, sin)
        qi_ref[2 * p] = blk[:, :IDX_DIM].astype(BF16)
        qi_ref[2 * p + 1] = blk[:, IDX_DIM:].astype(BF16)
        col += LANES
    kw = n_kv * HEAD_DIM
    for p in range(kw // LANES):
        blk = rope(acc[:, col:col + LANES], cos, sin)
        k_ref[:, p * LANES:(p + 1) * LANES] = blk
        kb_ref[:, p * LANES:(p + 1) * LANES] = blk.astype(BF16)
        col += LANES
    vblk = acc[:, col:col + kw]
    v_ref[...] = vblk
    vb_ref[...] = vblk.astype(BF16)
    col += kw
    first = lane < IDX_DIM
    blk = rope(acc[:, col:col + LANES], jnp.where(first, cos, 1.0), jnp.where(first, sin, 0.0))
    small_ref[...] = blk
    kidxb_ref[...] = blk[:, :IDX_DIM].astype(BF16)


def attn_proj(x, g, w, cos, sin, n_heads, n_idx, n_kv, tm):
    m, d = x.shape
    n = w.shape[1]
    kw = n_kv * HEAD_DIM
    row = lambda i: (i, 0)
    return pl.pallas_call(
        functools.partial(_attn_proj_kernel, n_heads=n_heads, n_idx=n_idx, n_kv=n_kv),
        grid=(m // tm,),
        in_specs=[pl.BlockSpec((tm, d), row),
                  pl.BlockSpec((1, d), lambda i: (0, 0)),
                  pl.BlockSpec((d, n), lambda i: (0, 0)),
                  pl.BlockSpec((tm, LANES), row),
                  pl.BlockSpec((tm, LANES), row)],
        out_specs=[pl.BlockSpec((n_heads, tm, HEAD_DIM), lambda i: (0, i, 0)),
                   pl.BlockSpec((n_idx, tm, IDX_DIM), lambda i: (0, i, 0)),
                   pl.BlockSpec((tm, kw), row),
                   pl.BlockSpec((tm, kw), row),
                   pl.BlockSpec((tm, LANES), row),
                   pl.BlockSpec((tm, kw), row),
                   pl.BlockSpec((tm, kw), row),
                   pl.BlockSpec((tm, IDX_DIM), row)],
        out_shape=[jax.ShapeDtypeStruct((n_heads, m, HEAD_DIM), BF16),
                   jax.ShapeDtypeStruct((n_idx, m, IDX_DIM), BF16),
                   jax.ShapeDtypeStruct((m, kw), F32),
                   jax.ShapeDtypeStruct((m, kw), F32),
                   jax.ShapeDtypeStruct((m, LANES), F32),
                   jax.ShapeDtypeStruct((m, kw), BF16),
                   jax.ShapeDtypeStruct((m, kw), BF16),
                   jax.ShapeDtypeStruct((m, IDX_DIM), BF16)],
        compiler_params=_cparams(("parallel",)),
        name="inproj_attn",
    )(x, g, w, cos, sin)


def _merge_kernel(ya_ref, yb_ref, yc_ref, g0_ref, g1_ref, g2_ref, x_ref, valid_ref,
                  pa_ref, pb_ref, pc_ref, wo_ref, o_ref):
    def branch(y_ref, p_ref, g_ref):
        return jax.nn.sigmoid(g_ref[...]) * jnp.dot(y_ref[...], p_ref[...], preferred_element_type=F32)

    merged = branch(ya_ref, pa_ref, g0_ref) + branch(yb_ref, pb_ref, g1_ref) + branch(yc_ref, pc_ref, g2_ref)
    xn = x_ref[...] + jnp.dot(merged.astype(BF16), wo_ref[...], preferred_element_type=F32)
    o_ref[...] = jnp.where(valid_ref[...] > 0.0, xn, 0.0)


def merge(ya, yb, yc, main, gate_col, x, valid, pa, pb, pc, wo, tm):
    m, d = x.shape
    row = lambda i: (i, 0)
    wspec = pl.BlockSpec((d, d), lambda i: (0, 0))
    return pl.pallas_call(
        _merge_kernel,
        grid=(m // tm,),
        in_specs=[pl.BlockSpec((tm, d), row), pl.BlockSpec((tm, d), row), pl.BlockSpec((tm, d), row),
                  pl.BlockSpec((tm, d), lambda i: (i, gate_col)),
                  pl.BlockSpec((tm, d), lambda i: (i, gate_col + 1)),
                  pl.BlockSpec((tm, d), lambda i: (i, gate_col + 2)),
                  pl.BlockSpec((tm, d), row),
                  pl.BlockSpec((tm, 1), row),
                  wspec, wspec, wspec, wspec],
        out_specs=pl.BlockSpec((tm, d), row),
        out_shape=jax.ShapeDtypeStruct((m, d), F32),
        compiler_params=_cparams(("parallel",)),
        name="merge",
    )(ya, yb, yc, main, main, main, x, valid, pa, pb, pc, wo)


def _mlp_kernel(x_ref, g_ref, wu_ref, wd_ref, o_ref, h_scr, acc_scr):
    j = pl.program_id(1)

    @pl.when(j == 0)
    def _():
        h_scr[...] = _rmsnorm_rows(x_ref[...], g_ref[...]).astype(BF16)
        acc_scr[...] = x_ref[...]

    u = jnp.dot(h_scr[...], wu_ref[...], preferred_element_type=F32)
    u = jnp.square(jnp.maximum(u, 0.0)).astype(BF16)
    acc_scr[...] += jnp.dot(u, wd_ref[...], preferred_element_type=F32)

    @pl.when(j == pl.num_programs(1) - 1)
    def _():
        o_ref[...] = acc_scr[...]


def mlp(x, g, wu, wd, tm, tf):
    m, d = x.shape
    f = wu.shape[1]
    return pl.pallas_call(
        _mlp_kernel,
        grid=(m // tm, f // tf),
        in_specs=[pl.BlockSpec((tm, d), lambda i, j: (i, 0)),
                  pl.BlockSpec((1, d), lambda i, j: (0, 0)),
                  pl.BlockSpec((d, tf), lambda i, j: (0, j)),
                  pl.BlockSpec((tf, d), lambda i, j: (j, 0))],
        out_specs=pl.BlockSpec((tm, d), lambda i, j: (i, 0)),
        out_shape=jax.ShapeDtypeStruct((m, d), F32),
        scratch_shapes=[pltpu.VMEM((tm, d), BF16), pltpu.VMEM((tm, d), F32)],
        compiler_params=_cparams(("parallel", "arbitrary")),
        name="mlp",
    )(x, g, wu, wd)


def _final_norm_kernel(x_ref, g_ref, o_ref):
    o_ref[...] = _rmsnorm_rows(x_ref[...], g_ref[...])


def final_rmsnorm(x, g, tm):
    m, d = x.shape
    return pl.pallas_call(
        _final_norm_kernel,
        grid=(m // tm,),
        in_specs=[pl.BlockSpec((tm, d), lambda i: (i, 0)), pl.BlockSpec((1, d), lambda i: (0, 0))],
        out_specs=pl.BlockSpec((tm, d), lambda i: (i, 0)),
        out_shape=jax.ShapeDtypeStruct((m, d), F32),
        compiler_params=_cparams(("parallel",)),
        name="final_norm",
    )(x, g)


def _softplus(x):
    return jnp.maximum(x, 0.0) + jnp.log(1.0 + jnp.exp(-jnp.abs(x)))


def _silu(x):
    return x * jax.nn.sigmoid(x)


def _gelu_tanh(x):
    return 0.5 * x * (1.0 + jnp.tanh(math.sqrt(2.0 / math.pi) * (x + 0.044715 * (x * x * x))))


def _causal_conv_rows(ext_ref, x, w_ref, b_ref, rows):
    ext_ref[8:8 + rows, :] = x
    out = b_ref[...] + w_ref[CONV_W - 1:CONV_W, :] * x
    for j in range(CONV_W - 1):
        out = out + w_ref[j:j + 1, :] * ext_ref[5 + j:5 + j + rows, :]
    ext_ref[5:8, :] = ext_ref[5 + rows:8 + rows, :]
    return out


def _lru_gates(u, wg_ref, ba_ref, bx_ref, clam_ref, n_grp):
    gw = u.shape[1] // n_grp
    rs, is_ = [], []
    for g in range(n_grp):
        rg = jnp.dot(u[:, g * gw:(g + 1) * gw].astype(BF16), wg_ref[g], preferred_element_type=F32)
        rs.append(rg[:, :gw])
        is_.append(rg[:, gw:])
    r = jax.nn.sigmoid(jnp.concatenate(rs, axis=1) + ba_ref[...])
    i = jax.nn.sigmoid(jnp.concatenate(is_, axis=1) + bx_ref[...])
    log_a = clam_ref[...] * r
    a = jnp.exp(log_a)
    b = jnp.sqrt(1.0 - jnp.exp(2.0 * log_a)) * (i * u)
    return a, b


def _lru_prompt_kernel(x_ref, g_ref, cw_ref, cb_ref, wg_ref, ba_ref, bx_ref, clam_ref,
                       y_ref, hout_ref, ext_scr, h_scr, *, pad, n_grp):
    c = pl.program_id(1)
    rows = x_ref.shape[0]

    @pl.when(c == 0)
    def _():
        ext_scr[0:8, :] = jnp.zeros((8, ext_scr.shape[1]), F32)
        h_scr[...] = jnp.zeros(h_scr.shape, F32)

    u = _causal_conv_rows(ext_scr, x_ref[...], cw_ref, cb_ref, rows)
    a, b = _lru_gates(u, wg_ref, ba_ref, bx_ref, clam_ref, n_grp)
    ridx = lax.broadcasted_iota(I32, (rows, 1), 0)
    b = jnp.where(c * rows + ridx >= pad, b, 0.0)
    s = 1
    while s < rows:
        keep = ridx >= s
        a_sh = jnp.where(keep, pltpu.roll(a, s, 0), 1.0)
        b_sh = jnp.where(keep, pltpu.roll(b, s, 0), 0.0)
        b = a * b_sh + b
        a = a * a_sh
        s *= 2
    h = a * h_scr[...] + b
    h_scr[...] = h[rows - 1:rows, :]
    hout_ref[0] = h[rows - 1:rows, :]
    y_ref[...] = (h * _gelu_tanh(g_ref[...])).astype(BF16)


def lru_prompt(main, xcol, gcol, nb, tp, pad, cw, cb, wg, ba, bx, clam):
    d = cw.shape[1]
    n_grp = wg.shape[0]
    nt = tp // ROW_TILE
    const2 = lambda b, c: (0, 0)
    return pl.pallas_call(
        functools.partial(_lru_prompt_kernel, pad=pad, n_grp=n_grp),
        grid=(nb, nt),
        in_specs=[pl.BlockSpec((ROW_TILE, d), lambda b, c: (b * nt + c, xcol)),
                  pl.BlockSpec((ROW_TILE, d), lambda b, c: (b * nt + c, gcol)),
                  pl.BlockSpec((CONV_W, d), const2),
                  pl.BlockSpec((1, d), const2),
                  pl.BlockSpec(wg.shape, lambda b, c: (0, 0, 0)),
                  pl.BlockSpec((1, d), const2),
                  pl.BlockSpec((1, d), const2),
                  pl.BlockSpec((1, d), const2)],
        out_specs=[pl.BlockSpec((ROW_TILE, d), lambda b, c: (b * nt + c, 0)),
                   pl.BlockSpec((1, 1, d), lambda b, c: (b, 0, 0))],
        out_shape=[jax.ShapeDtypeStruct((nb * tp, d), BF16),
                   jax.ShapeDtypeStruct((nb, 1, d), F32)],
        scratch_shapes=[pltpu.VMEM((8 + ROW_TILE, d), F32), pltpu.VMEM((1, d), F32)],
        compiler_params=_cparams(("parallel", "arbitrary")),
        name="lru_prompt",
    )(main, main, cw, cb, wg, ba, bx, clam)


DT_LANE = IDX_DIM


def _split3_bf16(x):
    hi = x.astype(BF16)
    r1 = x - hi.astype(F32)
    mid = r1.astype(BF16)
    lo = (r1 - mid.astype(F32)).astype(BF16)
    return hi, mid, lo


def _ssd_prompt_kernel(xbc_ref, z_ref, small_ref, cw_ref, cb_ref, dtb_ref, aneg_ref, dfull_ref, ng_ref,
                       y_ref, hout_ref, ext_scr, ht_scr, y_scr, *, pad, n_heads, n_grp):
    c = pl.program_id(1)
    rows = xbc_ref.shape[0]
    d_ssm = z_ref.shape[1]
    hpg = n_heads // n_grp
    P = SSM_HEAD_DIM
    N = SSM_STATE

    @pl.when(c == 0)
    def _():
        ext_scr[0:8, :] = jnp.zeros((8, ext_scr.shape[1]), F32)
        ht_scr[...] = jnp.zeros(ht_scr.shape, F32)

    xbc = _silu(_causal_conv_rows(ext_scr, xbc_ref[...], cw_ref, cb_ref, rows))
    xs = xbc[:, :d_ssm]
    ridx = lax.broadcasted_iota(I32, (rows, 1), 0)
    dt = _softplus(small_ref[...] + dtb_ref[...])
    dt = jnp.where(c * rows + ridx >= pad, dt, 0.0)
    da = dt * aneg_ref[...]
    ii = lax.broadcasted_iota(I32, (rows, rows), 0)
    jj = lax.broadcasted_iota(I32, (rows, rows), 1)
    causal = jj <= ii
    tri = jnp.where(causal, 1.0, 0.0).astype(BF16)
    cum = sum(jnp.dot(tri, part, preferred_element_type=F32) for part in _split3_bf16(da))
    cum_t = cum.T
    dt_t = dt.T
    cum_last = cum[rows - 1:rows, :]
    dec_end = jnp.exp(cum_last - cum) * dt
    ecum = jnp.exp(cum)
    chunk_dec = jnp.exp(cum_last)
    for g in range(n_grp):
        bg = xbc[:, d_ssm + g * N:d_ssm + (g + 1) * N]
        cg = xbc[:, d_ssm + n_grp * N + g * N:d_ssm + n_grp * N + (g + 1) * N]
        bg16 = bg.astype(BF16)
        cg16 = cg.astype(BF16)
        cb = lax.dot_general(cg16, bg16, (((1,), (1,)), ((), ())), preferred_element_type=F32)
        bgt16 = bg.T.astype(BF16)
        for hh in range(hpg):
            h = g * hpg + hh
            ln = DT_LANE + h
            x_h = xs[:, h * P:(h + 1) * P]
            seg = cum[:, ln:ln + 1] - cum_t[ln:ln + 1, :]
            decay = jnp.where(causal, jnp.exp(jnp.where(causal, seg, 0.0)), 0.0)
            w_intra = (cb * decay * dt_t[ln:ln + 1, :]).astype(BF16)
            y_h = jnp.dot(w_intra, x_h.astype(BF16), preferred_element_type=F32)
            ht = ht_scr[h]
            y_h = y_h + jnp.dot(cg16, ht.astype(BF16), preferred_element_type=F32) * ecum[:, ln:ln + 1]
            xw = (x_h * dec_end[:, ln:ln + 1]).astype(BF16)
            ht_new = chunk_dec[:, ln:ln + 1] * ht + jnp.dot(bgt16, xw, preferred_element_type=F32)
            ht_scr[h] = ht_new
            hout_ref[0, h] = ht_new
            y_scr[:, h * P:(h + 1) * P] = y_h
    y = y_scr[...] + dfull_ref[...] * xs
    y = y * _silu(z_ref[...])
    y_ref[...] = _rmsnorm_rows(y, ng_ref[...]).astype(BF16)


def ssd_prompt(main, xbc_col, z_col, small, nb, tp, pad, cw, cb, dtb, aneg, dfull, ng, n_heads, n_grp):
    d_xbc = cw.shape[1]
    d_ssm = ng.shape[1]
    nt = tp // ROW_TILE
    const2 = lambda b, c: (0, 0)
    return pl.pallas_call(
        functools.partial(_ssd_prompt_kernel, pad=pad, n_heads=n_heads, n_grp=n_grp),
        grid=(nb, nt),
        in_specs=[pl.BlockSpec((ROW_TILE, d_xbc), lambda b, c: (b * nt + c, xbc_col)),
                  pl.BlockSpec((ROW_TILE, d_ssm), lambda b, c: (b * nt + c, z_col)),
                  pl.BlockSpec((ROW_TILE, LANES), lambda b, c: (b * nt + c, 0)),
                  pl.BlockSpec((CONV_W, d_xbc), const2),
                  pl.BlockSpec((1, d_xbc), const2),
                  pl.BlockSpec((1, LANES), const2),
                  pl.BlockSpec((1, LANES), const2),
                  pl.BlockSpec((1, d_ssm), const2),
                  pl.BlockSpec((1, d_ssm), const2)],
        out_specs=[pl.BlockSpec((ROW_TILE, d_ssm), lambda b, c: (b * nt + c, 0)),
                   pl.BlockSpec((1, n_heads, SSM_STATE, SSM_HEAD_DIM), lambda b, c: (b, 0, 0, 0))],
        out_shape=[jax.ShapeDtypeStruct((nb * tp, d_ssm), BF16),
                   jax.ShapeDtypeStruct((nb, n_heads, SSM_STATE, SSM_HEAD_DIM), F32)],
        scratch_shapes=[pltpu.VMEM((8 + ROW_TILE, d_xbc), F32),
                        pltpu.VMEM((n_heads, SSM_STATE, SSM_HEAD_DIM), F32),
                        pltpu.VMEM((ROW_TILE, d_ssm), F32)],
        compiler_params=_cparams(("parallel", "arbitrary")),
        name="ssd_prompt",
    )(main, main, small, cw, cb, dtb, aneg, dfull, ng)


WIDX_LANE = IDX_DIM + 16


def _sortable_key(score):
    score = jnp.where(score == 0.0, 0.0, score)
    bits = pltpu.bitcast(score, I32)
    return jnp.where(bits < 0, bits ^ jnp.int32(0x7FFFFFFF), bits)


def _lane_blocks(x):
    return [x[:, u * LANES:(u + 1) * LANES] for u in range(x.shape[1] // LANES)]


def _row_total(x):
    return jnp.broadcast_to(jnp.sum(x, axis=1, keepdims=True), x.shape)


def _select_threshold(load_keys, n_chunks, rows, ktop, col_bits):
    zero = jnp.zeros((rows, LANES), I32)

    def count(pred):
        def body(c, cnt):
            kk = load_keys(c)
            col0 = c * kk.shape[1]
            for u, blk in enumerate(_lane_blocks(kk)):
                cnt = cnt + jnp.where(pred(blk, col0 + u * LANES), 1, 0)
            return cnt
        return _row_total(lax.fori_loop(0, n_chunks, body, zero))

    def t_step(it, t):
        cand = t + lax.shift_left(jnp.int32(1), 31 - it)
        n_ge = count(lambda blk, col0: blk >= cand)
        return jnp.where(n_ge >= ktop, cand, t)

    t = lax.fori_loop(0, 32, t_step, jnp.full((rows, LANES), INT_MIN, I32))
    n_ge = count(lambda blk, col0: blk >= t)
    n_gt = count(lambda blk, col0: blk > t)
    live = t != INT_MIN
    excess = live & (n_ge > ktop)
    need = ktop - n_gt
    lane = lax.broadcasted_iota(I32, (rows, LANES), 1)

    def j_search():
        def j_step(it, j):
            cand = j + lax.shift_left(jnp.int32(1), col_bits - 1 - it)
            n_tie = count(lambda blk, col0: (blk == t) & (lane + col0 < cand))
            return jnp.where(n_tie <= need, cand, j)
        return lax.fori_loop(0, col_bits, j_step, zero)

    any_excess = jnp.max(jnp.where(excess, 1, 0)) > 0
    j_cut = lax.cond(any_excess, j_search, lambda: zero)
    big = jnp.int32(2 ** 30)
    j_cut = jnp.where(excess, j_cut, jnp.where(live, big, 0))
    return t, j_cut


def _selected(keys, t, j_cut, col0):
    lane = lax.broadcasted_iota(I32, keys.shape, 1)
    return (keys > t) | ((keys == t) & (lane + col0 < j_cut))


def _index_scores(qi, w, kidx):
    rows, n_h = w.shape
    s = lax.dot_general(qi, kidx, (((1,), (1,)), ((), ())), preferred_element_type=F32)
    s = jnp.maximum(s, 0.0)
    acc = w[:, 0:1] * s[0:rows]
    for h in range(1, n_h):
        acc = acc + w[:, h:h + 1] * s[h * rows:(h + 1) * rows]
    return acc


def _attend_chunk(qexp_scr, kc, vc, bias, m_scr, l_scr, acc_scr, n_kv, rows_per_group):
    r4 = rows_per_group
    reps = r4 // bias.shape[0]
    bias_blocks = _lane_blocks(bias)
    for g in range(n_kv):
        rs = slice(g * r4, (g + 1) * r4)
        logits = lax.dot_general(qexp_scr[rs, :].astype(BF16), kc, (((1,), (1,)), ((), ())),
                                 preferred_element_type=F32)
        blocks = [blk + jnp.concatenate([bb] * reps, axis=0)
                  for blk, bb in zip(_lane_blocks(logits), bias_blocks)]
        mx = blocks[0]
        for blk in blocks[1:]:
            mx = jnp.maximum(mx, blk)
        m_prev = m_scr[rs, :]
        m_new = jnp.maximum(m_prev, jnp.broadcast_to(jnp.max(mx, axis=1, keepdims=True), mx.shape))
        alpha = jnp.exp(m_prev - m_new)
        ps = [jnp.exp(blk - m_new) for blk in blocks]
        psum = ps[0]
        for p in ps[1:]:
            psum = psum + p
        l_scr[rs, :] = alpha * l_scr[rs, :] + _row_total(psum)
        m_scr[rs, :] = m_new
        p16 = jnp.concatenate(ps, axis=1).astype(BF16)
        pv = jnp.dot(p16, vc, preferred_element_type=F32)
        acc_scr[rs, :] = jnp.concatenate([alpha] * (acc_scr.shape[1] // LANES), axis=1) * acc_scr[rs, :] + pv


def _dsa_prompt_kernel(q_ref, qi_ref, small_ref, kb_ref, vb_ref, kidx_ref, o_ref,
                       key_scr, qexp_scr, m_scr, l_scr, acc_scr, *, ktop, pad, tkc, col_bits):
    i = pl.program_id(1)
    n_q, tq, hd = q_ref.shape
    n_idx = qi_ref.shape[0]
    n_kv = kb_ref.shape[1] // hd
    qpk = n_q // n_kv
    r4 = qpk * tq
    q0 = i * tq
    n_chunks = (q0 + tq + tkc - 1) // tkc

    w = small_ref[:, WIDX_LANE:WIDX_LANE + n_idx] * ((n_idx * IDX_DIM) ** -0.5)
    qi = qi_ref[...].reshape(n_idx * tq, qi_ref.shape[2])
    row = q0 + lax.broadcasted_iota(I32, (tq, tkc), 0)

    def score_chunk(c, carry):
        k0 = pl.multiple_of(c * tkc, tkc)
        sc = _index_scores(qi, w, kidx_ref[pl.ds(k0, tkc), :])
        col = k0 + lax.broadcasted_iota(I32, (tq, tkc), 1)
        admissible = (col >= pad) & (col <= row)
        key_scr[c] = jnp.where(admissible, _sortable_key(sc), INT_MIN)
        return carry

    lax.fori_loop(0, n_chunks, score_chunk, 0)
    t, j_cut = _select_threshold(lambda c: key_scr[c], n_chunks, tq, ktop, col_bits)

    qexp_scr[...] = jnp.zeros(qexp_scr.shape, BF16)
    for h in range(n_q):
        g = h // qpk
        qexp_scr[h * tq:(h + 1) * tq, g * hd:(g + 1) * hd] = q_ref[h]
    m_scr[...] = jnp.full(m_scr.shape, NEG_BIG, F32)
    l_scr[...] = jnp.zeros(l_scr.shape, F32)
    acc_scr[...] = jnp.zeros(acc_scr.shape, F32)

    def attend(c, carry):
        k0 = pl.multiple_of(c * tkc, tkc)
        kk = key_scr[c]
        bias = jnp.concatenate(
            [jnp.where(_selected(blk, t, j_cut, k0 + u * LANES), 0.0, NEG_BIG)
             for u, blk in enumerate(_lane_blocks(kk))], axis=1)
        _attend_chunk(qexp_scr, kb_ref[pl.ds(k0, tkc), :], vb_ref[pl.ds(k0, tkc), :], bias,
                      m_scr, l_scr, acc_scr, n_kv, r4)
        return carry

    lax.fori_loop(0, n_chunks, attend, 0)
    for h in range(n_q):
        g = h // qpk
        rs = slice(h * tq, (h + 1) * tq)
        out = acc_scr[rs, g * hd:(g + 1) * hd] / l_scr[rs, 0:hd]
        o_ref[:, h * hd:(h + 1) * hd] = out.astype(BF16)


def dsa_prompt(q, qi, small, kb, vb, kidxb, nb, tp, pad, ktop, tkc):
    n_q, _, hd = q.shape
    n_idx = qi.shape[0]
    kw = kb.shape[1]
    tq = ROW_TILE
    nt = tp // tq
    col_bits = max(1, int(math.ceil(math.log2(tp))))
    return pl.pallas_call(
        functools.partial(_dsa_prompt_kernel, ktop=ktop, pad=pad, tkc=tkc, col_bits=col_bits),
        grid=(nb, nt),
        in_specs=[pl.BlockSpec((n_q, tq, hd), lambda b, i: (0, b * nt + i, 0)),
                  pl.BlockSpec((n_idx, tq, IDX_DIM), lambda b, i: (0, b * nt + i, 0)),
                  pl.BlockSpec((tq, LANES), lambda b, i: (b * nt + i, 0)),
                  pl.BlockSpec((tp, kw), lambda b, i: (b, 0)),
                  pl.BlockSpec((tp, kw), lambda b, i: (b, 0)),
                  pl.BlockSpec((tp, IDX_DIM), lambda b, i: (b, 0))],
        out_specs=pl.BlockSpec((tq, n_q * hd), lambda b, i: (b * nt + i, 0)),
        out_shape=jax.ShapeDtypeStruct((nb * tp, n_q * hd), BF16),
        scratch_shapes=[pltpu.VMEM((tp // tkc, tq, tkc), I32),
                        pltpu.VMEM((n_q * tq, kw), BF16),
                        pltpu.VMEM((n_q * tq, LANES), F32),
                        pltpu.VMEM((n_q * tq, LANES), F32),
                        pltpu.VMEM((n_q * tq, kw), F32)],
        compiler_params=_cparams(("parallel", "arbitrary")),
        name="dsa_prompt",
    )(q, qi, small, kb, vb, kidxb)


def _lru_sample_kernel(x_ref, g_ref, hist_ref, h0_ref, cw_ref, cb_ref, wg_ref, ba_ref, bx_ref, clam_ref,
                       y_ref, hout_ref, convout_ref, hist_scr, h_scr, *, n_grp):
    t = pl.program_id(0)

    @pl.when(t == 0)
    def _():
        hist_scr[...] = hist_ref[...]
        h_scr[...] = h0_ref[...]

    x = x_ref[...]
    u = cb_ref[...] + cw_ref[CONV_W - 1:CONV_W, :] * x
    for j in range(CONV_W - 1):
        u = u + cw_ref[j:j + 1, :] * hist_scr[j]
    for j in range(CONV_W - 2):
        hist_scr[j] = hist_scr[j + 1]
    hist_scr[CONV_W - 2] = x
    a, b = _lru_gates(u, wg_ref, ba_ref, bx_ref, clam_ref, n_grp)
    h = a * h_scr[...] + b
    h_scr[...] = h
    y_ref[...] = (h * _gelu_tanh(g_ref[...])).astype(BF16)
    hout_ref[...] = h
    convout_ref[...] = hist_scr[...]


def lru_sample(main, xcol, gcol, row0, db, ds, hist, h0, cw, cb, wg, ba, bx, clam):
    d = cw.shape[1]
    n_grp = wg.shape[0]
    blk0 = row0 // db
    const2 = lambda t: (0, 0)
    const3 = lambda t: (0, 0, 0)
    return pl.pallas_call(
        functools.partial(_lru_sample_kernel, n_grp=n_grp),
        grid=(ds,),
        in_specs=[pl.BlockSpec((db, d), lambda t: (blk0 + t, xcol)),
                  pl.BlockSpec((db, d), lambda t: (blk0 + t, gcol)),
                  pl.BlockSpec((CONV_W - 1, db, d), const3),
                  pl.BlockSpec((db, d), const2),
                  pl.BlockSpec((CONV_W, d), const2),
                  pl.BlockSpec((1, d), const2),
                  pl.BlockSpec(wg.shape, const3),
                  pl.BlockSpec((1, d), const2),
                  pl.BlockSpec((1, d), const2),
                  pl.BlockSpec((1, d), const2)],
        out_specs=[pl.BlockSpec((db, d), lambda t: (t, 0)),
                   pl.BlockSpec((db, d), const2),
                   pl.BlockSpec((CONV_W - 1, db, d), const3)],
        out_shape=[jax.ShapeDtypeStruct((ds * db, d), BF16),
                   jax.ShapeDtypeStruct((db, d), F32),
                   jax.ShapeDtypeStruct((CONV_W - 1, db, d), F32)],
        scratch_shapes=[pltpu.VMEM((CONV_W - 1, db, d), F32), pltpu.VMEM((db, d), F32)],
        compiler_params=_cparams(("arbitrary",)),
        name="lru_sample",
    )(main, main, hist, h0, cw, cb, wg, ba, bx, clam)


def _ssd_sample_kernel(*refs, ds, n_heads, n_grp):
    xbc_refs = refs[0:ds]
    z_refs = refs[ds:2 * ds]
    small_refs = refs[2 * ds:3 * ds]
    (hist_ref, h0_ref, cw_ref, cb_ref, dtb_ref, aneg_ref, dfull_ref, ng_ref, exp_ref,
     y_ref, hout_ref, convout_ref, yint_scr) = refs[3 * ds:]
    bb = xbc_refs[0].shape[0]
    d_ssm = z_refs[0].shape[1]
    hpg = n_heads // n_grp
    P = SSM_HEAD_DIM
    N = SSM_STATE
    gw = hpg * P

    ext = [hist_ref[j] for j in range(CONV_W - 1)] + [r[...] for r in xbc_refs]
    for j in range(CONV_W - 1):
        convout_ref[j] = ext[ds + j]
    xbc = []
    for t in range(ds):
        u = cb_ref[...]
        for j in range(CONV_W):
            u = u + cw_ref[j:j + 1, :] * ext[t + j]
        xbc.append(_silu(u))
    xs = [v[:, :d_ssm] for v in xbc]
    bm = [v[:, d_ssm:d_ssm + n_grp * N] for v in xbc]
    cm = [v[:, d_ssm + n_grp * N:] for v in xbc]
    dt = [_softplus(r[...] + dtb_ref[...]) for r in small_refs]
    cum = []
    run = jnp.zeros_like(dt[0])
    for t in range(ds):
        run = run + dt[t] * aneg_ref[...]
        cum.append(run)
    lane = lax.broadcasted_iota(I32, (bb, LANES), 1)
    coefs = []
    for t in range(ds):
        for j in range(t + 1):
            cbh = jnp.zeros((bb, LANES), F32)
            for g in range(n_grp):
                dotg = jnp.sum(cm[t][:, g * N:(g + 1) * N] * bm[j][:, g * N:(g + 1) * N], axis=1, keepdims=True)
                in_g = (lane >= DT_LANE + g * hpg) & (lane < DT_LANE + (g + 1) * hpg)
                cbh = cbh + jnp.where(in_g, dotg, 0.0)
            coefs.append(cbh * jnp.exp(cum[t] - cum[j]) * dt[j])
    n_intra = len(coefs)
    coefs += [jnp.exp(c) for c in cum]
    coefs += [jnp.exp(cum[ds - 1] - cum[j]) * dt[j] for j in range(ds)]
    coefs.append(jnp.exp(cum[ds - 1]))
    stack = jnp.concatenate(coefs, axis=0)
    wide = sum(jnp.dot(part, exp_ref[...], preferred_element_type=F32) for part in _split3_bf16(stack))
    wide = [wide[k * bb:(k + 1) * bb] for k in range(len(coefs))]
    intra_w = wide[:n_intra]
    ecum_w = wide[n_intra:n_intra + ds]
    dend_w = wide[n_intra + ds:n_intra + 2 * ds]
    cdec_w = wide[n_intra + 2 * ds]
    xw = [dend_w[j] * xs[j] for j in range(ds)]
    zpad = 8 - ds
    for b in range(bb):
        for g in range(n_grp):
            def rows_of(arrs, lo, width):
                parts = [a[b:b + 1, lo:lo + width] for a in arrs]
                if zpad:
                    parts.append(jnp.zeros((zpad, width), F32))
                return jnp.concatenate(parts, axis=0).astype(BF16)
            hg = h0_ref[b, g * hpg:(g + 1) * hpg].reshape(gw, N)
            yint = lax.dot_general(rows_of(cm, g * N, N), hg.astype(BF16), (((1,), (1,)), ((), ())),
                                   preferred_element_type=F32)
            for t in range(ds):
                yint_scr[t, b:b + 1, g * gw:(g + 1) * gw] = yint[t:t + 1]
            upd = lax.dot_general(rows_of(xw, g * gw, gw), rows_of(bm, g * N, N), (((0,), (0,)), ((), ())),
                                  preferred_element_type=F32)
            for hh in range(hpg):
                h = g * hpg + hh
                cd = cdec_w[b:b + 1, h * P:h * P + 1]
                hout_ref[b, h] = cd * hg[hh * P:(hh + 1) * P] + upd[hh * P:(hh + 1) * P]
    k = 0
    for t in range(ds):
        y = ecum_w[t] * yint_scr[t] + dfull_ref[...] * xs[t]
        for j in range(t + 1):
            y = y + intra_w[k] * xs[j]
            k += 1
        y = y * _silu(z_refs[t][...])
        y_ref[t] = _rmsnorm_rows(y, ng_ref[...]).astype(BF16)


def ssd_sample(main, xbc_col, z_col, small, row0, db, ds, bb, hist, h0, cw, cb, dtb, aneg, dfull, ng, expand,
               n_heads, n_grp):
    d_xbc = cw.shape[1]
    d_ssm = ng.shape[1]
    const2 = lambda i: (0, 0)

    def at_t(t, width, col):
        blk0 = (row0 + t * db) // bb
        return pl.BlockSpec((bb, width), lambda i: (blk0 + i, col))

    in_specs = ([at_t(t, d_xbc, xbc_col) for t in range(ds)] + [at_t(t, d_ssm, z_col) for t in range(ds)]
                + [at_t(t, LANES, 0) for t in range(ds)]
                + [pl.BlockSpec((CONV_W - 1, bb, d_xbc), lambda i: (0, i, 0)),
                   pl.BlockSpec((bb, n_heads, SSM_HEAD_DIM, SSM_STATE), lambda i: (i, 0, 0, 0)),
                   pl.BlockSpec((CONV_W, d_xbc), const2),
                   pl.BlockSpec((1, d_xbc), const2),
                   pl.BlockSpec((1, LANES), const2),
                   pl.BlockSpec((1, LANES), const2),
                   pl.BlockSpec((1, d_ssm), const2),
                   pl.BlockSpec((1, d_ssm), const2),
                   pl.BlockSpec((LANES, d_ssm), const2)])
    return pl.pallas_call(
        functools.partial(_ssd_sample_kernel, ds=ds, n_heads=n_heads, n_grp=n_grp),
        grid=(db // bb,),
        in_specs=in_specs,
        out_specs=[pl.BlockSpec((ds, bb, d_ssm), lambda i: (0, i, 0)),
                   pl.BlockSpec((bb, n_heads, SSM_HEAD_DIM, SSM_STATE), lambda i: (i, 0, 0, 0)),
                   pl.BlockSpec((CONV_W - 1, bb, d_xbc), lambda i: (0, i, 0))],
        out_shape=[jax.ShapeDtypeStruct((ds, db, d_ssm), BF16),
                   jax.ShapeDtypeStruct((db, n_heads, SSM_HEAD_DIM, SSM_STATE), F32),
                   jax.ShapeDtypeStruct((CONV_W - 1, db, d_xbc), F32)],
        scratch_shapes=[pltpu.VMEM((ds, bb, d_ssm), F32)],
        compiler_params=_cparams(("parallel",)),
        name="ssd_sample",
    )(*([main] * ds + [main] * ds + [small] * ds + [hist, h0, cw, cb, dtb, aneg, dfull, ng, expand]))


def _dsa_sample_kernel(pt_ref, *refs, n_pages, ds, ktop):
    del pt_ref
    q_ref, qi_ref, w_ref, knew_ref, vnew_ref, kxnew_ref = refs[0:6]
    kx_pages = refs[6:6 + n_pages]
    k_pages = refs[6 + n_pages:6 + 2 * n_pages]
    v_pages = refs[6 + 2 * n_pages:6 + 3 * n_pages]
    o_ref, key_scr, qexp_scr, m_scr, l_scr, acc_scr, new_scr = refs[6 + 3 * n_pages:]
    n_q, rq, hd = q_ref.shape[1:]
    n_idx = qi_ref.shape[1]
    n_kv = knew_ref.shape[2] // hd
    qpk = n_q // n_kv
    r4 = qpk * rq
    n_chunks = n_pages + 1

    w = w_ref[0] * ((n_idx * IDX_DIM) ** -0.5)
    qi = qi_ref[0].reshape(n_idx * rq, qi_ref.shape[3]).astype(BF16)
    qrow = lax.broadcasted_iota(I32, (rq, PAGE_SIZE), 0)
    lane = lax.broadcasted_iota(I32, (rq, PAGE_SIZE), 1)
    for p in range(n_pages):
        sc = _index_scores(qi, w, kx_pages[p][0].astype(BF16))
        key_scr[0, :, p * PAGE_SIZE:(p + 1) * PAGE_SIZE] = jnp.where(qrow < ds, _sortable_key(sc), INT_MIN)
    new_scr[...] = jnp.zeros(new_scr.shape, F32)
    new_scr[0:rq, 0:kxnew_ref.shape[2]] = kxnew_ref[0]
    sc = _index_scores(qi, w, new_scr[:, 0:kxnew_ref.shape[2]].astype(BF16))
    admissible = (lane <= qrow) & (qrow < ds)
    key_scr[0, :, n_pages * PAGE_SIZE:] = jnp.where(admissible, _sortable_key(sc), INT_MIN)
    col_bits = max(1, int(math.ceil(math.log2(n_chunks * PAGE_SIZE))))
    t, j_cut = _select_threshold(lambda c: key_scr[c], 1, rq, ktop, col_bits)

    qexp_scr[...] = jnp.zeros(qexp_scr.shape, F32)
    for h in range(n_q):
        g = h // qpk
        qexp_scr[h * rq:(h + 1) * rq, g * hd:(g + 1) * hd] = q_ref[0, h]
    m_scr[...] = jnp.full(m_scr.shape, NEG_BIG, F32)
    l_scr[...] = jnp.zeros(l_scr.shape, F32)
    acc_scr[...] = jnp.zeros(acc_scr.shape, F32)
    for p in range(n_chunks):
        kk = key_scr[0, :, p * PAGE_SIZE:(p + 1) * PAGE_SIZE]
        bias = jnp.where(_selected(kk, t, j_cut, p * PAGE_SIZE), 0.0, NEG_BIG)
        if p < n_pages:
            kc = k_pages[p][0].astype(BF16)
            vc = v_pages[p][0].astype(BF16)
        else:
            new_scr[0:rq, :] = knew_ref[0]
            kc = new_scr[...].astype(BF16)
            new_scr[0:rq, :] = vnew_ref[0]
            vc = new_scr[...].astype(BF16)
        _attend_chunk(qexp_scr, kc, vc, bias, m_scr, l_scr, acc_scr, n_kv, r4)
    for h in range(n_q):
        g = h // qpk
        rs = slice(h * rq, (h + 1) * rq)
        o_ref[0, h] = acc_scr[rs, g * hd:(g + 1) * hd] / l_scr[rs, 0:hd]


def dsa_sample(page_table, q, qi, w, knew, vnew, kxnew, cache_k, cache_v, cache_kidx, layer, n_pool, ds, ktop):
    db, n_q, rq, hd = q.shape
    n_idx = qi.shape[1]
    kw = knew.shape[2]
    n_pages = page_table.shape[1]

    def page_spec(width, p):
        return pl.BlockSpec((1, PAGE_SIZE, width), lambda b, pt: (layer * n_pool + pt[b, p], 0, 0))

    own3 = lambda b, pt: (b, 0, 0)
    own4 = lambda b, pt: (b, 0, 0, 0)
    in_specs = ([pl.BlockSpec((1, n_q, rq, hd), own4),
                 pl.BlockSpec((1, n_idx, rq, IDX_DIM), own4),
                 pl.BlockSpec((1, rq, n_idx), own3),
                 pl.BlockSpec((1, rq, kw), own3),
                 pl.BlockSpec((1, rq, kw), own3),
                 pl.BlockSpec((1, rq, IDX_DIM), own3)]
                + [page_spec(IDX_DIM, p) for p in range(n_pages)]
                + [page_spec(kw, p) for p in range(n_pages)]
                + [page_spec(kw, p) for p in range(n_pages)])
    grid_spec = pltpu.PrefetchScalarGridSpec(
        num_scalar_prefetch=1,
        grid=(db,),
        in_specs=in_specs,
        out_specs=pl.BlockSpec((1, n_q, rq, hd), own4),
        scratch_shapes=[pltpu.VMEM((1, rq, (n_pages + 1) * PAGE_SIZE), I32),
                        pltpu.VMEM((n_q * rq, kw), F32),
                        pltpu.VMEM((n_q * rq, LANES), F32),
                        pltpu.VMEM((n_q * rq, LANES), F32),
                        pltpu.VMEM((n_q * rq, kw), F32),
                        pltpu.VMEM((PAGE_SIZE, kw), F32)])
    return pl.pallas_call(
        functools.partial(_dsa_sample_kernel, n_pages=n_pages, ds=ds, ktop=ktop),
        grid_spec=grid_spec,
        out_shape=jax.ShapeDtypeStruct((db, n_q, rq, hd), F32),
        compiler_params=_cparams(("arbitrary",)),
        name="dsa_sample",
    )(page_table, q, qi, w, knew, vnew, kxnew,
      *([cache_kidx] * n_pages + [cache_k] * n_pages + [cache_v] * n_pages))


def _round_up(x, m):
    return (x + m - 1) // m * m


def _largest_tile(total, unit, cap):
    best = unit
    for k in range(1, cap // unit + 1):
        if total % (k * unit) == 0:
            best = k * unit
    return best


def _block_diag_groups(w, per_group):
    nb, r, _ = w.shape
    ng = nb // per_group
    w = w.reshape(ng, per_group, r, r)
    eye = jnp.eye(per_group, dtype=w.dtype)
    return jnp.einsum('gbij,bc->gbicj', w, eye).reshape(ng, per_group * r, per_group * r)


def _pad_rows_to(a, axis, n):
    pad = [(0, 0)] * a.ndim
    pad[axis] = (0, n - a.shape[axis])
    return jnp.pad(a, pad)


def kernel(x_prompt, x_sample, cache_k, cache_v, cache_kidx, state_ssm, state_ssm_conv, state_lru_h, state_lru_conv, page_table, meta_tokens, norm1, w_in, lru_conv_w, lru_conv_b, lru_wa, lru_ba, lru_wx, lru_bx, lru_lambda, ssm_conv_w, ssm_conv_b, ssm_dt_bias, ssm_a_log, ssm_d, ssm_norm, p_lru, p_ssm, p_attn, w_o, norm2, w_up, w_down, final_norm):
    nb, seq, d = x_prompt.shape
    db, ds, _ = x_sample.shape
    depth = w_in.shape[0]
    n_pages = page_table.shape[1]
    past = n_pages * PAGE_SIZE
    n_pool = cache_k.shape[1]
    n_kv = cache_k.shape[3]
    kw = n_kv * HEAD_DIM
    d_rnn = lru_conv_w.shape[2]
    d_xbc = ssm_conv_w.shape[2]
    d_ssm = ssm_norm.shape[1]
    n_ssm_heads = ssm_d.shape[1]
    n_heads = p_attn.shape[1] // HEAD_DIM
    n_idx = N_IDX_HEADS
    d_ff = w_up.shape[2]
    assert d_rnn == d and d_ssm == d and d_xbc == 2 * d and n_heads * HEAD_DIM == d
    assert seq % ROW_TILE == 0 and ds <= 8 and n_ssm_heads <= LANES - DT_LANE

    pad = (-N_META) % ROW_TILE
    tp = pad + N_META + seq
    mp = nb * tp
    ms = db * ds
    mt = _round_up(mp + ms, 2 * DENSE_TM)
    bb = 8
    assert mp % db == 0 and db % bb == 0
    ktop_p = min(TOP_K_MAX, seq // TOP_K_FRACTION)
    ktop_s = min(TOP_K_MAX, (past + ds) // TOP_K_FRACTION)
    tkc = _largest_tile(tp, LANES, 640)

    xp = jnp.concatenate([jnp.zeros((nb, pad, d), F32),
                          jnp.broadcast_to(meta_tokens[None].astype(F32), (nb, N_META, d)),
                          x_prompt], axis=1).reshape(mp, d)
    xs = jnp.swapaxes(x_sample, 0, 1).reshape(ms, d)
    x = jnp.concatenate([xp, xs, jnp.zeros((mt - mp - ms, d), F32)], axis=0)
    r = np.arange(mt)
    in_prompt = r < mp
    rp = r % tp
    valid_np = np.where(in_prompt, rp >= pad, r < mp + ms)
    pos_np = np.where(in_prompt, np.maximum(rp - pad, 0), np.where(r < mp + ms, past + (r - mp) // db, 0))
    valid = jnp.asarray(valid_np.astype(np.float32)).reshape(mt, 1)
    half = HEAD_DIM // 2
    freq = ROPE_THETA ** (-jnp.arange(half, dtype=F32) / half)
    ang = jnp.asarray(pos_np).astype(F32)[:, None] * freq[None, :]
    cos = jnp.cos(ang)
    sin = jnp.sin(ang)
    cos_t = jnp.concatenate([cos] * (LANES // half), axis=1)
    sin_t = jnp.concatenate([-sin, sin] * (LANES // HEAD_DIM), axis=1)

    splits = (d_rnn, d_rnn, d_ssm, d_xbc, n_ssm_heads, n_heads * HEAD_DIM, kw, kw, n_idx * IDX_DIM, IDX_DIM,
              n_idx, N_BRANCH * d)
    assert sum(splits) == w_in.shape[2]
    off = np.concatenate([[0], np.cumsum(splits)])
    seg = lambda w, i: w[:, off[i]:off[i + 1]]
    LRU_X, LRU_G, SSM_Z, SSM_XBC, SSM_DT, Q, K, V, QI, KIDX, WIDX, GATES = range(12)
    XBC_COL, X_COL, G_COL, Z_COL, GATE_COL = 0, 2, 3, 4, 5
    lane_pos = np.arange(LANES)
    head_lanes = (lane_pos >= DT_LANE) & (lane_pos < DT_LANE + n_ssm_heads)
    expand_np = np.zeros((LANES, d_ssm), np.float32)
    for h in range(n_ssm_heads):
        expand_np[DT_LANE + h, h * SSM_HEAD_DIM:(h + 1) * SSM_HEAD_DIM] = 1.0
    expand = jnp.asarray(expand_np).astype(BF16)

    def to_head_lanes(v):
        return jnp.zeros((1, LANES), F32).at[0, DT_LANE:DT_LANE + n_ssm_heads].set(v)

    outs_p = [[] for _ in range(7)]
    outs_s = [[] for _ in range(7)]
    for l in range(depth):
        w = w_in[l]
        w_main = jnp.concatenate([seg(w, SSM_XBC), seg(w, LRU_X), seg(w, LRU_G), seg(w, SSM_Z), seg(w, GATES)],
                                 axis=1).astype(BF16)
        tail = LANES - IDX_DIM - n_ssm_heads - n_idx
        w_attn = jnp.concatenate([seg(w, Q), seg(w, QI), seg(w, K), seg(w, V), seg(w, KIDX), seg(w, SSM_DT),
                                  seg(w, WIDX), jnp.zeros((d, tail), F32)], axis=1).astype(BF16)
        per_group = 256 // (d_rnn // N_RNN_BLOCKS)
        wg = jnp.concatenate([_block_diag_groups(lru_wa[l], per_group), _block_diag_groups(lru_wx[l], per_group)],
                             axis=2).astype(BF16)
        clam = (-LRU_C * jax.nn.softplus(-lru_lambda[l])).reshape(1, d_rnn)
        lru_args = (lru_conv_w[l], lru_conv_b[l].reshape(1, -1), wg, lru_ba[l].reshape(1, -1),
                    lru_bx[l].reshape(1, -1), clam)
        dtb = to_head_lanes(ssm_dt_bias[l])
        aneg = to_head_lanes(-jnp.exp(ssm_a_log[l]))
        dfull = jnp.repeat(ssm_d[l], SSM_HEAD_DIM).reshape(1, d_ssm)
        ssm_args = (ssm_conv_w[l], ssm_conv_b[l].reshape(1, -1), dtb, aneg, dfull, ssm_norm[l].reshape(1, -1))

        main = norm_matmul(x, norm1[l].reshape(1, d), w_main, 2 * DENSE_TM, 2048)
        q, qi, kf, vf, small, kb, vb, kidxb = attn_proj(x, norm1[l].reshape(1, d), w_attn, cos_t, sin_t,
                                                        n_heads, n_idx, n_kv, DENSE_TM)
        ya_p, hlru_p = lru_prompt(main, X_COL, G_COL, nb, tp, pad, *lru_args)
        yb_p, hssm_p = ssd_prompt(main, XBC_COL, Z_COL, small, nb, tp, pad, *ssm_args, n_ssm_heads, N_SSM_GROUPS)
        yc_p = dsa_prompt(q, qi, small, kb, vb, kidxb, nb, tp, pad, ktop_p, tkc)
        ya_s, hlru_s, conv_lru_s = lru_sample(main, X_COL, G_COL, mp, db, ds,
                                              jnp.swapaxes(state_lru_conv[l], 0, 1), state_lru_h[l], *lru_args)
        yb_s, hssm_s, conv_ssm_s = ssd_sample(main, XBC_COL, Z_COL, small, mp, db, ds, bb,
                                              jnp.swapaxes(state_ssm_conv[l], 0, 1), state_ssm[l], *ssm_args,
                                              expand, n_ssm_heads, N_SSM_GROUPS)

        def batch_major(a):
            lead = a.shape[:-2]
            a = a.astype(F32).reshape(lead + (ds, db, a.shape[-1]))
            a = jnp.moveaxis(a, len(lead) + 1, 0)
            return _pad_rows_to(a, a.ndim - 2, 8)

        small_s = batch_major(small[mp:mp + ms])
        yc_s = dsa_sample(page_table, batch_major(q[:, mp:mp + ms]), batch_major(qi[:, mp:mp + ms]),
                          small_s[:, :, WIDX_LANE:WIDX_LANE + n_idx], batch_major(kf[mp:mp + ms]),
                          batch_major(vf[mp:mp + ms]), small_s[:, :, :IDX_DIM],
                          cache_k.reshape(depth * n_pool, PAGE_SIZE, kw),
                          cache_v.reshape(depth * n_pool, PAGE_SIZE, kw),
                          cache_kidx.reshape(depth * n_pool, PAGE_SIZE, IDX_DIM), l, n_pool, ds, ktop_s)
        yc_s = jnp.transpose(yc_s[:, :, :ds], (2, 0, 1, 3)).reshape(ms, n_heads * HEAD_DIM).astype(BF16)

        tail_rows = jnp.zeros((mt - mp - ms, d), BF16)
        ya = jnp.concatenate([ya_p, ya_s, tail_rows], axis=0)
        yb = jnp.concatenate([yb_p, yb_s.reshape(ms, d_ssm), tail_rows], axis=0)
        yc = jnp.concatenate([yc_p, yc_s, tail_rows], axis=0)
        x = merge(ya, yb, yc, main, GATE_COL, x, valid, p_lru[l].astype(BF16), p_ssm[l].astype(BF16),
                  p_attn[l].astype(BF16), w_o[l].astype(BF16), DENSE_TM)
        x = mlp(x, norm2[l].reshape(1, d), w_up[l].astype(BF16), w_down[l].astype(BF16), 2 * DENSE_TM, 1024)

        prm = lambda a: a[:mp].reshape((nb, tp) + a.shape[1:])
        smp = lambda a: jnp.swapaxes(a[mp:mp + ms].reshape((ds, db) + a.shape[1:]), 0, 1)
        last = CONV_W - 1
        rows_p = (prm(kf)[:, pad:].reshape(nb, tp - pad, n_kv, HEAD_DIM),
                  prm(vf)[:, pad:].reshape(nb, tp - pad, n_kv, HEAD_DIM),
                  prm(small)[:, pad:, :IDX_DIM],
                  jnp.swapaxes(hssm_p, 2, 3),
                  prm(main)[:, tp - last:, XBC_COL * d:XBC_COL * d + d_xbc],
                  hlru_p[:, 0],
                  prm(main)[:, tp - last:, X_COL * d:X_COL * d + d_rnn])
        rows_s = (smp(kf).reshape(db, ds, n_kv, HEAD_DIM),
                  smp(vf).reshape(db, ds, n_kv, HEAD_DIM),
                  smp(small)[:, :, :IDX_DIM],
                  hssm_s,
                  jnp.swapaxes(conv_ssm_s, 0, 1),
                  hlru_s,
                  jnp.swapaxes(conv_lru_s, 0, 1))
        for acc, val in zip(outs_p, rows_p):
            acc.append(val)
        for acc, val in zip(outs_s, rows_s):
            acc.append(val)

    y = final_rmsnorm(x, final_norm.reshape(1, d), DENSE_TM)
    y_prompt = y[:mp].reshape(nb, tp, d)[:, pad + N_META:]
    y_sample = jnp.swapaxes(y[mp:mp + ms].reshape(ds, db, d), 0, 1)
    return (y_prompt, y_sample) + tuple(jnp.stack(a) for a in outs_p) + tuple(jnp.stack(a) for a in outs_s)
```

```python
import functools
import math

import numpy as np
import jax
import jax.numpy as jnp
from jax import lax
from jax.experimental import pallas as pl
from jax.experimental.pallas import tpu as pltpu

F32 = jnp.float32
BF16 = jnp.bfloat16
I32 = jnp.int32

N_META = 16
CONV_W = 4
EPS = 1e-6
LRU_C = 8.0
N_RNN_BLOCKS = 16
SSM_HEAD_DIM = 64
N_SSM_GROUPS = 4
SSM_STATE = 128
SSM_CHUNK = 128
HEAD_DIM = 64
N_KV_HEADS = 4
N_IDX_HEADS = 8
IDX_DIM = 64
TOP_K_MAX = 256
TOP_K_FRACTION = 4
ROPE_THETA = 10000.0
PAGE_SIZE = 128
N_BRANCH = 3

LANES = 128
ROW_TILE = 128
DENSE_TM = 512
VMEM_LIMIT = 56 * 1024 * 1024
INT_MIN = -(2 ** 31)
NEG_BIG = -1e30


def _cparams(sem):
    return pltpu.CompilerParams(dimension_semantics=sem, vmem_limit_bytes=VMEM_LIMIT)


def _rmsnorm_rows(x, g):
    return x * lax.rsqrt(jnp.mean(x * x, axis=-1, keepdims=True) + EPS) * g


def _norm_matmul_kernel(x_ref, g_ref, w_ref, o_ref, h_scr):
    @pl.when(pl.program_id(1) == 0)
    def _():
        h_scr[...] = _rmsnorm_rows(x_ref[...], g_ref[...]).astype(BF16)

    o_ref[...] = jnp.dot(h_scr[...], w_ref[...], preferred_element_type=F32)


def norm_matmul(x, g, w, tm, tn):
    m, d = x.shape
    n = w.shape[1]
    return pl.pallas_call(
        _norm_matmul_kernel,
        grid=(m // tm, n // tn),
        in_specs=[pl.BlockSpec((tm, d), lambda i, j: (i, 0)),
                  pl.BlockSpec((1, d), lambda i, j: (0, 0)),
                  pl.BlockSpec((d, tn), lambda i, j: (0, j))],
        out_specs=pl.BlockSpec((tm, tn), lambda i, j: (i, j)),
        out_shape=jax.ShapeDtypeStruct((m, n), F32),
        scratch_shapes=[pltpu.VMEM((tm, d), BF16)],
        compiler_params=_cparams(("parallel", "arbitrary")),
        name="inproj_main",
    )(x, g, w)


def _attn_proj_kernel(x_ref, g_ref, w_ref, cos_ref, sin_ref,
                      q_ref, qi_ref, k_ref, v_ref, small_ref, kb_ref, vb_ref, kidxb_ref,
                      *, n_heads, n_idx, n_kv):
    h = _rmsnorm_rows(x_ref[...], g_ref[...]).astype(BF16)
    acc = jnp.dot(h, w_ref[...], preferred_element_type=F32)
    cos = cos_ref[...]
    sin = sin_ref[...]
    lane = lax.broadcasted_iota(I32, cos.shape, 1)
    lo_half = (lane & (HEAD_DIM // 2)) == 0

    def rope(xb, c, s):
        partner = jnp.where(lo_half, pltpu.roll(xb, LANES - HEAD_DIM // 2, 1), pltpu.roll(xb, HEAD_DIM // 2, 1))
        return xb * c + partner * s

    col = 0
    scale = HEAD_DIM ** -0.5
    for p in range(n_heads // 2):
        blk = rope(acc[:, col:col + LANES], cos, sin) * scale
        q_ref[2 * p] = blk[:, :HEAD_DIM].astype(BF16)
        q_ref[2 * p + 1] = blk[:, HEAD_DIM:].astype(BF16)
        col += LANES
    for p in range(n_idx // 2):
        blk = rope(acc[:, col:col + LANES], cos, sin)
        qi_ref[2 * p] = blk[:, :IDX_DIM].astype(BF16)
        qi_ref[2 * p + 1] = blk[:, IDX_DIM:].astype(BF16)
        col += LANES
    kw = n_kv * HEAD_DIM
    for p in range(kw // LANES):
        blk = rope(acc[:, col:col + LANES], cos, sin)
        k_ref[:, p * LANES:(p + 1) * LANES] = blk
        kb_ref[:, p * LANES:(p + 1) * LANES] = blk.astype(BF16)
        col += LANES
    vblk = acc[:, col:col + kw]
    v_ref[...] = vblk
    vb_ref[...] = vblk.astype(BF16)
    col += kw
    first = lane < IDX_DIM
    blk = rope(acc[:, col:col + LANES], jnp.where(first, cos, 1.0), jnp.where(first, sin, 0.0))
    small_ref[...] = blk
    kidxb_ref[...] = blk[:, :IDX_DIM].astype(BF16)


def attn_proj(x, g, w, cos, sin, n_heads, n_idx, n_kv, tm):
    m, d = x.shape
    n = w.shape[1]
    kw = n_kv * HEAD_DIM
    row = lambda i: (i, 0)
    return pl.pallas_call(
        functools.partial(_attn_proj_kernel, n_heads=n_heads, n_idx=n_idx, n_kv=n_kv),
        grid=(m // tm,),
        in_specs=[pl.BlockSpec((tm, d), row),
                  pl.BlockSpec((1, d), lambda i: (0, 0)),
                  pl.BlockSpec((d, n), lambda i: (0, 0)),
                  pl.BlockSpec((tm, LANES), row),
                  pl.BlockSpec((tm, LANES), row)],
        out_specs=[pl.BlockSpec((n_heads, tm, HEAD_DIM), lambda i: (0, i, 0)),
                   pl.BlockSpec((n_idx, tm, IDX_DIM), lambda i: (0, i, 0)),
                   pl.BlockSpec((tm, kw), row),
                   pl.BlockSpec((tm, kw), row),
                   pl.BlockSpec((tm, LANES), row),
                   pl.BlockSpec((tm, kw), row),
                   pl.BlockSpec((tm, kw), row),
                   pl.BlockSpec((tm, IDX_DIM), row)],
        out_shape=[jax.ShapeDtypeStruct((n_heads, m, HEAD_DIM), BF16),
                   jax.ShapeDtypeStruct((n_idx, m, IDX_DIM), BF16),
                   jax.ShapeDtypeStruct((m, kw), F32),
                   jax.ShapeDtypeStruct((m, kw), F32),
                   jax.ShapeDtypeStruct((m, LANES), F32),
                   jax.ShapeDtypeStruct((m, kw), BF16),
                   jax.ShapeDtypeStruct((m, kw), BF16),
                   jax.ShapeDtypeStruct((m, IDX_DIM), BF16)],
        compiler_params=_cparams(("parallel",)),
        name="inproj_attn",
    )(x, g, w, cos, sin)


def _merge_kernel(ya_ref, yb_ref, yc_ref, g0_ref, g1_ref, g2_ref, x_ref, valid_ref,
                  pa_ref, pb_ref, pc_ref, wo_ref, o_ref):
    def branch(y_ref, p_ref, g_ref):
        return jax.nn.sigmoid(g_ref[...]) * jnp.dot(y_ref[...], p_ref[...], preferred_element_type=F32)

    merged = branch(ya_ref, pa_ref, g0_ref) + branch(yb_ref, pb_ref, g1_ref) + branch(yc_ref, pc_ref, g2_ref)
    xn = x_ref[...] + jnp.dot(merged.astype(BF16), wo_ref[...], preferred_element_type=F32)
    o_ref[...] = jnp.where(valid_ref[...] > 0.0, xn, 0.0)


def merge(ya, yb, yc, main, gate_col, x, valid, pa, pb, pc, wo, tm):
    m, d = x.shape
    row = lambda i: (i, 0)
    wspec = pl.BlockSpec((d, d), lambda i: (0, 0))
    return pl.pallas_call(
        _merge_kernel,
        grid=(m // tm,),
        in_specs=[pl.BlockSpec((tm, d), row), pl.BlockSpec((tm, d), row), pl.BlockSpec((tm, d), row),
                  pl.BlockSpec((tm, d), lambda i: (i, gate_col)),
                  pl.BlockSpec((tm, d), lambda i: (i, gate_col + 1)),
                  pl.BlockSpec((tm, d), lambda i: (i, gate_col + 2)),
                  pl.BlockSpec((tm, d), row),
                  pl.BlockSpec((tm, 1), row),
                  wspec, wspec, wspec, wspec],
        out_specs=pl.BlockSpec((tm, d), row),
        out_shape=jax.ShapeDtypeStruct((m, d), F32),
        compiler_params=_cparams(("parallel",)),
        name="merge",
    )(ya, yb, yc, main, main, main, x, valid, pa, pb, pc, wo)


def _mlp_kernel(x_ref, g_ref, wu_ref, wd_ref, o_ref, h_scr, acc_scr):
    j = pl.program_id(1)

    @pl.when(j == 0)
    def _():
        h_scr[...] = _rmsnorm_rows(x_ref[...], g_ref[...]).astype(BF16)
        acc_scr[...] = x_ref[...]

    u = jnp.dot(h_scr[...], wu_ref[...], preferred_element_type=F32)
    u = jnp.square(jnp.maximum(u, 0.0)).astype(BF16)
    acc_scr[...] += jnp.dot(u, wd_ref[...], preferred_element_type=F32)

    @pl.when(j == pl.num_programs(1) - 1)
    def _():
        o_ref[...] = acc_scr[...]


def mlp(x, g, wu, wd, tm, tf):
    m, d = x.shape
    f = wu.shape[1]
    return pl.pallas_call(
        _mlp_kernel,
        grid=(m // tm, f // tf),
        in_specs=[pl.BlockSpec((tm, d), lambda i, j: (i, 0)),
                  pl.BlockSpec((1, d), lambda i, j: (0, 0)),
                  pl.BlockSpec((d, tf), lambda i, j: (0, j)),
                  pl.BlockSpec((tf, d), lambda i, j: (j, 0))],
        out_specs=pl.BlockSpec((tm, d), lambda i, j: (i, 0)),
        out_shape=jax.ShapeDtypeStruct((m, d), F32),
        scratch_shapes=[pltpu.VMEM((tm, d), BF16), pltpu.VMEM((tm, d), F32)],
        compiler_params=_cparams(("parallel", "arbitrary")),
        name="mlp",
    )(x, g, wu, wd)


def _final_norm_kernel(x_ref, g_ref, o_ref):
    o_ref[...] = _rmsnorm_rows(x_ref[...], g_ref[...])


def final_rmsnorm(x, g, tm):
    m, d = x.shape
    return pl.pallas_call(
        _final_norm_kernel,
        grid=(m // tm,),
        in_specs=[pl.BlockSpec((tm, d), lambda i: (i, 0)), pl.BlockSpec((1, d), lambda i: (0, 0))],
        out_specs=pl.BlockSpec((tm, d), lambda i: (i, 0)),
        out_shape=jax.ShapeDtypeStruct((m, d), F32),
        compiler_params=_cparams(("parallel",)),
        name="final_norm",
    )(x, g)


def _softplus(x):
    return jnp.maximum(x, 0.0) + jnp.log(1.0 + jnp.exp(-jnp.abs(x)))


def _silu(x):
    return x * jax.nn.sigmoid(x)


def _gelu_tanh(x):
    return 0.5 * x * (1.0 + jnp.tanh(math.sqrt(2.0 / math.pi) * (x + 0.044715 * (x * x * x))))


def _causal_conv_rows(ext_ref, x, w_ref, b_ref, rows):
    ext_ref[8:8 + rows, :] = x
    out = b_ref[...] + w_ref[CONV_W - 1:CONV_W, :] * x
    for j in range(CONV_W - 1):
        out = out + w_ref[j:j + 1, :] * ext_ref[5 + j:5 + j + rows, :]
    ext_ref[5:8, :] = ext_ref[5 + rows:8 + rows, :]
    return out


def _lru_gates(u, wg_ref, ba_ref, bx_ref, clam_ref, n_grp):
    gw = u.shape[1] // n_grp
    rs, is_ = [], []
    for g in range(n_grp):
        rg = jnp.dot(u[:, g * gw:(g + 1) * gw].astype(BF16), wg_ref[g], preferred_element_type=F32)
        rs.append(rg[:, :gw])
        is_.append(rg[:, gw:])
    r = jax.nn.sigmoid(jnp.concatenate(rs, axis=1) + ba_ref[...])
    i = jax.nn.sigmoid(jnp.concatenate(is_, axis=1) + bx_ref[...])
    log_a = clam_ref[...] * r
    a = jnp.exp(log_a)
    b = jnp.sqrt(1.0 - jnp.exp(2.0 * log_a)) * (i * u)
    return a, b


def _lru_prompt_kernel(x_ref, g_ref, cw_ref, cb_ref, wg_ref, ba_ref, bx_ref, clam_ref,
                       y_ref, hout_ref, ext_scr, h_scr, *, pad, n_grp):
    c = pl.program_id(1)
    rows = x_ref.shape[0]

    @pl.when(c == 0)
    def _():
        ext_scr[0:8, :] = jnp.zeros((8, ext_scr.shape[1]), F32)
        h_scr[...] = jnp.zeros(h_scr.shape, F32)

    u = _causal_conv_rows(ext_scr, x_ref[...], cw_ref, cb_ref, rows)
    a, b = _lru_gates(u, wg_ref, ba_ref, bx_ref, clam_ref, n_grp)
    ridx = lax.broadcasted_iota(I32, (rows, 1), 0)
    b = jnp.where(c * rows + ridx >= pad, b, 0.0)
    s = 1
    while s < rows:
        keep = ridx >= s
        a_sh = jnp.where(keep, pltpu.roll(a, s, 0), 1.0)
        b_sh = jnp.where(keep, pltpu.roll(b, s, 0), 0.0)
        b = a * b_sh + b
        a = a * a_sh
        s *= 2
    h = a * h_scr[...] + b
    h_scr[...] = h[rows - 1:rows, :]
    hout_ref[0] = h[rows - 1:rows, :]
    y_ref[...] = (h * _gelu_tanh(g_ref[...])).astype(BF16)


def lru_prompt(main, xcol, gcol, nb, tp, pad, cw, cb, wg, ba, bx, clam):
    d = cw.shape[1]
    n_grp = wg.shape[0]
    nt = tp // ROW_TILE
    const2 = lambda b, c: (0, 0)
    return pl.pallas_call(
        functools.partial(_lru_prompt_kernel, pad=pad, n_grp=n_grp),
        grid=(nb, nt),
        in_specs=[pl.BlockSpec((ROW_TILE, d), lambda b, c: (b * nt + c, xcol)),
                  pl.BlockSpec((ROW_TILE, d), lambda b, c: (b * nt + c, gcol)),
                  pl.BlockSpec((CONV_W, d), const2),
                  pl.BlockSpec((1, d), const2),
                  pl.BlockSpec(wg.shape, lambda b, c: (0, 0, 0)),
                  pl.BlockSpec((1, d), const2),
                  pl.BlockSpec((1, d), const2),
                  pl.BlockSpec((1, d), const2)],
        out_specs=[pl.BlockSpec((ROW_TILE, d), lambda b, c: (b * nt + c, 0)),
                   pl.BlockSpec((1, 1, d), lambda b, c: (b, 0, 0))],
        out_shape=[jax.ShapeDtypeStruct((nb * tp, d), BF16),
                   jax.ShapeDtypeStruct((nb, 1, d), F32)],
        scratch_shapes=[pltpu.VMEM((8 + ROW_TILE, d), F32), pltpu.VMEM((1, d), F32)],
        compiler_params=_cparams(("parallel", "arbitrary")),
        name="lru_prompt",
    )(main, main, cw, cb, wg, ba, bx, clam)


DT_LANE = IDX_DIM


def _split3_bf16(x):
    hi = x.astype(BF16)
    r1 = x - hi.astype(F32)
    mid = r1.astype(BF16)
    lo = (r1 - mid.astype(F32)).astype(BF16)
    return hi, mid, lo


def _ssd_prompt_kernel(xbc_ref, z_ref, small_ref, cw_ref, cb_ref, dtb_ref, aneg_ref, dfull_ref, ng_ref,
                       y_ref, hout_ref, ext_scr, ht_scr, y_scr, *, pad, n_heads, n_grp):
    c = pl.program_id(1)
    rows = xbc_ref.shape[0]
    d_ssm = z_ref.shape[1]
    hpg = n_heads // n_grp
    P = SSM_HEAD_DIM
    N = SSM_STATE

    @pl.when(c == 0)
    def _():
        ext_scr[0:8, :] = jnp.zeros((8, ext_scr.shape[1]), F32)
        ht_scr[...] = jnp.zeros(ht_scr.shape, F32)

    xbc = _silu(_causal_conv_rows(ext_scr, xbc_ref[...], cw_ref, cb_ref, rows))
    xs = xbc[:, :d_ssm]
    ridx = lax.broadcasted_iota(I32, (rows, 1), 0)
    dt = _softplus(small_ref[...] + dtb_ref[...])
    dt = jnp.where(c * rows + ridx >= pad, dt, 0.0)
    da = dt * aneg_ref[...]
    ii = lax.broadcasted_iota(I32, (rows, rows), 0)
    jj = lax.broadcasted_iota(I32, (rows, rows), 1)
    causal = jj <= ii
    tri = jnp.where(causal, 1.0, 0.0).astype(BF16)
    cum = sum(jnp.dot(tri, part, preferred_element_type=F32) for part in _split3_bf16(da))
    cum_t = cum.T
    dt_t = dt.T
    cum_last = cum[rows - 1:rows, :]
    dec_end = jnp.exp(cum_last - cum) * dt
    ecum = jnp.exp(cum)
    chunk_dec = jnp.exp(cum_last)
    for g in range(n_grp):
        bg = xbc[:, d_ssm + g * N:d_ssm + (g + 1) * N]
        cg = xbc[:, d_ssm + n_grp * N + g * N:d_ssm + n_grp * N + (g + 1) * N]
        bg16 = bg.astype(BF16)
        cg16 = cg.astype(BF16)
        cb = lax.dot_general(cg16, bg16, (((1,), (1,)), ((), ())), preferred_element_type=F32)
        bgt16 = bg.T.astype(BF16)
        for hh in range(hpg):
            h = g * hpg + hh
            ln = DT_LANE + h
            x_h = xs[:, h * P:(h + 1) * P]
            seg = cum[:, ln:ln + 1] - cum_t[ln:ln + 1, :]
            decay = jnp.where(causal, jnp.exp(jnp.where(causal, seg, 0.0)), 0.0)
            w_intra = (cb * decay * dt_t[ln:ln + 1, :]).astype(BF16)
            y_h = jnp.dot(w_intra, x_h.astype(BF16), preferred_element_type=F32)
            ht = ht_scr[h]
            y_h = y_h + jnp.dot(cg16, ht.astype(BF16), preferred_element_type=F32) * ecum[:, ln:ln + 1]
            xw = (x_h * dec_end[:, ln:ln + 1]).astype(BF16)
            ht_new = chunk_dec[:, ln:ln + 1] * ht + jnp.dot(bgt16, xw, preferred_element_type=F32)
            ht_scr[h] = ht_new
            hout_ref[0, h] = ht_new
            y_scr[:, h * P:(h + 1) * P] = y_h
    y = y_scr[...] + dfull_ref[...] * xs
    y = y * _silu(z_ref[...])
    y_ref[...] = _rmsnorm_rows(y, ng_ref[...]).astype(BF16)


def ssd_prompt(main, xbc_col, z_col, small, nb, tp, pad, cw, cb, dtb, aneg, dfull, ng, n_heads, n_grp):
    d_xbc = cw.shape[1]
    d_ssm = ng.shape[1]
    nt = tp // ROW_TILE
    const2 = lambda b, c: (0, 0)
    return pl.pallas_call(
        functools.partial(_ssd_prompt_kernel, pad=pad, n_heads=n_heads, n_grp=n_grp),
        grid=(nb, nt),
        in_specs=[pl.BlockSpec((ROW_TILE, d_xbc), lambda b, c: (b * nt + c, xbc_col)),
                  pl.BlockSpec((ROW_TILE, d_ssm), lambda b, c: (b * nt + c, z_col)),
                  pl.BlockSpec((ROW_TILE, LANES), lambda b, c: (b * nt + c, 0)),
                  pl.BlockSpec((CONV_W, d_xbc), const2),
                  pl.BlockSpec((1, d_xbc), const2),
                  pl.BlockSpec((1, LANES), const2),
                  pl.BlockSpec((1, LANES), const2),
                  pl.BlockSpec((1, d_ssm), const2),
                  pl.BlockSpec((1, d_ssm), const2)],
        out_specs=[pl.BlockSpec((ROW_TILE, d_ssm), lambda b, c: (b * nt + c, 0)),
                   pl.BlockSpec((1, n_heads, SSM_STATE, SSM_HEAD_DIM), lambda b, c: (b, 0, 0, 0))],
        out_shape=[jax.ShapeDtypeStruct((nb * tp, d_ssm), BF16),
                   jax.ShapeDtypeStruct((nb, n_heads, SSM_STATE, SSM_HEAD_DIM), F32)],
        scratch_shapes=[pltpu.VMEM((8 + ROW_TILE, d_xbc), F32),
                        pltpu.VMEM((n_heads, SSM_STATE, SSM_HEAD_DIM), F32),
                        pltpu.VMEM((ROW_TILE, d_ssm), F32)],
        compiler_params=_cparams(("parallel", "arbitrary")),
        name="ssd_prompt",
    )(main, main, small, cw, cb, dtb, aneg, dfull, ng)


WIDX_LANE = IDX_DIM + 16


def _sortable_key(score):
    score = jnp.where(score == 0.0, 0.0, score)
    bits = pltpu.bitcast(score, I32)
    return jnp.where(bits < 0, bits ^ jnp.int32(0x7FFFFFFF), bits)


def _lane_blocks(x):
    return [x[:, u * LANES:(u + 1) * LANES] for u in range(x.shape[1] // LANES)]


def _row_total(x):
    return jnp.broadcast_to(jnp.sum(x, axis=1, keepdims=True), x.shape)


def _select_threshold(load_keys, n_chunks, rows, ktop, col_bits):
    zero = jnp.zeros((rows, LANES), I32)

    def count(pred):
        def body(c, cnt):
            kk = load_keys(c)
            col0 = c * kk.shape[1]
            for u, blk in enumerate(_lane_blocks(kk)):
                cnt = cnt + jnp.where(pred(blk, col0 + u * LANES), 1, 0)
            return cnt
        return _row_total(lax.fori_loop(0, n_chunks, body, zero))

    def t_step(it, t):
        cand = t + lax.shift_left(jnp.int32(1), 31 - it)
        n_ge = count(lambda blk, col0: blk >= cand)
        return jnp.where(n_ge >= ktop, cand, t)

    t = lax.fori_loop(0, 32, t_step, jnp.full((rows, LANES), INT_MIN, I32))
    n_ge = count(lambda blk, col0: blk >= t)
    n_gt = count(lambda blk, col0: blk > t)
    live = t != INT_MIN
    excess = live & (n_ge > ktop)
    need = ktop - n_gt
    lane = lax.broadcasted_iota(I32, (rows, LANES), 1)

    def j_search():
        def j_step(it, j):
            cand = j + lax.shift_left(jnp.int32(1), col_bits - 1 - it)
            n_tie = count(lambda blk, col0: (blk == t) & (lane + col0 < cand))
            return jnp.where(n_tie <= need, cand, j)
        return lax.fori_loop(0, col_bits, j_step, zero)

    any_excess = jnp.max(jnp.where(excess, 1, 0)) > 0
    j_cut = lax.cond(any_excess, j_search, lambda: zero)
    big = jnp.int32(2 ** 30)
    j_cut = jnp.where(excess, j_cut, jnp.where(live, big, 0))
    return t, j_cut


def _selected(keys, t, j_cut, col0):
    lane = lax.broadcasted_iota(I32, keys.shape, 1)
    return (keys > t) | ((keys == t) & (lane + col0 < j_cut))


def _index_scores(qi, w, kidx):
    rows, n_h = w.shape
    s = lax.dot_general(qi, kidx, (((1,), (1,)), ((), ())), preferred_element_type=F32)
    s = jnp.maximum(s, 0.0)
    acc = w[:, 0:1] * s[0:rows]
    for h in range(1, n_h):
        acc = acc + w[:, h:h + 1] * s[h * rows:(h + 1) * rows]
    return acc


def _attend_chunk(qexp_scr, kc, vc, bias, m_scr, l_scr, acc_scr, n_kv, rows_per_group):
    r4 = rows_per_group
    reps = r4 // bias.shape[0]
    bias_blocks = _lane_blocks(bias)
    for g in range(n_kv):
        rs = slice(g * r4, (g + 1) * r4)
        logits = lax.dot_general(qexp_scr[rs, :].astype(BF16), kc, (((1,), (1,)), ((), ())),
                                 preferred_element_type=F32)
        blocks = [blk + jnp.concatenate([bb] * reps, axis=0)
                  for blk, bb in zip(_lane_blocks(logits), bias_blocks)]
        mx = blocks[0]
        for blk in blocks[1:]:
            mx = jnp.maximum(mx, blk)
        m_prev = m_scr[rs, :]
        m_new = jnp.maximum(m_prev, jnp.broadcast_to(jnp.max(mx, axis=1, keepdims=True), mx.shape))
        alpha = jnp.exp(m_prev - m_new)
        ps = [jnp.exp(blk - m_new) for blk in blocks]
        psum = ps[0]
        for p in ps[1:]:
            psum = psum + p
        l_scr[rs, :] = alpha * l_scr[rs, :] + _row_total(psum)
        m_scr[rs, :] = m_new
        p16 = jnp.concatenate(ps, axis=1).astype(BF16)
        pv = jnp.dot(p16, vc, preferred_element_type=F32)
        acc_scr[rs, :] = jnp.concatenate([alpha] * (acc_scr.shape[1] // LANES), axis=1) * acc_scr[rs, :] + pv


def _dsa_prompt_kernel(q_ref, qi_ref, small_ref, kb_ref, vb_ref, kidx_ref, o_ref,
                       key_scr, qexp_scr, m_scr, l_scr, acc_scr, *, ktop, pad, tkc, col_bits):
    i = pl.program_id(1)
    n_q, tq, hd = q_ref.shape
    n_idx = qi_ref.shape[0]
    n_kv = kb_ref.shape[1] // hd
    qpk = n_q // n_kv
    r4 = qpk * tq
    q0 = i * tq
    n_chunks = (q0 + tq + tkc - 1) // tkc

    w = small_ref[:, WIDX_LANE:WIDX_LANE + n_idx] * ((n_idx * IDX_DIM) ** -0.5)
    qi = qi_ref[...].reshape(n_idx * tq, qi_ref.shape[2])
    row = q0 + lax.broadcasted_iota(I32, (tq, tkc), 0)

    def score_chunk(c, carry):
        k0 = pl.multiple_of(c * tkc, tkc)
        sc = _index_scores(qi, w, kidx_ref[pl.ds(k0, tkc), :])
        col = k0 + lax.broadcasted_iota(I32, (tq, tkc), 1)
        admissible = (col >= pad) & (col <= row)
        key_scr[c] = jnp.where(admissible, _sortable_key(sc), INT_MIN)
        return carry

    lax.fori_loop(0, n_chunks, score_chunk, 0)
    t, j_cut = _select_threshold(lambda c: key_scr[c], n_chunks, tq, ktop, col_bits)

    qexp_scr[...] = jnp.zeros(qexp_scr.shape, BF16)
    for h in range(n_q):
        g = h // qpk
        qexp_scr[h * tq:(h + 1) * tq, g * hd:(g + 1) * hd] = q_ref[h]
    m_scr[...] = jnp.full(m_scr.shape, NEG_BIG, F32)
    l_scr[...] = jnp.zeros(l_scr.shape, F32)
    acc_scr[...] = jnp.zeros(acc_scr.shape, F32)

    def attend(c, carry):
        k0 = pl.multiple_of(c * tkc, tkc)
        kk = key_scr[c]
        bias = jnp.concatenate(
            [jnp.where(_selected(blk, t, j_cut, k0 + u * LANES), 0.0, NEG_BIG)
             for u, blk in enumerate(_lane_blocks(kk))], axis=1)
        _attend_chunk(qexp_scr, kb_ref[pl.ds(k0, tkc), :], vb_ref[pl.ds(k0, tkc), :], bias,
                      m_scr, l_scr, acc_scr, n_kv, r4)
        return carry

    lax.fori_loop(0, n_chunks, attend, 0)
    for h in range(n_q):
        g = h // qpk
        rs = slice(h * tq, (h + 1) * tq)
        out = acc_scr[rs, g * hd:(g + 1) * hd] / l_scr[rs, 0:hd]
        o_ref[:, h * hd:(h + 1) * hd] = out.astype(BF16)


def dsa_prompt(q, qi, small, kb, vb, kidxb, nb, tp, pad, ktop, tkc):
    n_q, _, hd = q.shape
    n_idx = qi.shape[0]
    kw = kb.shape[1]
    tq = ROW_TILE
    nt = tp // tq
    col_bits = max(1, int(math.ceil(math.log2(tp))))
    return pl.pallas_call(
        functools.partial(_dsa_prompt_kernel, ktop=ktop, pad=pad, tkc=tkc, col_bits=col_bits),
        grid=(nb, nt),
        in_specs=[pl.BlockSpec((n_q, tq, hd), lambda b, i: (0, b * nt + i, 0)),
                  pl.BlockSpec((n_idx, tq, IDX_DIM), lambda b, i: (0, b * nt + i, 0)),
                  pl.BlockSpec((tq, LANES), lambda b, i: (b * nt + i, 0)),
                  pl.BlockSpec((tp, kw), lambda b, i: (b, 0)),
                  pl.BlockSpec((tp, kw), lambda b, i: (b, 0)),
                  pl.BlockSpec((tp, IDX_DIM), lambda b, i: (b, 0))],
        out_specs=pl.BlockSpec((tq, n_q * hd), lambda b, i: (b * nt + i, 0)),
        out_shape=jax.ShapeDtypeStruct((nb * tp, n_q * hd), BF16),
        scratch_shapes=[pltpu.VMEM((tp // tkc, tq, tkc), I32),
                        pltpu.VMEM((n_q * tq, kw), BF16),
                        pltpu.VMEM((n_q * tq, LANES), F32),
                        pltpu.VMEM((n_q * tq, LANES), F32),
                        pltpu.VMEM((n_q * tq, kw), F32)],
        compiler_params=_cparams(("parallel", "arbitrary")),
        name="dsa_prompt",
    )(q, qi, small, kb, vb, kidxb)


def _lru_sample_kernel(x_ref, g_ref, hist_ref, h0_ref, cw_ref, cb_ref, wg_ref, ba_ref, bx_ref, clam_ref,
                       y_ref, hout_ref, convout_ref, hist_scr, h_scr, *, n_grp):
    t = pl.program_id(0)

    @pl.when(t == 0)
    def _():
        hist_scr[...] = hist_ref[...]
        h_scr[...] = h0_ref[...]

    x = x_ref[...]
    u = cb_ref[...] + cw_ref[CONV_W - 1:CONV_W, :] * x
    for j in range(CONV_W - 1):
        u = u + cw_ref[j:j + 1, :] * hist_scr[j]
    for j in range(CONV_W - 2):
        hist_scr[j] = hist_scr[j + 1]
    hist_scr[CONV_W - 2] = x
    a, b = _lru_gates(u, wg_ref, ba_ref, bx_ref, clam_ref, n_grp)
    h = a * h_scr[...] + b
    h_scr[...] = h
    y_ref[...] = (h * _gelu_tanh(g_ref[...])).astype(BF16)
    hout_ref[...] = h
    convout_ref[...] = hist_scr[...]


def lru_sample(main, xcol, gcol, row0, db, ds, hist, h0, cw, cb, wg, ba, bx, clam):
    d = cw.shape[1]
    n_grp = wg.shape[0]
    blk0 = row0 // db
    const2 = lambda t: (0, 0)
    const3 = lambda t: (0, 0, 0)
    return pl.pallas_call(
        functools.partial(_lru_sample_kernel, n_grp=n_grp),
        grid=(ds,),
        in_specs=[pl.BlockSpec((db, d), lambda t: (blk0 + t, xcol)),
                  pl.BlockSpec((db, d), lambda t: (blk0 + t, gcol)),
                  pl.BlockSpec((CONV_W - 1, db, d), const3),
                  pl.BlockSpec((db, d), const2),
                  pl.BlockSpec((CONV_W, d), const2),
                  pl.BlockSpec((1, d), const2),
                  pl.BlockSpec(wg.shape, const3),
                  pl.BlockSpec((1, d), const2),
                  pl.BlockSpec((1, d), const2),
                  pl.BlockSpec((1, d), const2)],
        out_specs=[pl.BlockSpec((db, d), lambda t: (t, 0)),
                   pl.BlockSpec((db, d), const2),
                   pl.BlockSpec((CONV_W - 1, db, d), const3)],
        out_shape=[jax.ShapeDtypeStruct((ds * db, d), BF16),
                   jax.ShapeDtypeStruct((db, d), F32),
                   jax.ShapeDtypeStruct((CONV_W - 1, db, d), F32)],
        scratch_shapes=[pltpu.VMEM((CONV_W - 1, db, d), F32), pltpu.VMEM((db, d), F32)],
        compiler_params=_cparams(("arbitrary",)),
        name="lru_sample",
    )(main, main, hist, h0, cw, cb, wg, ba, bx, clam)


def _ssd_sample_kernel(*refs, ds, n_heads, n_grp):
    xbc_refs = refs[0:ds]
    z_refs = refs[ds:2 * ds]
    small_refs = refs[2 * ds:3 * ds]
    (hist_ref, h0_ref, cw_ref, cb_ref, dtb_ref, aneg_ref, dfull_ref, ng_ref, exp_ref,
     y_ref, hout_ref, convout_ref, yint_scr) = refs[3 * ds:]
    bb = xbc_refs[0].shape[0]
    d_ssm = z_refs[0].shape[1]
    hpg = n_heads // n_grp
    P = SSM_HEAD_DIM
    N = SSM_STATE
    gw = hpg * P

    ext = [hist_ref[j] for j in range(CONV_W - 1)] + [r[...] for r in xbc_refs]
    for j in range(CONV_W - 1):
        convout_ref[j] = ext[ds + j]
    xbc = []
    for t in range(ds):
        u = cb_ref[...]
        for j in range(CONV_W):
            u = u + cw_ref[j:j + 1, :] * ext[t + j]
        xbc.append(_silu(u))
    xs = [v[:, :d_ssm] for v in xbc]
    bm = [v[:, d_ssm:d_ssm + n_grp * N] for v in xbc]
    cm = [v[:, d_ssm + n_grp * N:] for v in xbc]
    dt = [_softplus(r[...] + dtb_ref[...]) for r in small_refs]
    cum = []
    run = jnp.zeros_like(dt[0])
    for t in range(ds):
        run = run + dt[t] * aneg_ref[...]
        cum.append(run)
    lane = lax.broadcasted_iota(I32, (bb, LANES), 1)
    coefs = []
    for t in range(ds):
        for j in range(t + 1):
            cbh = jnp.zeros((bb, LANES), F32)
            for g in range(n_grp):
                dotg = jnp.sum(cm[t][:, g * N:(g + 1) * N] * bm[j][:, g * N:(g + 1) * N], axis=1, keepdims=True)
                in_g = (lane >= DT_LANE + g * hpg) & (lane < DT_LANE + (g + 1) * hpg)
                cbh = cbh + jnp.where(in_g, dotg, 0.0)
            coefs.append(cbh * jnp.exp(cum[t] - cum[j]) * dt[j])
    n_intra = len(coefs)
    coefs += [jnp.exp(c) for c in cum]
    coefs += [jnp.exp(cum[ds - 1] - cum[j]) * dt[j] for j in range(ds)]
    coefs.append(jnp.exp(cum[ds - 1]))
    stack = jnp.concatenate(coefs, axis=0)
    wide = sum(jnp.dot(part, exp_ref[...], preferred_element_type=F32) for part in _split3_bf16(stack))
    wide = [wide[k * bb:(k + 1) * bb] for k in range(len(coefs))]
    intra_w = wide[:n_intra]
    ecum_w = wide[n_intra:n_intra + ds]
    dend_w = wide[n_intra + ds:n_intra + 2 * ds]
    cdec_w = wide[n_intra + 2 * ds]
    xw = [dend_w[j] * xs[j] for j in range(ds)]
    zpad = 8 - ds
    for b in range(bb):
        for g in range(n_grp):
            def rows_of(arrs, lo, width):
                parts = [a[b:b + 1, lo:lo + width] for a in arrs]
                if zpad:
                    parts.append(jnp.zeros((zpad, width), F32))
                return jnp.concatenate(parts, axis=0).astype(BF16)
            hg = h0_ref[b, g * hpg:(g + 1) * hpg].reshape(gw, N)
            yint = lax.dot_general(rows_of(cm, g * N, N), hg.astype(BF16), (((1,), (1,)), ((), ())),
                                   preferred_element_type=F32)
            for t in range(ds):
                yint_scr[t, b:b + 1, g * gw:(g + 1) * gw] = yint[t:t + 1]
            upd = lax.dot_general(rows_of(xw, g * gw, gw), rows_of(bm, g * N, N), (((0,), (0,)), ((), ())),
                                  preferred_element_type=F32)
            for hh in range(hpg):
                h = g * hpg + hh
                cd = cdec_w[b:b + 1, h * P:h * P + 1]
                hout_ref[b, h] = cd * hg[hh * P:(hh + 1) * P] + upd[hh * P:(hh + 1) * P]
    k = 0
    for t in range(ds):
        y = ecum_w[t] * yint_scr[t] + dfull_ref[...] * xs[t]
        for j in range(t + 1):
            y = y + intra_w[k] * xs[j]
            k += 1
        y = y * _silu(z_refs[t][...])
        y_ref[t] = _rmsnorm_rows(y, ng_ref[...]).astype(BF16)


def ssd_sample(main, xbc_col, z_col, small, row0, db, ds, bb, hist, h0, cw, cb, dtb, aneg, dfull, ng, expand,
               n_heads, n_grp):
    d_xbc = cw.shape[1]
    d_ssm = ng.shape[1]
    const2 = lambda i: (0, 0)

    def at_t(t, width, col):
        blk0 = (row0 + t * db) // bb
        return pl.BlockSpec((bb, width), lambda i: (blk0 + i, col))

    in_specs = ([at_t(t, d_xbc, xbc_col) for t in range(ds)] + [at_t(t, d_ssm, z_col) for t in range(ds)]
                + [at_t(t, LANES, 0) for t in range(ds)]
                + [pl.BlockSpec((CONV_W - 1, bb, d_xbc), lambda i: (0, i, 0)),
                   pl.BlockSpec((bb, n_heads, SSM_HEAD_DIM, SSM_STATE), lambda i: (i, 0, 0, 0)),
                   pl.BlockSpec((CONV_W, d_xbc), const2),
                   pl.BlockSpec((1, d_xbc), const2),
                   pl.BlockSpec((1, LANES), const2),
                   pl.BlockSpec((1, LANES), const2),
                   pl.BlockSpec((1, d_ssm), const2),
                   pl.BlockSpec((1, d_ssm), const2),
                   pl.BlockSpec((LANES, d_ssm), const2)])
    return pl.pallas_call(
        functools.partial(_ssd_sample_kernel, ds=ds, n_heads=n_heads, n_grp=n_grp),
        grid=(db // bb,),
        in_specs=in_specs,
        out_specs=[pl.BlockSpec((ds, bb, d_ssm), lambda i: (0, i, 0)),
                   pl.BlockSpec((bb, n_heads, SSM_HEAD_DIM, SSM_STATE), lambda i: (i, 0, 0, 0)),
                   pl.BlockSpec((CONV_W - 1, bb, d_xbc), lambda i: (0, i, 0))],
        out_shape=[jax.ShapeDtypeStruct((ds, db, d_ssm), BF16),
                   jax.ShapeDtypeStruct((db, n_heads, SSM_HEAD_DIM, SSM_STATE), F32),
                   jax.ShapeDtypeStruct((CONV_W - 1, db, d_xbc), F32)],
        scratch_shapes=[pltpu.VMEM((ds, bb, d_ssm), F32)],
        compiler_params=_cparams(("parallel",)),
        name="ssd_sample",
    )(*([main] * ds + [main] * ds + [small] * ds + [hist, h0, cw, cb, dtb, aneg, dfull, ng, expand]))


def _dsa_sample_kernel(pt_ref, *refs, n_pages, ds, ktop):
    del pt_ref
    q_ref, qi_ref, w_ref, knew_ref, vnew_ref, kxnew_ref = refs[0:6]
    kx_pages = refs[6:6 + n_pages]
    k_pages = refs[6 + n_pages:6 + 2 * n_pages]
    v_pages = refs[6 + 2 * n_pages:6 + 3 * n_pages]
    o_ref, key_scr, logit_scr, new_scr = refs[6 + 3 * n_pages:]
    n_rows, hd = q_ref.shape[1:]
    n_q = n_rows // ds
    n_idx, rq = qi_ref.shape[1:3]
    kw = knew_ref.shape[2]
    n_kv = kw // hd
    qpk = n_q // n_kv
    n_chunks = n_pages + 1

    w = w_ref[0] * ((n_idx * IDX_DIM) ** -0.5)
    qi = qi_ref[0].reshape(n_idx * rq, qi_ref.shape[3]).astype(BF16)
    qrow = lax.broadcasted_iota(I32, (rq, PAGE_SIZE), 0)
    lane = lax.broadcasted_iota(I32, (rq, PAGE_SIZE), 1)
    for p in range(n_pages):
        sc = _index_scores(qi, w, kx_pages[p][0].astype(BF16))
        key_scr[0, :, p * PAGE_SIZE:(p + 1) * PAGE_SIZE] = jnp.where(qrow < ds, _sortable_key(sc), INT_MIN)
    new_scr[...] = jnp.zeros(new_scr.shape, F32)
    new_scr[0:rq, 0:kxnew_ref.shape[2]] = kxnew_ref[0]
    sc = _index_scores(qi, w, new_scr[:, 0:kxnew_ref.shape[2]].astype(BF16))
    admissible = (lane <= qrow) & (qrow < ds)
    key_scr[0, :, n_pages * PAGE_SIZE:] = jnp.where(admissible, _sortable_key(sc), INT_MIN)
    col_bits = max(1, int(math.ceil(math.log2(n_chunks * PAGE_SIZE))))
    t, j_cut = _select_threshold(lambda c: key_scr[c], 1, rq, ktop, col_bits)

    assert n_q & (n_q - 1) == 0 and qpk & (qpk - 1) == 0 and hd & (hd - 1) == 0
    row_group = (lax.broadcasted_iota(I32, (n_rows, kw), 0) & (n_q - 1)) >> (qpk.bit_length() - 1)
    own_lanes = (lax.broadcasted_iota(I32, (n_rows, kw), 1) >> (hd.bit_length() - 1)) == row_group
    qexp = jnp.where(own_lanes, jnp.concatenate([q_ref[0]] * n_kv, axis=1), 0.0).astype(BF16)
    new_scr[0:rq, :] = knew_ref[0]
    knew = new_scr[...].astype(BF16)
    new_scr[0:rq, :] = vnew_ref[0]
    vnew = new_scr[...].astype(BF16)
    for p in range(n_chunks):
        kk = key_scr[0, :, p * PAGE_SIZE:(p + 1) * PAGE_SIZE]
        bias = jnp.where(_selected(kk, t, j_cut, p * PAGE_SIZE), 0.0, NEG_BIG)
        bias_rows = jnp.concatenate([jnp.broadcast_to(bias[ts:ts + 1], (n_q, PAGE_SIZE)) for ts in range(ds)], axis=0)
        kc = k_pages[p][0].astype(BF16) if p < n_pages else knew
        logit_scr[:, p * PAGE_SIZE:(p + 1) * PAGE_SIZE] = bias_rows + lax.dot_general(
            qexp, kc, (((1,), (1,)), ((), ())), preferred_element_type=F32)
    logits = logit_scr[...]
    prob = jnp.exp(logits - jnp.max(logits, axis=1, keepdims=True))
    denom = jnp.sum(prob, axis=1, keepdims=True)
    prob = prob.astype(BF16)
    acc = jnp.zeros((n_rows, kw), F32)
    for p in range(n_chunks):
        vc = v_pages[p][0].astype(BF16) if p < n_pages else vnew
        acc = acc + jnp.dot(prob[:, p * PAGE_SIZE:(p + 1) * PAGE_SIZE], vc, preferred_element_type=F32)
    out = jnp.zeros((n_rows, hd), F32)
    for g in range(n_kv):
        out = out + jnp.where(row_group[:, :hd] == g, acc[:, g * hd:(g + 1) * hd], 0.0)
    o_ref[0] = out / denom


def dsa_sample(page_table, q, qi, w, knew, vnew, kxnew, cache_k, cache_v, cache_kidx, layer, n_pool, ds, ktop):
    db, n_rows, hd = q.shape
    n_idx, rq = qi.shape[1:3]
    kw = knew.shape[2]
    n_pages = page_table.shape[1]

    def page_spec(width, p):
        return pl.BlockSpec((1, PAGE_SIZE, width), lambda b, pt: (layer * n_pool + pt[b, p], 0, 0))

    own3 = lambda b, pt: (b, 0, 0)
    own4 = lambda b, pt: (b, 0, 0, 0)
    in_specs = ([pl.BlockSpec((1, n_rows, hd), own3),
                 pl.BlockSpec((1, n_idx, rq, IDX_DIM), own4),
                 pl.BlockSpec((1, rq, n_idx), own3),
                 pl.BlockSpec((1, rq, kw), own3),
                 pl.BlockSpec((1, rq, kw), own3),
                 pl.BlockSpec((1, rq, IDX_DIM), own3)]
                + [page_spec(IDX_DIM, p) for p in range(n_pages)]
                + [page_spec(kw, p) for p in range(n_pages)]
                + [page_spec(kw, p) for p in range(n_pages)])
    grid_spec = pltpu.PrefetchScalarGridSpec(
        num_scalar_prefetch=1,
        grid=(db,),
        in_specs=in_specs,
        out_specs=pl.BlockSpec((1, n_rows, hd), own3),
        scratch_shapes=[pltpu.VMEM((1, rq, (n_pages + 1) * PAGE_SIZE), I32),
                        pltpu.VMEM((n_rows, (n_pages + 1) * PAGE_SIZE), F32),
                        pltpu.VMEM((PAGE_SIZE, kw), F32)])
    return pl.pallas_call(
        functools.partial(_dsa_sample_kernel, n_pages=n_pages, ds=ds, ktop=ktop),
        grid_spec=grid_spec,
        out_shape=jax.ShapeDtypeStruct((db, n_rows, hd), F32),
        compiler_params=_cparams(("arbitrary",)),
        name="dsa_sample",
    )(page_table, q, qi, w, knew, vnew, kxnew,
      *([cache_kidx] * n_pages + [cache_k] * n_pages + [cache_v] * n_pages))


def _round_up(x, m):
    return (x + m - 1) // m * m


def _largest_tile(total, unit, cap):
    best = unit
    for k in range(1, cap // unit + 1):
        if total % (k * unit) == 0:
            best = k * unit
    return best


def _block_diag_groups(w, per_group):
    nb, r, _ = w.shape
    ng = nb // per_group
    w = w.reshape(ng, per_group, r, r)
    eye = jnp.eye(per_group, dtype=w.dtype)
    return jnp.einsum('gbij,bc->gbicj', w, eye).reshape(ng, per_group * r, per_group * r)


def _pad_rows_to(a, axis, n):
    pad = [(0, 0)] * a.ndim
    pad[axis] = (0, n - a.shape[axis])
    return jnp.pad(a, pad)


def kernel(x_prompt, x_sample, cache_k, cache_v, cache_kidx, state_ssm, state_ssm_conv, state_lru_h, state_lru_conv, page_table, meta_tokens, norm1, w_in, lru_conv_w, lru_conv_b, lru_wa, lru_ba, lru_wx, lru_bx, lru_lambda, ssm_conv_w, ssm_conv_b, ssm_dt_bias, ssm_a_log, ssm_d, ssm_norm, p_lru, p_ssm, p_attn, w_o, norm2, w_up, w_down, final_norm):
    nb, seq, d = x_prompt.shape
    db, ds, _ = x_sample.shape
    depth = w_in.shape[0]
    n_pages = page_table.shape[1]
    past = n_pages * PAGE_SIZE
    n_pool = cache_k.shape[1]
    n_kv = cache_k.shape[3]
    kw = n_kv * HEAD_DIM
    d_rnn = lru_conv_w.shape[2]
    d_xbc = ssm_conv_w.shape[2]
    d_ssm = ssm_norm.shape[1]
    n_ssm_heads = ssm_d.shape[1]
    n_heads = p_attn.shape[1] // HEAD_DIM
    n_idx = N_IDX_HEADS
    d_ff = w_up.shape[2]
    assert d_rnn == d and d_ssm == d and d_xbc == 2 * d and n_heads * HEAD_DIM == d
    assert seq % ROW_TILE == 0 and ds <= 8 and n_ssm_heads <= LANES - DT_LANE

    pad = (-N_META) % ROW_TILE
    tp = pad + N_META + seq
    mp = nb * tp
    ms = db * ds
    mt = _round_up(mp + ms, 2 * DENSE_TM)
    bb = 8
    assert mp % db == 0 and db % bb == 0
    ktop_p = min(TOP_K_MAX, seq // TOP_K_FRACTION)
    ktop_s = min(TOP_K_MAX, (past + ds) // TOP_K_FRACTION)
    tkc = _largest_tile(tp, LANES, 640)

    xp = jnp.concatenate([jnp.zeros((nb, pad, d), F32),
                          jnp.broadcast_to(meta_tokens[None].astype(F32), (nb, N_META, d)),
                          x_prompt], axis=1).reshape(mp, d)
    xs = jnp.swapaxes(x_sample, 0, 1).reshape(ms, d)
    x = jnp.concatenate([xp, xs, jnp.zeros((mt - mp - ms, d), F32)], axis=0)
    r = np.arange(mt)
    in_prompt = r < mp
    rp = r % tp
    valid_np = np.where(in_prompt, rp >= pad, r < mp + ms)
    pos_np = np.where(in_prompt, np.maximum(rp - pad, 0), np.where(r < mp + ms, past + (r - mp) // db, 0))
    valid = jnp.asarray(valid_np.astype(np.float32)).reshape(mt, 1)
    half = HEAD_DIM // 2
    freq = ROPE_THETA ** (-jnp.arange(half, dtype=F32) / half)
    ang = jnp.asarray(pos_np).astype(F32)[:, None] * freq[None, :]
    cos = jnp.cos(ang)
    sin = jnp.sin(ang)
    cos_t = jnp.concatenate([cos] * (LANES // half), axis=1)
    sin_t = jnp.concatenate([-sin, sin] * (LANES // HEAD_DIM), axis=1)

    splits = (d_rnn, d_rnn, d_ssm, d_xbc, n_ssm_heads, n_heads * HEAD_DIM, kw, kw, n_idx * IDX_DIM, IDX_DIM,
              n_idx, N_BRANCH * d)
    assert sum(splits) == w_in.shape[2]
    off = np.concatenate([[0], np.cumsum(splits)])
    seg = lambda w, i: w[:, off[i]:off[i + 1]]
    LRU_X, LRU_G, SSM_Z, SSM_XBC, SSM_DT, Q, K, V, QI, KIDX, WIDX, GATES = range(12)
    XBC_COL, X_COL, G_COL, Z_COL, GATE_COL = 0, 2, 3, 4, 5
    lane_pos = np.arange(LANES)
    head_lanes = (lane_pos >= DT_LANE) & (lane_pos < DT_LANE + n_ssm_heads)
    expand_np = np.zeros((LANES, d_ssm), np.float32)
    for h in range(n_ssm_heads):
        expand_np[DT_LANE + h, h * SSM_HEAD_DIM:(h + 1) * SSM_HEAD_DIM] = 1.0
    expand = jnp.asarray(expand_np).astype(BF16)

    def to_head_lanes(v):
        return jnp.zeros((1, LANES), F32).at[0, DT_LANE:DT_LANE + n_ssm_heads].set(v)

    outs_p = [[] for _ in range(7)]
    outs_s = [[] for _ in range(7)]
    for l in range(depth):
        w = w_in[l]
        w_main = jnp.concatenate([seg(w, SSM_XBC), seg(w, LRU_X), seg(w, LRU_G), seg(w, SSM_Z), seg(w, GATES)],
                                 axis=1).astype(BF16)
        tail = LANES - IDX_DIM - n_ssm_heads - n_idx
        w_attn = jnp.concatenate([seg(w, Q), seg(w, QI), seg(w, K), seg(w, V), seg(w, KIDX), seg(w, SSM_DT),
                                  seg(w, WIDX), jnp.zeros((d, tail), F32)], axis=1).astype(BF16)
        per_group = 256 // (d_rnn // N_RNN_BLOCKS)
        wg = jnp.concatenate([_block_diag_groups(lru_wa[l], per_group), _block_diag_groups(lru_wx[l], per_group)],
                             axis=2).astype(BF16)
        clam = (-LRU_C * jax.nn.softplus(-lru_lambda[l])).reshape(1, d_rnn)
        lru_args = (lru_conv_w[l], lru_conv_b[l].reshape(1, -1), wg, lru_ba[l].reshape(1, -1),
                    lru_bx[l].reshape(1, -1), clam)
        dtb = to_head_lanes(ssm_dt_bias[l])
        aneg = to_head_lanes(-jnp.exp(ssm_a_log[l]))
        dfull = jnp.repeat(ssm_d[l], SSM_HEAD_DIM).reshape(1, d_ssm)
        ssm_args = (ssm_conv_w[l], ssm_conv_b[l].reshape(1, -1), dtb, aneg, dfull, ssm_norm[l].reshape(1, -1))

        main = norm_matmul(x, norm1[l].reshape(1, d), w_main, 2 * DENSE_TM, 2048)
        q, qi, kf, vf, small, kb, vb, kidxb = attn_proj(x, norm1[l].reshape(1, d), w_attn, cos_t, sin_t,
                                                        n_heads, n_idx, n_kv, DENSE_TM)
        ya_p, hlru_p = lru_prompt(main, X_COL, G_COL, nb, tp, pad, *lru_args)
        yb_p, hssm_p = ssd_prompt(main, XBC_COL, Z_COL, small, nb, tp, pad, *ssm_args, n_ssm_heads, N_SSM_GROUPS)
        yc_p = dsa_prompt(q, qi, small, kb, vb, kidxb, nb, tp, pad, ktop_p, tkc)
        ya_s, hlru_s, conv_lru_s = lru_sample(main, X_COL, G_COL, mp, db, ds,
                                              jnp.swapaxes(state_lru_conv[l], 0, 1), state_lru_h[l], *lru_args)
        yb_s, hssm_s, conv_ssm_s = ssd_sample(main, XBC_COL, Z_COL, small, mp, db, ds, bb,
                                              jnp.swapaxes(state_ssm_conv[l], 0, 1), state_ssm[l], *ssm_args,
                                              expand, n_ssm_heads, N_SSM_GROUPS)

        def batch_major(a):
            lead = a.shape[:-2]
            a = a.astype(F32).reshape(lead + (ds, db, a.shape[-1]))
            a = jnp.moveaxis(a, len(lead) + 1, 0)
            return _pad_rows_to(a, a.ndim - 2, 8)

        small_s = batch_major(small[mp:mp + ms])
        q_s = jnp.transpose(q[:, mp:mp + ms].astype(F32).reshape(n_heads, ds, db, HEAD_DIM), (2, 1, 0, 3))
        yc_s = dsa_sample(page_table, q_s.reshape(db, ds * n_heads, HEAD_DIM), batch_major(qi[:, mp:mp + ms]),
                          small_s[:, :, WIDX_LANE:WIDX_LANE + n_idx], batch_major(kf[mp:mp + ms]),
                          batch_major(vf[mp:mp + ms]), small_s[:, :, :IDX_DIM],
                          cache_k.reshape(depth * n_pool, PAGE_SIZE, kw),
                          cache_v.reshape(depth * n_pool, PAGE_SIZE, kw),
                          cache_kidx.reshape(depth * n_pool, PAGE_SIZE, IDX_DIM), l, n_pool, ds, ktop_s)
        yc_s = jnp.swapaxes(yc_s.reshape(db, ds, n_heads * HEAD_DIM), 0, 1).reshape(ms, n_heads * HEAD_DIM).astype(BF16)

        tail_rows = jnp.zeros((mt - mp - ms, d), BF16)
        ya = jnp.concatenate([ya_p, ya_s, tail_rows], axis=0)
        yb = jnp.concatenate([yb_p, yb_s.reshape(ms, d_ssm), tail_rows], axis=0)
        yc = jnp.concatenate([yc_p, yc_s, tail_rows], axis=0)
        x = merge(ya, yb, yc, main, GATE_COL, x, valid, p_lru[l].astype(BF16), p_ssm[l].astype(BF16),
                  p_attn[l].astype(BF16), w_o[l].astype(BF16), DENSE_TM)
        x = mlp(x, norm2[l].reshape(1, d), w_up[l].astype(BF16), w_down[l].astype(BF16), 2 * DENSE_TM, 1024)

        prm = lambda a: a[:mp].reshape((nb, tp) + a.shape[1:])
        smp = lambda a: jnp.swapaxes(a[mp:mp + ms].reshape((ds, db) + a.shape[1:]), 0, 1)
        last = CONV_W - 1
        conv_tail = lambda c0, width: jnp.stack([main[(b + 1) * tp - last:(b + 1) * tp, c0:c0 + width]
                                                 for b in range(nb)])
        rows_p = (prm(kf)[:, pad:].reshape(nb, tp - pad, n_kv, HEAD_DIM),
                  prm(vf)[:, pad:].reshape(nb, tp - pad, n_kv, HEAD_DIM),
                  prm(small)[:, pad:, :IDX_DIM],
                  jnp.swapaxes(hssm_p, 2, 3),
                  conv_tail(XBC_COL * d, d_xbc),
                  hlru_p[:, 0],
                  conv_tail(X_COL * d, d_rnn))
        rows_s = (smp(kf).reshape(db, ds, n_kv, HEAD_DIM),
                  smp(vf).reshape(db, ds, n_kv, HEAD_DIM),
                  smp(small)[:, :, :IDX_DIM],
                  hssm_s,
                  jnp.swapaxes(conv_ssm_s, 0, 1),
                  hlru_s,
                  jnp.swapaxes(conv_lru_s, 0, 1))
        for acc, val in zip(outs_p, rows_p):
            acc.append(val)
        for acc, val in zip(outs_s, rows_s):
            acc.append(val)

    y = final_rmsnorm(x, final_norm.reshape(1, d), DENSE_TM)
    y_prompt = y[:mp].reshape(nb, tp, d)[:, pad + N_META:]
    y_sample = jnp.swapaxes(y[mp:mp + ms].reshape(ds, db, d), 0, 1)
    return (y_prompt, y_sample) + tuple(jnp.stack(a) for a in outs_p) + tuple(jnp.stack(a) for a in outs_s)
```

```python
import functools
import math

import numpy as np
import jax
import jax.numpy as jnp
from jax import lax
from jax.experimental import pallas as pl
from jax.experimental.pallas import tpu as pltpu

F32 = jnp.float32
BF16 = jnp.bfloat16
I32 = jnp.int32

N_META = 16
CONV_W = 4
EPS = 1e-6
LRU_C = 8.0
N_RNN_BLOCKS = 16
SSM_HEAD_DIM = 64
N_SSM_GROUPS = 4
SSM_STATE = 128
SSM_CHUNK = 128
HEAD_DIM = 64
N_KV_HEADS = 4
N_IDX_HEADS = 8
IDX_DIM = 64
TOP_K_MAX = 256
TOP_K_FRACTION = 4
ROPE_THETA = 10000.0
PAGE_SIZE = 128
N_BRANCH = 3

LANES = 128
ROW_TILE = 128
MXU_DIM = 256
DSA_KEY_UNIT = MXU_DIM
DSA_KEY_CHUNK_MAX = 3 * MXU_DIM
DENSE_TM = 512
VMEM_LIMIT = 56 * 1024 * 1024
INT_MIN = -(2 ** 31)
NEG_BIG = -1e30


def _cparams(sem):
    return pltpu.CompilerParams(dimension_semantics=sem, vmem_limit_bytes=VMEM_LIMIT)


def _rmsnorm_rows(x, g):
    return x * lax.rsqrt(jnp.mean(x * x, axis=-1, keepdims=True) + EPS) * g


def _norm_matmul_kernel(x_ref, g_ref, w_ref, o_ref, h_scr):
    @pl.when(pl.program_id(1) == 0)
    def _():
        h_scr[...] = _rmsnorm_rows(x_ref[...], g_ref[...]).astype(BF16)

    o_ref[...] = jnp.dot(h_scr[...], w_ref[...], preferred_element_type=F32)


def norm_matmul(x, g, w, tm, tn):
    m, d = x.shape
    n = w.shape[1]
    return pl.pallas_call(
        _norm_matmul_kernel,
        grid=(m // tm, n // tn),
        in_specs=[pl.BlockSpec((tm, d), lambda i, j: (i, 0)),
                  pl.BlockSpec((1, d), lambda i, j: (0, 0)),
                  pl.BlockSpec((d, tn), lambda i, j: (0, j))],
        out_specs=pl.BlockSpec((tm, tn), lambda i, j: (i, j)),
        out_shape=jax.ShapeDtypeStruct((m, n), F32),
        scratch_shapes=[pltpu.VMEM((tm, d), BF16)],
        compiler_params=_cparams(("parallel", "arbitrary")),
        name="inproj_main",
    )(x, g, w)


def _attn_proj_kernel(x_ref, g_ref, w_ref, cos_ref, sin_ref,
                      q_ref, qi_ref, k_ref, v_ref, small_ref, kb_ref, vbt_ref, kidxb_ref,
                      *, n_heads, n_idx, n_kv):
    h = _rmsnorm_rows(x_ref[...], g_ref[...]).astype(BF16)
    acc = jnp.dot(h, w_ref[...], preferred_element_type=F32)
    cos = cos_ref[...]
    sin = sin_ref[...]
    lane = lax.broadcasted_iota(I32, cos.shape, 1)
    lo_half = (lane & (HEAD_DIM // 2)) == 0

    def rope(xb, c, s):
        partner = jnp.where(lo_half, pltpu.roll(xb, LANES - HEAD_DIM // 2, 1), pltpu.roll(xb, HEAD_DIM // 2, 1))
        return xb * c + partner * s

    col = 0
    scale = HEAD_DIM ** -0.5 * math.log2(math.e)
    for p in range(n_heads // 2):
        blk = rope(acc[:, col:col + LANES], cos, sin) * scale
        q_ref[2 * p] = blk[:, :HEAD_DIM].astype(BF16)
        q_ref[2 * p + 1] = blk[:, HEAD_DIM:].astype(BF16)
        col += LANES
    for p in range(n_idx // 2):
        blk = rope(acc[:, col:col + LANES], cos, sin)
        qi_ref[2 * p] = blk[:, :IDX_DIM].astype(BF16)
        qi_ref[2 * p + 1] = blk[:, IDX_DIM:].astype(BF16)
        col += LANES
    kw = n_kv * HEAD_DIM
    first = lane < HEAD_DIM
    ones_lane = jnp.where(lane == HEAD_DIM, 1.0, 0.0)
    for p in range(kw // LANES):
        blk = rope(acc[:, col:col + LANES], cos, sin)
        k_ref[:, p * LANES:(p + 1) * LANES] = blk
        kb_ref[2 * p] = jnp.where(first, blk, 0.0).astype(BF16)
        kb_ref[2 * p + 1] = jnp.where(first, pltpu.roll(blk, HEAD_DIM, 1), 0.0).astype(BF16)
        col += LANES
    v_ref[...] = acc[:, col:col + kw]
    for p in range(kw // LANES):
        blk = acc[:, col:col + LANES]
        vbt_ref[2 * p] = jnp.where(first, blk, ones_lane).T.astype(BF16)
        vbt_ref[2 * p + 1] = jnp.where(first, pltpu.roll(blk, HEAD_DIM, 1), ones_lane).T.astype(BF16)
        col += LANES
    is_kidx = lane < IDX_DIM
    blk = rope(acc[:, col:col + LANES], jnp.where(is_kidx, cos, 1.0), jnp.where(is_kidx, sin, 0.0))
    small_ref[...] = blk
    kidxb_ref[...] = blk[:, :IDX_DIM].astype(BF16)


def attn_proj(x, g, w, cos, sin, n_heads, n_idx, n_kv, tm):
    m, d = x.shape
    n = w.shape[1]
    kw = n_kv * HEAD_DIM
    row = lambda i: (i, 0)
    return pl.pallas_call(
        functools.partial(_attn_proj_kernel, n_heads=n_heads, n_idx=n_idx, n_kv=n_kv),
        grid=(m // tm,),
        in_specs=[pl.BlockSpec((tm, d), row),
                  pl.BlockSpec((1, d), lambda i: (0, 0)),
                  pl.BlockSpec((d, n), lambda i: (0, 0)),
                  pl.BlockSpec((tm, LANES), row),
                  pl.BlockSpec((tm, LANES), row)],
        out_specs=[pl.BlockSpec((n_heads, tm, HEAD_DIM), lambda i: (0, i, 0)),
                   pl.BlockSpec((n_idx, tm, IDX_DIM), lambda i: (0, i, 0)),
                   pl.BlockSpec((tm, kw), row),
                   pl.BlockSpec((tm, kw), row),
                   pl.BlockSpec((tm, LANES), row),
                   pl.BlockSpec((n_kv, tm, LANES), lambda i: (0, i, 0)),
                   pl.BlockSpec((n_kv, LANES, tm), lambda i: (0, 0, i)),
                   pl.BlockSpec((tm, IDX_DIM), row)],
        out_shape=[jax.ShapeDtypeStruct((n_heads, m, HEAD_DIM), BF16),
                   jax.ShapeDtypeStruct((n_idx, m, IDX_DIM), BF16),
                   jax.ShapeDtypeStruct((m, kw), F32),
                   jax.ShapeDtypeStruct((m, kw), F32),
                   jax.ShapeDtypeStruct((m, LANES), F32),
                   jax.ShapeDtypeStruct((n_kv, m, LANES), BF16),
                   jax.ShapeDtypeStruct((n_kv, LANES, m), BF16),
                   jax.ShapeDtypeStruct((m, IDX_DIM), BF16)],
        compiler_params=_cparams(("parallel",)),
        name="inproj_attn",
    )(x, g, w, cos, sin)


def _merge_kernel(ya_ref, yb_ref, yc_ref, g0_ref, g1_ref, g2_ref, x_ref, valid_ref,
                  pa_ref, pb_ref, pc_ref, wo_ref, o_ref):
    def branch(y_ref, p_ref, g_ref):
        return jax.nn.sigmoid(g_ref[...]) * jnp.dot(y_ref[...], p_ref[...], preferred_element_type=F32)

    merged = branch(ya_ref, pa_ref, g0_ref) + branch(yb_ref, pb_ref, g1_ref) + branch(yc_ref, pc_ref, g2_ref)
    xn = x_ref[...] + jnp.dot(merged.astype(BF16), wo_ref[...], preferred_element_type=F32)
    o_ref[...] = jnp.where(valid_ref[...] > 0.0, xn, 0.0)


def merge(ya, yb, yc, main, gate_col, x, valid, pa, pb, pc, wo, tm):
    m, d = x.shape
    row = lambda i: (i, 0)
    wspec = pl.BlockSpec((d, d), lambda i: (0, 0))
    return pl.pallas_call(
        _merge_kernel,
        grid=(m // tm,),
        in_specs=[pl.BlockSpec((tm, d), row), pl.BlockSpec((tm, d), row), pl.BlockSpec((tm, d), row),
                  pl.BlockSpec((tm, d), lambda i: (i, gate_col)),
                  pl.BlockSpec((tm, d), lambda i: (i, gate_col + 1)),
                  pl.BlockSpec((tm, d), lambda i: (i, gate_col + 2)),
                  pl.BlockSpec((tm, d), row),
                  pl.BlockSpec((tm, 1), row),
                  wspec, wspec, wspec, wspec],
        out_specs=pl.BlockSpec((tm, d), row),
        out_shape=jax.ShapeDtypeStruct((m, d), F32),
        compiler_params=_cparams(("parallel",)),
        name="merge",
    )(ya, yb, yc, main, main, main, x, valid, pa, pb, pc, wo)


def _mlp_kernel(x_ref, g_ref, wu_ref, wd_ref, o_ref, h_scr, acc_scr):
    j = pl.program_id(1)

    @pl.when(j == 0)
    def _():
        h_scr[...] = _rmsnorm_rows(x_ref[...], g_ref[...]).astype(BF16)
        acc_scr[...] = x_ref[...]

    u = jnp.dot(h_scr[...], wu_ref[...], preferred_element_type=F32)
    u = jnp.square(jnp.maximum(u, 0.0)).astype(BF16)
    acc_scr[...] += jnp.dot(u, wd_ref[...], preferred_element_type=F32)

    @pl.when(j == pl.num_programs(1) - 1)
    def _():
        o_ref[...] = acc_scr[...]


def mlp(x, g, wu, wd, tm, tf):
    m, d = x.shape
    f = wu.shape[1]
    return pl.pallas_call(
        _mlp_kernel,
        grid=(m // tm, f // tf),
        in_specs=[pl.BlockSpec((tm, d), lambda i, j: (i, 0)),
                  pl.BlockSpec((1, d), lambda i, j: (0, 0)),
                  pl.BlockSpec((d, tf), lambda i, j: (0, j)),
                  pl.BlockSpec((tf, d), lambda i, j: (j, 0))],
        out_specs=pl.BlockSpec((tm, d), lambda i, j: (i, 0)),
        out_shape=jax.ShapeDtypeStruct((m, d), F32),
        scratch_shapes=[pltpu.VMEM((tm, d), BF16), pltpu.VMEM((tm, d), F32)],
        compiler_params=_cparams(("parallel", "arbitrary")),
        name="mlp",
    )(x, g, wu, wd)


def _final_norm_kernel(x_ref, g_ref, o_ref):
    o_ref[...] = _rmsnorm_rows(x_ref[...], g_ref[...])


def final_rmsnorm(x, g, tm):
    m, d = x.shape
    return pl.pallas_call(
        _final_norm_kernel,
        grid=(m // tm,),
        in_specs=[pl.BlockSpec((tm, d), lambda i: (i, 0)), pl.BlockSpec((1, d), lambda i: (0, 0))],
        out_specs=pl.BlockSpec((tm, d), lambda i: (i, 0)),
        out_shape=jax.ShapeDtypeStruct((m, d), F32),
        compiler_params=_cparams(("parallel",)),
        name="final_norm",
    )(x, g)


def _softplus(x):
    return jnp.maximum(x, 0.0) + jnp.log(1.0 + jnp.exp(-jnp.abs(x)))


def _silu(x):
    return x * jax.nn.sigmoid(x)


def _gelu_tanh(x):
    return 0.5 * x * (1.0 + jnp.tanh(math.sqrt(2.0 / math.pi) * (x + 0.044715 * (x * x * x))))


def _causal_conv_rows(ext_ref, x, w_ref, b_ref, rows):
    ext_ref[8:8 + rows, :] = x
    out = b_ref[...] + w_ref[CONV_W - 1:CONV_W, :] * x
    for j in range(CONV_W - 1):
        out = out + w_ref[j:j + 1, :] * ext_ref[5 + j:5 + j + rows, :]
    ext_ref[5:8, :] = ext_ref[5 + rows:8 + rows, :]
    return out


def _lru_gates(u, wg_ref, ba_ref, bx_ref, clam_ref, n_grp):
    gw = u.shape[1] // n_grp
    rs, is_ = [], []
    for g in range(n_grp):
        rg = jnp.dot(u[:, g * gw:(g + 1) * gw].astype(BF16), wg_ref[g], preferred_element_type=F32)
        rs.append(rg[:, :gw])
        is_.append(rg[:, gw:])
    r = jax.nn.sigmoid(jnp.concatenate(rs, axis=1) + ba_ref[...])
    i = jax.nn.sigmoid(jnp.concatenate(is_, axis=1) + bx_ref[...])
    log_a = clam_ref[...] * r
    a = jnp.exp(log_a)
    b = jnp.sqrt(1.0 - jnp.exp(2.0 * log_a)) * (i * u)
    return a, b


def _lru_prompt_kernel(x_ref, g_ref, cw_ref, cb_ref, wg_ref, ba_ref, bx_ref, clam_ref,
                       y_ref, hout_ref, ext_scr, h_scr, *, pad, n_grp):
    c = pl.program_id(1)
    rows = x_ref.shape[0]

    @pl.when(c == 0)
    def _():
        ext_scr[0:8, :] = jnp.zeros((8, ext_scr.shape[1]), F32)
        h_scr[...] = jnp.zeros(h_scr.shape, F32)

    u = _causal_conv_rows(ext_scr, x_ref[...], cw_ref, cb_ref, rows)
    a, b = _lru_gates(u, wg_ref, ba_ref, bx_ref, clam_ref, n_grp)
    ridx = lax.broadcasted_iota(I32, (rows, 1), 0)
    b = jnp.where(c * rows + ridx >= pad, b, 0.0)
    s = 1
    while s < rows:
        keep = ridx >= s
        a_sh = jnp.where(keep, pltpu.roll(a, s, 0), 1.0)
        b_sh = jnp.where(keep, pltpu.roll(b, s, 0), 0.0)
        b = a * b_sh + b
        a = a * a_sh
        s *= 2
    h = a * h_scr[...] + b
    h_scr[...] = h[rows - 1:rows, :]
    hout_ref[0] = h[rows - 1:rows, :]
    y_ref[...] = (h * _gelu_tanh(g_ref[...])).astype(BF16)


def lru_prompt(main, xcol, gcol, nb, tp, pad, cw, cb, wg, ba, bx, clam):
    d = cw.shape[1]
    n_grp = wg.shape[0]
    nt = tp // ROW_TILE
    const2 = lambda b, c: (0, 0)
    return pl.pallas_call(
        functools.partial(_lru_prompt_kernel, pad=pad, n_grp=n_grp),
        grid=(nb, nt),
        in_specs=[pl.BlockSpec((ROW_TILE, d), lambda b, c: (b * nt + c, xcol)),
                  pl.BlockSpec((ROW_TILE, d), lambda b, c: (b * nt + c, gcol)),
                  pl.BlockSpec((CONV_W, d), const2),
                  pl.BlockSpec((1, d), const2),
                  pl.BlockSpec(wg.shape, lambda b, c: (0, 0, 0)),
                  pl.BlockSpec((1, d), const2),
                  pl.BlockSpec((1, d), const2),
                  pl.BlockSpec((1, d), const2)],
        out_specs=[pl.BlockSpec((ROW_TILE, d), lambda b, c: (b * nt + c, 0)),
                   pl.BlockSpec((1, 1, d), lambda b, c: (b, 0, 0))],
        out_shape=[jax.ShapeDtypeStruct((nb * tp, d), BF16),
                   jax.ShapeDtypeStruct((nb, 1, d), F32)],
        scratch_shapes=[pltpu.VMEM((8 + ROW_TILE, d), F32), pltpu.VMEM((1, d), F32)],
        compiler_params=_cparams(("parallel", "arbitrary")),
        name="lru_prompt",
    )(main, main, cw, cb, wg, ba, bx, clam)


DT_LANE = IDX_DIM


def _split3_bf16(x):
    hi = x.astype(BF16)
    r1 = x - hi.astype(F32)
    mid = r1.astype(BF16)
    lo = (r1 - mid.astype(F32)).astype(BF16)
    return hi, mid, lo


def _ssd_prompt_kernel(xbc_ref, z_ref, small_ref, cw_ref, cb_ref, dtb_ref, aneg_ref, dfull_ref, ng_ref,
                       y_ref, hout_ref, ext_scr, ht_scr, y_scr, *, pad, n_heads, n_grp):
    c = pl.program_id(1)
    rows = xbc_ref.shape[0]
    d_ssm = z_ref.shape[1]
    hpg = n_heads // n_grp
    P = SSM_HEAD_DIM
    N = SSM_STATE

    @pl.when(c == 0)
    def _():
        ext_scr[0:8, :] = jnp.zeros((8, ext_scr.shape[1]), F32)
        ht_scr[...] = jnp.zeros(ht_scr.shape, F32)

    xbc = _silu(_causal_conv_rows(ext_scr, xbc_ref[...], cw_ref, cb_ref, rows))
    xs = xbc[:, :d_ssm]
    ridx = lax.broadcasted_iota(I32, (rows, 1), 0)
    dt = _softplus(small_ref[...] + dtb_ref[...])
    dt = jnp.where(c * rows + ridx >= pad, dt, 0.0)
    da = dt * aneg_ref[...]
    ii = lax.broadcasted_iota(I32, (rows, rows), 0)
    jj = lax.broadcasted_iota(I32, (rows, rows), 1)
    causal = jj <= ii
    tri = jnp.where(causal, 1.0, 0.0).astype(BF16)
    cum = sum(jnp.dot(tri, part, preferred_element_type=F32) for part in _split3_bf16(da))
    cum_t = cum.T
    dt_t = dt.T
    cum_last = cum[rows - 1:rows, :]
    dec_end = jnp.exp(cum_last - cum) * dt
    ecum = jnp.exp(cum)
    chunk_dec = jnp.exp(cum_last)
    for g in range(n_grp):
        bg = xbc[:, d_ssm + g * N:d_ssm + (g + 1) * N]
        cg = xbc[:, d_ssm + n_grp * N + g * N:d_ssm + n_grp * N + (g + 1) * N]
        bg16 = bg.astype(BF16)
        cg16 = cg.astype(BF16)
        cb = lax.dot_general(cg16, bg16, (((1,), (1,)), ((), ())), preferred_element_type=F32)
        bgt16 = bg.T.astype(BF16)
        for hh in range(hpg):
            h = g * hpg + hh
            ln = DT_LANE + h
            x_h = xs[:, h * P:(h + 1) * P]
            seg = cum[:, ln:ln + 1] - cum_t[ln:ln + 1, :]
            decay = jnp.where(causal, jnp.exp(jnp.where(causal, seg, 0.0)), 0.0)
            w_intra = (cb * decay * dt_t[ln:ln + 1, :]).astype(BF16)
            y_h = jnp.dot(w_intra, x_h.astype(BF16), preferred_element_type=F32)
            ht = ht_scr[h]
            y_h = y_h + jnp.dot(cg16, ht.astype(BF16), preferred_element_type=F32) * ecum[:, ln:ln + 1]
            xw = (x_h * dec_end[:, ln:ln + 1]).astype(BF16)
            ht_new = chunk_dec[:, ln:ln + 1] * ht + jnp.dot(bgt16, xw, preferred_element_type=F32)
            ht_scr[h] = ht_new
            hout_ref[0, h] = ht_new
            y_scr[:, h * P:(h + 1) * P] = y_h
    y = y_scr[...] + dfull_ref[...] * xs
    y = y * _silu(z_ref[...])
    y_ref[...] = _rmsnorm_rows(y, ng_ref[...]).astype(BF16)


def ssd_prompt(main, xbc_col, z_col, small, nb, tp, pad, cw, cb, dtb, aneg, dfull, ng, n_heads, n_grp):
    d_xbc = cw.shape[1]
    d_ssm = ng.shape[1]
    nt = tp // ROW_TILE
    const2 = lambda b, c: (0, 0)
    return pl.pallas_call(
        functools.partial(_ssd_prompt_kernel, pad=pad, n_heads=n_heads, n_grp=n_grp),
        grid=(nb, nt),
        in_specs=[pl.BlockSpec((ROW_TILE, d_xbc), lambda b, c: (b * nt + c, xbc_col)),
                  pl.BlockSpec((ROW_TILE, d_ssm), lambda b, c: (b * nt + c, z_col)),
                  pl.BlockSpec((ROW_TILE, LANES), lambda b, c: (b * nt + c, 0)),
                  pl.BlockSpec((CONV_W, d_xbc), const2),
                  pl.BlockSpec((1, d_xbc), const2),
                  pl.BlockSpec((1, LANES), const2),
                  pl.BlockSpec((1, LANES), const2),
                  pl.BlockSpec((1, d_ssm), const2),
                  pl.BlockSpec((1, d_ssm), const2)],
        out_specs=[pl.BlockSpec((ROW_TILE, d_ssm), lambda b, c: (b * nt + c, 0)),
                   pl.BlockSpec((1, n_heads, SSM_STATE, SSM_HEAD_DIM), lambda b, c: (b, 0, 0, 0))],
        out_shape=[jax.ShapeDtypeStruct((nb * tp, d_ssm), BF16),
                   jax.ShapeDtypeStruct((nb, n_heads, SSM_STATE, SSM_HEAD_DIM), F32)],
        scratch_shapes=[pltpu.VMEM((8 + ROW_TILE, d_xbc), F32),
                        pltpu.VMEM((n_heads, SSM_STATE, SSM_HEAD_DIM), F32),
                        pltpu.VMEM((ROW_TILE, d_ssm), F32)],
        compiler_params=_cparams(("parallel", "arbitrary")),
        name="ssd_prompt",
    )(main, main, small, cw, cb, dtb, aneg, dfull, ng)


WIDX_LANE = IDX_DIM + 16


def _sortable_key(score):
    score = jnp.where(score == 0.0, 0.0, score)
    bits = pltpu.bitcast(score, I32)
    return jnp.where(bits < 0, bits ^ jnp.int32(0x7FFFFFFF), bits)


def _lane_blocks(x):
    return [x[:, u * LANES:(u + 1) * LANES] for u in range(x.shape[1] // LANES)]


def _row_total(x):
    return jnp.broadcast_to(jnp.sum(x, axis=1, keepdims=True), x.shape)


def _select_threshold(count, like, ktop, col_bits):
    zero = jnp.zeros(like.shape, I32)

    def t_step(it, t):
        cand = t + lax.shift_left(jnp.int32(1), 31 - it)
        n_ge = count(lambda keys, idx: keys >= cand)
        return jnp.where(n_ge >= ktop, cand, t)

    t = lax.fori_loop(0, 32, t_step, jnp.full(like.shape, INT_MIN, I32))
    n_ge = count(lambda keys, idx: keys >= t)
    n_gt = count(lambda keys, idx: keys > t)
    live = t != INT_MIN
    excess = live & (n_ge > ktop)
    need = ktop - n_gt

    def j_search():
        def j_step(it, j):
            cand = j + lax.shift_left(jnp.int32(1), col_bits - 1 - it)
            n_tie = count(lambda keys, idx: (keys == t) & (idx < cand))
            return jnp.where(n_tie <= need, cand, j)
        return lax.fori_loop(0, col_bits, j_step, zero)

    any_excess = jnp.max(jnp.where(excess, 1, 0)) > 0
    j_cut = lax.cond(any_excess, j_search, lambda: zero)
    big = jnp.int32(2 ** 30)
    j_cut = jnp.where(excess, j_cut, jnp.where(live, big, 0))
    return t, j_cut


def _selected(keys, idx, t, j_cut):
    return (keys > t) | ((keys == t) & (idx < j_cut))


def _index_scores(qi, w, kidx):
    rows, n_h = w.shape
    s = lax.dot_general(qi, kidx, (((1,), (1,)), ((), ())), preferred_element_type=F32)
    s = jnp.maximum(s, 0.0)
    acc = w[:, 0:1] * s[0:rows]
    for h in range(1, n_h):
        acc = acc + w[:, h:h + 1] * s[h * rows:(h + 1) * rows]
    return acc


def _dsa_prompt_kernel(q_ref, qi_ref, small_ref, kb_ref, vbt_ref, kidx_ref, o_ref,
                       key_scr, qaug_scr, s_scr, p_scr, m_scr, acc_scr, *, ktop, pad, tkc, col_bits):
    i = pl.program_id(1)
    n_q, tq, hd = q_ref.shape
    n_idx = qi_ref.shape[0]
    n_kv = kb_ref.shape[0]
    qpk = n_q // n_kv
    r4 = qpk * tq
    q0 = i * tq
    n_chunks = (q0 + tq + tkc - 1) // tkc
    nt_dims = (((1,), (1,)), ((), ()))

    wt = small_ref[...].T[WIDX_LANE:WIDX_LANE + n_idx, :] * ((n_idx * IDX_DIM) ** -0.5)
    qi_all = qi_ref[...].reshape(n_idx * tq, qi_ref.shape[2])
    qcol = q0 + lax.broadcasted_iota(I32, (tkc, tq), 1)
    krow = lax.broadcasted_iota(I32, (tkc, tq), 0)

    def score_chunk(c, carry):
        k0 = pl.multiple_of(c * tkc, tkc)
        s = lax.dot_general(kidx_ref[pl.ds(k0, tkc), :], qi_all, nt_dims, preferred_element_type=F32)
        acc = wt[0:1] * jnp.maximum(s[:, 0:tq], 0.0)
        for h in range(1, n_idx):
            acc = acc + wt[h:h + 1] * jnp.maximum(s[:, h * tq:(h + 1) * tq], 0.0)
        kidx = krow + k0
        admissible = (kidx >= pad) & (kidx <= qcol)
        key_scr[c] = jnp.where(admissible, _sortable_key(acc), INT_MIN)
        return carry

    lax.fori_loop(0, n_chunks, score_chunk, 0)

    fold = 8 * 8

    def count(pred):
        def body(c, cnt):
            hit = jnp.where(pred(key_scr[c], krow + c * tkc), 1, 0)
            return cnt + jnp.sum(hit.reshape(tkc // fold, fold, tq), axis=0)
        cnt = lax.fori_loop(0, n_chunks, body, jnp.zeros((fold, tq), I32))
        return jnp.sum(cnt, axis=0, keepdims=True)

    t, j_cut = _select_threshold(count, jnp.zeros((1, tq), I32), ktop, col_bits)

    rr = lax.broadcasted_iota(I32, (tq, tq), 0)
    cc = lax.broadcasted_iota(I32, (tq, tq), 1)
    onehot = jnp.where(rr == cc, 1.0, 0.0).astype(BF16)
    zeros_tail = jnp.zeros((tq, qaug_scr.shape[1] - tq - hd), BF16)
    for h in range(n_q):
        qaug_scr[h * tq:(h + 1) * tq, :] = jnp.concatenate([onehot, q_ref[h], zeros_tail], axis=1)
    m_scr[...] = jnp.full(m_scr.shape, NEG_BIG, F32)
    acc_scr[...] = jnp.zeros(acc_scr.shape, F32)

    def attend(c, carry):
        k0 = pl.multiple_of(c * tkc, tkc)
        bias_t = jnp.where(_selected(key_scr[c], krow + k0, t, j_cut), 0.0, NEG_BIG).astype(BF16)
        ucols = r4
        units = [(g, part) for g in range(n_kv) for part in range(r4 // ucols)]

        def qk(g, part):
            lhs = jnp.concatenate([bias_t, kb_ref[g, pl.ds(k0, tkc), :]], axis=1)
            rows = slice(g * r4 + part * ucols, g * r4 + (part + 1) * ucols)
            return lax.dot_general(lhs, qaug_scr[rows, :], nt_dims, preferred_element_type=F32)

        s_next = qk(*units[0])
        for n, (g, part) in enumerate(units):
            s = s_next
            if n + 1 < len(units):
                s_next = qk(*units[n + 1])
            cols = slice(part * ucols, (part + 1) * ucols)
            m_prev = m_scr[g * 8:g * 8 + 1, cols]
            m_new = jnp.maximum(m_prev, jnp.max(s, axis=0, keepdims=True))
            m_scr[g * 8:g * 8 + 1, cols] = m_new
            p = jnp.exp2(s - m_new).astype(BF16)
            pv = jnp.dot(vbt_ref[g, :, pl.ds(k0, tkc)], p, preferred_element_type=F32)
            gs = slice(g * LANES, (g + 1) * LANES)
            acc_scr[gs, cols] = jnp.exp2(m_prev - m_new) * acc_scr[gs, cols] + pv
        return carry

    lax.fori_loop(0, n_chunks, attend, 0)
    for pair in range(n_q // 2):
        g, hh = (2 * pair) // qpk, (2 * pair) % qpk
        cols = slice(hh * tq, (hh + 2) * tq)
        denom = acc_scr[g * LANES + hd:g * LANES + hd + 1, cols]
        denom = jnp.where(denom > 0.0, denom, 1.0)
        out_t = acc_scr[g * LANES:g * LANES + hd, cols] / denom
        both = jnp.concatenate([out_t[:, :tq], out_t[:, tq:]], axis=0).T
        o_ref[:, 2 * pair * hd:(2 * pair + 2) * hd] = both.astype(BF16)


def dsa_prompt(q, qi, small, kb, vbt, kidxb, nb, tp, pad, ktop, tkc):
    n_q, _, hd = q.shape
    n_idx = qi.shape[0]
    n_kv = kb.shape[0]
    tq = ROW_TILE
    nt = tp // tq
    r4 = (n_q // n_kv) * tq
    col_bits = max(1, int(math.ceil(math.log2(tp))))
    return pl.pallas_call(
        functools.partial(_dsa_prompt_kernel, ktop=ktop, pad=pad, tkc=tkc, col_bits=col_bits),
        grid=(nb, nt),
        in_specs=[pl.BlockSpec((n_q, tq, hd), lambda b, i: (0, b * nt + i, 0)),
                  pl.BlockSpec((n_idx, tq, IDX_DIM), lambda b, i: (0, b * nt + i, 0)),
                  pl.BlockSpec((tq, LANES), lambda b, i: (b * nt + i, 0)),
                  pl.BlockSpec((n_kv, tp, LANES), lambda b, i: (0, b, 0)),
                  pl.BlockSpec((n_kv, LANES, tp), lambda b, i: (0, 0, b)),
                  pl.BlockSpec((tp, IDX_DIM), lambda b, i: (b, 0))],
        out_specs=pl.BlockSpec((tq, n_q * hd), lambda b, i: (b * nt + i, 0)),
        out_shape=jax.ShapeDtypeStruct((nb * tp, n_q * hd), BF16),
        scratch_shapes=[pltpu.VMEM((tp // tkc, tkc, tq), I32),
                        pltpu.VMEM((n_q * tq, 2 * LANES), BF16),
                        pltpu.VMEM((tkc, r4), F32),
                        pltpu.VMEM((tkc, r4), BF16),
                        pltpu.VMEM((n_kv * 8, r4), F32),
                        pltpu.VMEM((n_kv * LANES, r4), F32)],
        compiler_params=_cparams(("parallel", "arbitrary")),
        name="dsa_prompt",
    )(q, qi, small, kb, vbt, kidxb)


def _lru_sample_kernel(x_ref, g_ref, hist_ref, h0_ref, cw_ref, cb_ref, wg_ref, ba_ref, bx_ref, clam_ref,
                       y_ref, hout_ref, convout_ref, hist_scr, h_scr, *, n_grp):
    t = pl.program_id(0)

    @pl.when(t == 0)
    def _():
        hist_scr[...] = hist_ref[...]
        h_scr[...] = h0_ref[...]

    x = x_ref[...]
    u = cb_ref[...] + cw_ref[CONV_W - 1:CONV_W, :] * x
    for j in range(CONV_W - 1):
        u = u + cw_ref[j:j + 1, :] * hist_scr[j]
    for j in range(CONV_W - 2):
        hist_scr[j] = hist_scr[j + 1]
    hist_scr[CONV_W - 2] = x
    a, b = _lru_gates(u, wg_ref, ba_ref, bx_ref, clam_ref, n_grp)
    h = a * h_scr[...] + b
    h_scr[...] = h
    y_ref[...] = (h * _gelu_tanh(g_ref[...])).astype(BF16)
    hout_ref[...] = h
    convout_ref[...] = hist_scr[...]


def lru_sample(main, xcol, gcol, row0, db, ds, hist, h0, cw, cb, wg, ba, bx, clam):
    d = cw.shape[1]
    n_grp = wg.shape[0]
    blk0 = row0 // db
    const2 = lambda t: (0, 0)
    const3 = lambda t: (0, 0, 0)
    return pl.pallas_call(
        functools.partial(_lru_sample_kernel, n_grp=n_grp),
        grid=(ds,),
        in_specs=[pl.BlockSpec((db, d), lambda t: (blk0 + t, xcol)),
                  pl.BlockSpec((db, d), lambda t: (blk0 + t, gcol)),
                  pl.BlockSpec((CONV_W - 1, db, d), const3),
                  pl.BlockSpec((db, d), const2),
                  pl.BlockSpec((CONV_W, d), const2),
                  pl.BlockSpec((1, d), const2),
                  pl.BlockSpec(wg.shape, const3),
                  pl.BlockSpec((1, d), const2),
                  pl.BlockSpec((1, d), const2),
                  pl.BlockSpec((1, d), const2)],
        out_specs=[pl.BlockSpec((db, d), lambda t: (t, 0)),
                   pl.BlockSpec((db, d), const2),
                   pl.BlockSpec((CONV_W - 1, db, d), const3)],
        out_shape=[jax.ShapeDtypeStruct((ds * db, d), BF16),
                   jax.ShapeDtypeStruct((db, d), F32),
                   jax.ShapeDtypeStruct((CONV_W - 1, db, d), F32)],
        scratch_shapes=[pltpu.VMEM((CONV_W - 1, db, d), F32), pltpu.VMEM((db, d), F32)],
        compiler_params=_cparams(("arbitrary",)),
        name="lru_sample",
    )(main, main, hist, h0, cw, cb, wg, ba, bx, clam)


def _ssd_sample_kernel(*refs, ds, n_heads, n_grp):
    xbc_refs = refs[0:ds]
    z_refs = refs[ds:2 * ds]
    small_refs = refs[2 * ds:3 * ds]
    (hist_ref, h0_ref, cw_ref, cb_ref, dtb_ref, aneg_ref, dfull_ref, ng_ref, exp_ref,
     y_ref, hout_ref, convout_ref, yint_scr) = refs[3 * ds:]
    bb = xbc_refs[0].shape[0]
    d_ssm = z_refs[0].shape[1]
    hpg = n_heads // n_grp
    P = SSM_HEAD_DIM
    N = SSM_STATE
    gw = hpg * P

    ext = [hist_ref[j] for j in range(CONV_W - 1)] + [r[...] for r in xbc_refs]
    for j in range(CONV_W - 1):
        convout_ref[j] = ext[ds + j]
    xbc = []
    for t in range(ds):
        u = cb_ref[...]
        for j in range(CONV_W):
            u = u + cw_ref[j:j + 1, :] * ext[t + j]
        xbc.append(_silu(u))
    xs = [v[:, :d_ssm] for v in xbc]
    bm = [v[:, d_ssm:d_ssm + n_grp * N] for v in xbc]
    cm = [v[:, d_ssm + n_grp * N:] for v in xbc]
    dt = [_softplus(r[...] + dtb_ref[...]) for r in small_refs]
    cum = []
    run = jnp.zeros_like(dt[0])
    for t in range(ds):
        run = run + dt[t] * aneg_ref[...]
        cum.append(run)
    lane = lax.broadcasted_iota(I32, (bb, LANES), 1)
    coefs = []
    for t in range(ds):
        for j in range(t + 1):
            cbh = jnp.zeros((bb, LANES), F32)
            for g in range(n_grp):
                dotg = jnp.sum(cm[t][:, g * N:(g + 1) * N] * bm[j][:, g * N:(g + 1) * N], axis=1, keepdims=True)
                in_g = (lane >= DT_LANE + g * hpg) & (lane < DT_LANE + (g + 1) * hpg)
                cbh = cbh + jnp.where(in_g, dotg, 0.0)
            coefs.append(cbh * jnp.exp(cum[t] - cum[j]) * dt[j])
    n_intra = len(coefs)
    coefs += [jnp.exp(c) for c in cum]
    coefs += [jnp.exp(cum[ds - 1] - cum[j]) * dt[j] for j in range(ds)]
    coefs.append(jnp.exp(cum[ds - 1]))
    stack = jnp.concatenate(coefs, axis=0)
    wide = sum(jnp.dot(part, exp_ref[...], preferred_element_type=F32) for part in _split3_bf16(stack))
    wide = [wide[k * bb:(k + 1) * bb] for k in range(len(coefs))]
    intra_w = wide[:n_intra]
    ecum_w = wide[n_intra:n_intra + ds]
    dend_w = wide[n_intra + ds:n_intra + 2 * ds]
    cdec_w = wide[n_intra + 2 * ds]
    xw = [dend_w[j] * xs[j] for j in range(ds)]
    zpad = 8 - ds
    for b in range(bb):
        for g in range(n_grp):
            def rows_of(arrs, lo, width):
                parts = [a[b:b + 1, lo:lo + width] for a in arrs]
                if zpad:
                    parts.append(jnp.zeros((zpad, width), F32))
                return jnp.concatenate(parts, axis=0).astype(BF16)
            hg = h0_ref[b, g * hpg:(g + 1) * hpg].reshape(gw, N)
            yint = lax.dot_general(rows_of(cm, g * N, N), hg.astype(BF16), (((1,), (1,)), ((), ())),
                                   preferred_element_type=F32)
            for t in range(ds):
                yint_scr[t, b:b + 1, g * gw:(g + 1) * gw] = yint[t:t + 1]
            upd = lax.dot_general(rows_of(xw, g * gw, gw), rows_of(bm, g * N, N), (((0,), (0,)), ((), ())),
                                  preferred_element_type=F32)
            for hh in range(hpg):
                h = g * hpg + hh
                cd = cdec_w[b:b + 1, h * P:h * P + 1]
                hout_ref[b, h] = cd * hg[hh * P:(hh + 1) * P] + upd[hh * P:(hh + 1) * P]
    k = 0
    for t in range(ds):
        y = ecum_w[t] * yint_scr[t] + dfull_ref[...] * xs[t]
        for j in range(t + 1):
            y = y + intra_w[k] * xs[j]
            k += 1
        y = y * _silu(z_refs[t][...])
        y_ref[t] = _rmsnorm_rows(y, ng_ref[...]).astype(BF16)


def ssd_sample(main, xbc_col, z_col, small, row0, db, ds, bb, hist, h0, cw, cb, dtb, aneg, dfull, ng, expand,
               n_heads, n_grp):
    d_xbc = cw.shape[1]
    d_ssm = ng.shape[1]
    const2 = lambda i: (0, 0)

    def at_t(t, width, col):
        blk0 = (row0 + t * db) // bb
        return pl.BlockSpec((bb, width), lambda i: (blk0 + i, col))

    in_specs = ([at_t(t, d_xbc, xbc_col) for t in range(ds)] + [at_t(t, d_ssm, z_col) for t in range(ds)]
                + [at_t(t, LANES, 0) for t in range(ds)]
                + [pl.BlockSpec((CONV_W - 1, bb, d_xbc), lambda i: (0, i, 0)),
                   pl.BlockSpec((bb, n_heads, SSM_HEAD_DIM, SSM_STATE), lambda i: (i, 0, 0, 0)),
                   pl.BlockSpec((CONV_W, d_xbc), const2),
                   pl.BlockSpec((1, d_xbc), const2),
                   pl.BlockSpec((1, LANES), const2),
                   pl.BlockSpec((1, LANES), const2),
                   pl.BlockSpec((1, d_ssm), const2),
                   pl.BlockSpec((1, d_ssm), const2),
                   pl.BlockSpec((LANES, d_ssm), const2)])
    return pl.pallas_call(
        functools.partial(_ssd_sample_kernel, ds=ds, n_heads=n_heads, n_grp=n_grp),
        grid=(db // bb,),
        in_specs=in_specs,
        out_specs=[pl.BlockSpec((ds, bb, d_ssm), lambda i: (0, i, 0)),
                   pl.BlockSpec((bb, n_heads, SSM_HEAD_DIM, SSM_STATE), lambda i: (i, 0, 0, 0)),
                   pl.BlockSpec((CONV_W - 1, bb, d_xbc), lambda i: (0, i, 0))],
        out_shape=[jax.ShapeDtypeStruct((ds, db, d_ssm), BF16),
                   jax.ShapeDtypeStruct((db, n_heads, SSM_HEAD_DIM, SSM_STATE), F32),
                   jax.ShapeDtypeStruct((CONV_W - 1, db, d_xbc), F32)],
        scratch_shapes=[pltpu.VMEM((ds, bb, d_ssm), F32)],
        compiler_params=_cparams(("parallel",)),
        name="ssd_sample",
    )(*([main] * ds + [main] * ds + [small] * ds + [hist, h0, cw, cb, dtb, aneg, dfull, ng, expand]))


def _dsa_sample_kernel(pt_ref, *refs, n_pages, ds, ktop):
    del pt_ref
    q_ref, qi_ref, w_ref, knew_ref, vnew_ref, kxnew_ref = refs[0:6]
    kx_pages = refs[6:6 + n_pages]
    k_pages = refs[6 + n_pages:6 + 2 * n_pages]
    v_pages = refs[6 + 2 * n_pages:6 + 3 * n_pages]
    o_ref, key_scr, logit_scr, new_scr = refs[6 + 3 * n_pages:]
    n_rows, hd = q_ref.shape[1:]
    n_q = n_rows // ds
    n_idx, rq = qi_ref.shape[1:3]
    kw = knew_ref.shape[2]
    n_kv = kw // hd
    qpk = n_q // n_kv
    n_chunks = n_pages + 1

    w = w_ref[0] * ((n_idx * IDX_DIM) ** -0.5)
    qi = qi_ref[0].reshape(n_idx * rq, qi_ref.shape[3]).astype(BF16)
    qrow = lax.broadcasted_iota(I32, (rq, PAGE_SIZE), 0)
    lane = lax.broadcasted_iota(I32, (rq, PAGE_SIZE), 1)
    for p in range(n_pages):
        sc = _index_scores(qi, w, kx_pages[p][0].astype(BF16))
        key_scr[0, :, p * PAGE_SIZE:(p + 1) * PAGE_SIZE] = jnp.where(qrow < ds, _sortable_key(sc), INT_MIN)
    new_scr[...] = jnp.zeros(new_scr.shape, F32)
    new_scr[0:rq, 0:kxnew_ref.shape[2]] = kxnew_ref[0]
    sc = _index_scores(qi, w, new_scr[:, 0:kxnew_ref.shape[2]].astype(BF16))
    admissible = (lane <= qrow) & (qrow < ds)
    key_scr[0, :, n_pages * PAGE_SIZE:] = jnp.where(admissible, _sortable_key(sc), INT_MIN)
    col_bits = max(1, int(math.ceil(math.log2(n_chunks * PAGE_SIZE))))

    def count(pred):
        cnt = jnp.zeros((rq, PAGE_SIZE), I32)
        for p in range(n_chunks):
            kk = key_scr[0, :, p * PAGE_SIZE:(p + 1) * PAGE_SIZE]
            cnt = cnt + jnp.where(pred(kk, lane + p * PAGE_SIZE), 1, 0)
        return _row_total(cnt)

    t, j_cut = _select_threshold(count, lane, ktop, col_bits)

    assert n_q & (n_q - 1) == 0 and qpk & (qpk - 1) == 0 and hd & (hd - 1) == 0
    row_group = (lax.broadcasted_iota(I32, (n_rows, kw), 0) & (n_q - 1)) >> (qpk.bit_length() - 1)
    own_lanes = (lax.broadcasted_iota(I32, (n_rows, kw), 1) >> (hd.bit_length() - 1)) == row_group
    qexp = jnp.where(own_lanes, jnp.concatenate([q_ref[0]] * n_kv, axis=1), 0.0).astype(BF16)
    new_scr[0:rq, :] = knew_ref[0]
    knew = new_scr[...].astype(BF16)
    new_scr[0:rq, :] = vnew_ref[0]
    vnew = new_scr[...].astype(BF16)
    for p in range(n_chunks):
        kk = key_scr[0, :, p * PAGE_SIZE:(p + 1) * PAGE_SIZE]
        bias = jnp.where(_selected(kk, lane + p * PAGE_SIZE, t, j_cut), 0.0, NEG_BIG)
        bias_rows = jnp.concatenate([jnp.broadcast_to(bias[ts:ts + 1], (n_q, PAGE_SIZE)) for ts in range(ds)], axis=0)
        kc = k_pages[p][0].astype(BF16) if p < n_pages else knew
        logit_scr[:, p * PAGE_SIZE:(p + 1) * PAGE_SIZE] = bias_rows + lax.dot_general(
            qexp, kc, (((1,), (1,)), ((), ())), preferred_element_type=F32)
    logits = logit_scr[...]
    prob = jnp.exp2(logits - jnp.max(logits, axis=1, keepdims=True))
    denom = jnp.sum(prob, axis=1, keepdims=True)
    prob = prob.astype(BF16)
    acc = jnp.zeros((n_rows, kw), F32)
    for p in range(n_chunks):
        vc = v_pages[p][0].astype(BF16) if p < n_pages else vnew
        acc = acc + jnp.dot(prob[:, p * PAGE_SIZE:(p + 1) * PAGE_SIZE], vc, preferred_element_type=F32)
    out = jnp.zeros((n_rows, hd), F32)
    for g in range(n_kv):
        out = out + jnp.where(row_group[:, :hd] == g, acc[:, g * hd:(g + 1) * hd], 0.0)
    o_ref[0] = out / denom


def dsa_sample(page_table, q, qi, w, knew, vnew, kxnew, cache_k, cache_v, cache_kidx, layer, n_pool, ds, ktop):
    db, n_rows, hd = q.shape
    n_idx, rq = qi.shape[1:3]
    kw = knew.shape[2]
    n_pages = page_table.shape[1]

    def page_spec(width, p):
        return pl.BlockSpec((1, PAGE_SIZE, width), lambda b, pt: (layer * n_pool + pt[b, p], 0, 0))

    own3 = lambda b, pt: (b, 0, 0)
    own4 = lambda b, pt: (b, 0, 0, 0)
    in_specs = ([pl.BlockSpec((1, n_rows, hd), own3),
                 pl.BlockSpec((1, n_idx, rq, IDX_DIM), own4),
                 pl.BlockSpec((1, rq, n_idx), own3),
                 pl.BlockSpec((1, rq, kw), own3),
                 pl.BlockSpec((1, rq, kw), own3),
                 pl.BlockSpec((1, rq, IDX_DIM), own3)]
                + [page_spec(IDX_DIM, p) for p in range(n_pages)]
                + [page_spec(kw, p) for p in range(n_pages)]
                + [page_spec(kw, p) for p in range(n_pages)])
    grid_spec = pltpu.PrefetchScalarGridSpec(
        num_scalar_prefetch=1,
        grid=(db,),
        in_specs=in_specs,
        out_specs=pl.BlockSpec((1, n_rows, hd), own3),
        scratch_shapes=[pltpu.VMEM((1, rq, (n_pages + 1) * PAGE_SIZE), I32),
                        pltpu.VMEM((n_rows, (n_pages + 1) * PAGE_SIZE), F32),
                        pltpu.VMEM((PAGE_SIZE, kw), F32)])
    return pl.pallas_call(
        functools.partial(_dsa_sample_kernel, n_pages=n_pages, ds=ds, ktop=ktop),
        grid_spec=grid_spec,
        out_shape=jax.ShapeDtypeStruct((db, n_rows, hd), F32),
        compiler_params=_cparams(("arbitrary",)),
        name="dsa_sample",
    )(page_table, q, qi, w, knew, vnew, kxnew,
      *([cache_kidx] * n_pages + [cache_k] * n_pages + [cache_v] * n_pages))


def _round_up(x, m):
    return (x + m - 1) // m * m


def _largest_tile(total, unit, cap):
    best = unit
    for k in range(1, cap // unit + 1):
        if total % (k * unit) == 0:
            best = k * unit
    return best


def _block_diag_groups(w, per_group):
    nb, r, _ = w.shape
    ng = nb // per_group
    w = w.reshape(ng, per_group, r, r)
    eye = jnp.eye(per_group, dtype=w.dtype)
    return jnp.einsum('gbij,bc->gbicj', w, eye).reshape(ng, per_group * r, per_group * r)


def _pad_rows_to(a, axis, n):
    pad = [(0, 0)] * a.ndim
    pad[axis] = (0, n - a.shape[axis])
    return jnp.pad(a, pad)


def kernel(x_prompt, x_sample, cache_k, cache_v, cache_kidx, state_ssm, state_ssm_conv, state_lru_h, state_lru_conv, page_table, meta_tokens, norm1, w_in, lru_conv_w, lru_conv_b, lru_wa, lru_ba, lru_wx, lru_bx, lru_lambda, ssm_conv_w, ssm_conv_b, ssm_dt_bias, ssm_a_log, ssm_d, ssm_norm, p_lru, p_ssm, p_attn, w_o, norm2, w_up, w_down, final_norm):
    nb, seq, d = x_prompt.shape
    db, ds, _ = x_sample.shape
    depth = w_in.shape[0]
    n_pages = page_table.shape[1]
    past = n_pages * PAGE_SIZE
    n_pool = cache_k.shape[1]
    n_kv = cache_k.shape[3]
    kw = n_kv * HEAD_DIM
    d_rnn = lru_conv_w.shape[2]
    d_xbc = ssm_conv_w.shape[2]
    d_ssm = ssm_norm.shape[1]
    n_ssm_heads = ssm_d.shape[1]
    n_heads = p_attn.shape[1] // HEAD_DIM
    n_idx = N_IDX_HEADS
    d_ff = w_up.shape[2]
    assert d_rnn == d and d_ssm == d and d_xbc == 2 * d and n_heads * HEAD_DIM == d
    assert seq % DSA_KEY_UNIT == 0 and ds <= 8 and n_ssm_heads <= LANES - DT_LANE

    pad = (-N_META) % DSA_KEY_UNIT
    tp = pad + N_META + seq
    mp = nb * tp
    ms = db * ds
    mt = _round_up(mp + ms, 2 * DENSE_TM)
    bb = 8
    assert mp % db == 0 and db % bb == 0
    ktop_p = min(TOP_K_MAX, seq // TOP_K_FRACTION)
    ktop_s = min(TOP_K_MAX, (past + ds) // TOP_K_FRACTION)
    tkc = _largest_tile(tp, DSA_KEY_UNIT, DSA_KEY_CHUNK_MAX)

    xp = jnp.concatenate([jnp.zeros((nb, pad, d), F32),
                          jnp.broadcast_to(meta_tokens[None].astype(F32), (nb, N_META, d)),
                          x_prompt], axis=1).reshape(mp, d)
    xs = jnp.swapaxes(x_sample, 0, 1).reshape(ms, d)
    x = jnp.concatenate([xp, xs, jnp.zeros((mt - mp - ms, d), F32)], axis=0)
    r = np.arange(mt)
    in_prompt = r < mp
    rp = r % tp
    valid_np = np.where(in_prompt, rp >= pad, r < mp + ms)
    pos_np = np.where(in_prompt, np.maximum(rp - pad, 0), np.where(r < mp + ms, past + (r - mp) // db, 0))
    valid = jnp.asarray(valid_np.astype(np.float32)).reshape(mt, 1)
    half = HEAD_DIM // 2
    freq = ROPE_THETA ** (-jnp.arange(half, dtype=F32) / half)
    ang = jnp.asarray(pos_np).astype(F32)[:, None] * freq[None, :]
    cos = jnp.cos(ang)
    sin = jnp.sin(ang)
    cos_t = jnp.concatenate([cos] * (LANES // half), axis=1)
    sin_t = jnp.concatenate([-sin, sin] * (LANES // HEAD_DIM), axis=1)

    splits = (d_rnn, d_rnn, d_ssm, d_xbc, n_ssm_heads, n_heads * HEAD_DIM, kw, kw, n_idx * IDX_DIM, IDX_DIM,
              n_idx, N_BRANCH * d)
    assert sum(splits) == w_in.shape[2]
    off = np.concatenate([[0], np.cumsum(splits)])
    seg = lambda w, i: w[:, off[i]:off[i + 1]]
    LRU_X, LRU_G, SSM_Z, SSM_XBC, SSM_DT, Q, K, V, QI, KIDX, WIDX, GATES = range(12)
    XBC_COL, X_COL, G_COL, Z_COL, GATE_COL = 0, 2, 3, 4, 5
    lane_pos = np.arange(LANES)
    head_lanes = (lane_pos >= DT_LANE) & (lane_pos < DT_LANE + n_ssm_heads)
    expand_np = np.zeros((LANES, d_ssm), np.float32)
    for h in range(n_ssm_heads):
        expand_np[DT_LANE + h, h * SSM_HEAD_DIM:(h + 1) * SSM_HEAD_DIM] = 1.0
    expand = jnp.asarray(expand_np).astype(BF16)

    def to_head_lanes(v):
        return jnp.zeros((1, LANES), F32).at[0, DT_LANE:DT_LANE + n_ssm_heads].set(v)

    outs_p = [[] for _ in range(7)]
    outs_s = [[] for _ in range(7)]
    for l in range(depth):
        w = w_in[l]
        w_main = jnp.concatenate([seg(w, SSM_XBC), seg(w, LRU_X), seg(w, LRU_G), seg(w, SSM_Z), seg(w, GATES)],
                                 axis=1).astype(BF16)
        tail = LANES - IDX_DIM - n_ssm_heads - n_idx
        w_attn = jnp.concatenate([seg(w, Q), seg(w, QI), seg(w, K), seg(w, V), seg(w, KIDX), seg(w, SSM_DT),
                                  seg(w, WIDX), jnp.zeros((d, tail), F32)], axis=1).astype(BF16)
        per_group = 256 // (d_rnn // N_RNN_BLOCKS)
        wg = jnp.concatenate([_block_diag_groups(lru_wa[l], per_group), _block_diag_groups(lru_wx[l], per_group)],
                             axis=2).astype(BF16)
        clam = (-LRU_C * jax.nn.softplus(-lru_lambda[l])).reshape(1, d_rnn)
        lru_args = (lru_conv_w[l], lru_conv_b[l].reshape(1, -1), wg, lru_ba[l].reshape(1, -1),
                    lru_bx[l].reshape(1, -1), clam)
        dtb = to_head_lanes(ssm_dt_bias[l])
        aneg = to_head_lanes(-jnp.exp(ssm_a_log[l]))
        dfull = jnp.repeat(ssm_d[l], SSM_HEAD_DIM).reshape(1, d_ssm)
        ssm_args = (ssm_conv_w[l], ssm_conv_b[l].reshape(1, -1), dtb, aneg, dfull, ssm_norm[l].reshape(1, -1))

        main = norm_matmul(x, norm1[l].reshape(1, d), w_main, 2 * DENSE_TM, 2048)
        q, qi, kf, vf, small, kb, vbt, kidxb = attn_proj(x, norm1[l].reshape(1, d), w_attn, cos_t, sin_t,
                                                        n_heads, n_idx, n_kv, DENSE_TM)
        ya_p, hlru_p = lru_prompt(main, X_COL, G_COL, nb, tp, pad, *lru_args)
        yb_p, hssm_p = ssd_prompt(main, XBC_COL, Z_COL, small, nb, tp, pad, *ssm_args, n_ssm_heads, N_SSM_GROUPS)
        yc_p = dsa_prompt(q, qi, small, kb, vbt, kidxb, nb, tp, pad, ktop_p, tkc)
        ya_s, hlru_s, conv_lru_s = lru_sample(main, X_COL, G_COL, mp, db, ds,
                                              jnp.swapaxes(state_lru_conv[l], 0, 1), state_lru_h[l], *lru_args)
        yb_s, hssm_s, conv_ssm_s = ssd_sample(main, XBC_COL, Z_COL, small, mp, db, ds, bb,
                                              jnp.swapaxes(state_ssm_conv[l], 0, 1), state_ssm[l], *ssm_args,
                                              expand, n_ssm_heads, N_SSM_GROUPS)

        def batch_major(a):
            lead = a.shape[:-2]
            a = a.astype(F32).reshape(lead + (ds, db, a.shape[-1]))
            a = jnp.moveaxis(a, len(lead) + 1, 0)
            return _pad_rows_to(a, a.ndim - 2, 8)

        small_s = batch_major(small[mp:mp + ms])
        q_s = jnp.transpose(q[:, mp:mp + ms].astype(F32).reshape(n_heads, ds, db, HEAD_DIM), (2, 1, 0, 3))
        yc_s = dsa_sample(page_table, q_s.reshape(db, ds * n_heads, HEAD_DIM), batch_major(qi[:, mp:mp + ms]),
                          small_s[:, :, WIDX_LANE:WIDX_LANE + n_idx], batch_major(kf[mp:mp + ms]),
                          batch_major(vf[mp:mp + ms]), small_s[:, :, :IDX_DIM],
                          cache_k.reshape(depth * n_pool, PAGE_SIZE, kw),
                          cache_v.reshape(depth * n_pool, PAGE_SIZE, kw),
                          cache_kidx.reshape(depth * n_pool, PAGE_SIZE, IDX_DIM), l, n_pool, ds, ktop_s)
        yc_s = jnp.swapaxes(yc_s.reshape(db, ds, n_heads * HEAD_DIM), 0, 1).reshape(ms, n_heads * HEAD_DIM).astype(BF16)

        tail_rows = jnp.zeros((mt - mp - ms, d), BF16)
        ya = jnp.concatenate([ya_p, ya_s, tail_rows], axis=0)
        yb = jnp.concatenate([yb_p, yb_s.reshape(ms, d_ssm), tail_rows], axis=0)
        yc = jnp.concatenate([yc_p, yc_s, tail_rows], axis=0)
        x = merge(ya, yb, yc, main, GATE_COL, x, valid, p_lru[l].astype(BF16), p_ssm[l].astype(BF16),
                  p_attn[l].astype(BF16), w_o[l].astype(BF16), DENSE_TM)
        x = mlp(x, norm2[l].reshape(1, d), w_up[l].astype(BF16), w_down[l].astype(BF16), 2 * DENSE_TM, 1024)

        prm = lambda a: a[:mp].reshape((nb, tp) + a.shape[1:])
        smp = lambda a: jnp.swapaxes(a[mp:mp + ms].reshape((ds, db) + a.shape[1:]), 0, 1)
        last = CONV_W - 1
        conv_tail = lambda c0, width: jnp.stack([main[(b + 1) * tp - last:(b + 1) * tp, c0:c0 + width]
                                                 for b in range(nb)])
        rows_p = (prm(kf)[:, pad:].reshape(nb, tp - pad, n_kv, HEAD_DIM),
                  prm(vf)[:, pad:].reshape(nb, tp - pad, n_kv, HEAD_DIM),
                  prm(small)[:, pad:, :IDX_DIM],
                  jnp.swapaxes(hssm_p, 2, 3),
                  conv_tail(XBC_COL * d, d_xbc),
                  hlru_p[:, 0],
                  conv_tail(X_COL * d, d_rnn))
        rows_s = (smp(kf).reshape(db, ds, n_kv, HEAD_DIM),
                  smp(vf).reshape(db, ds, n_kv, HEAD_DIM),
                  smp(small)[:, :, :IDX_DIM],
                  hssm_s,
                  jnp.swapaxes(conv_ssm_s, 0, 1),
                  hlru_s,
                  jnp.swapaxes(conv_lru_s, 0, 1))
        for acc, val in zip(outs_p, rows_p):
            acc.append(val)
        for acc, val in zip(outs_s, rows_s):
            acc.append(val)

    y = final_rmsnorm(x, final_norm.reshape(1, d), DENSE_TM)
    y_prompt = y[:mp].reshape(nb, tp, d)[:, pad + N_META:]
    y_sample = jnp.swapaxes(y[mp:mp + ms].reshape(ds, db, d), 0, 1)
    return (y_prompt, y_sample) + tuple(jnp.stack(a) for a in outs_p) + tuple(jnp.stack(a) for a in outs_s)
```

```python
import functools
import math

import numpy as np
import jax
import jax.numpy as jnp
from jax import lax
from jax.experimental import pallas as pl
from jax.experimental.pallas import tpu as pltpu

F32 = jnp.float32
BF16 = jnp.bfloat16
I32 = jnp.int32

N_META = 16
CONV_W = 4
EPS = 1e-6
LRU_C = 8.0
N_RNN_BLOCKS = 16
SSM_HEAD_DIM = 64
N_SSM_GROUPS = 4
SSM_STATE = 128
SSM_CHUNK = 128
HEAD_DIM = 64
N_KV_HEADS = 4
N_IDX_HEADS = 8
IDX_DIM = 64
TOP_K_MAX = 256
TOP_K_FRACTION = 4
ROPE_THETA = 10000.0
PAGE_SIZE = 128
N_BRANCH = 3

LANES = 128
ROW_TILE = 128
MXU_DIM = 256
DSA_KEY_UNIT = MXU_DIM
DSA_KEY_CHUNK_MAX = 3 * MXU_DIM
DENSE_TM = 512
VMEM_LIMIT = 56 * 1024 * 1024
INT_MIN = -(2 ** 31)
NEG_BIG = -1e30


def _cparams(sem):
    return pltpu.CompilerParams(dimension_semantics=sem, vmem_limit_bytes=VMEM_LIMIT)


def _rmsnorm_rows(x, g):
    return x * lax.rsqrt(jnp.mean(x * x, axis=-1, keepdims=True) + EPS) * g


def _norm_matmul_kernel(x_ref, g_ref, w_ref, o_ref, h_scr):
    @pl.when(pl.program_id(1) == 0)
    def _():
        h_scr[...] = _rmsnorm_rows(x_ref[...], g_ref[...]).astype(BF16)

    o_ref[...] = jnp.dot(h_scr[...], w_ref[...], preferred_element_type=F32)


def norm_matmul(x, g, w, tm, tn):
    m, d = x.shape
    n = w.shape[1]
    return pl.pallas_call(
        _norm_matmul_kernel,
        grid=(m // tm, n // tn),
        in_specs=[pl.BlockSpec((tm, d), lambda i, j: (i, 0)),
                  pl.BlockSpec((1, d), lambda i, j: (0, 0)),
                  pl.BlockSpec((d, tn), lambda i, j: (0, j))],
        out_specs=pl.BlockSpec((tm, tn), lambda i, j: (i, j)),
        out_shape=jax.ShapeDtypeStruct((m, n), F32),
        scratch_shapes=[pltpu.VMEM((tm, d), BF16)],
        compiler_params=_cparams(("parallel", "arbitrary")),
        name="inproj_main",
    )(x, g, w)


def _attn_proj_kernel(x_ref, g_ref, w_ref, cos_ref, sin_ref,
                      q_ref, qi_ref, k_ref, v_ref, small_ref, kb_ref, vbt_ref, kidxb_ref,
                      *, n_heads, n_idx, n_kv):
    h = _rmsnorm_rows(x_ref[...], g_ref[...]).astype(BF16)
    acc = jnp.dot(h, w_ref[...], preferred_element_type=F32)
    cos = cos_ref[...]
    sin = sin_ref[...]
    lane = lax.broadcasted_iota(I32, cos.shape, 1)
    lo_half = (lane & (HEAD_DIM // 2)) == 0

    def rope(xb, c, s):
        partner = jnp.where(lo_half, pltpu.roll(xb, LANES - HEAD_DIM // 2, 1), pltpu.roll(xb, HEAD_DIM // 2, 1))
        return xb * c + partner * s

    col = 0
    scale = HEAD_DIM ** -0.5 * math.log2(math.e)
    for p in range(n_heads // 2):
        blk = rope(acc[:, col:col + LANES], cos, sin) * scale
        q_ref[2 * p] = blk[:, :HEAD_DIM].astype(BF16)
        q_ref[2 * p + 1] = blk[:, HEAD_DIM:].astype(BF16)
        col += LANES
    for p in range(n_idx // 2):
        blk = rope(acc[:, col:col + LANES], cos, sin)
        qi_ref[2 * p] = blk[:, :IDX_DIM].astype(BF16)
        qi_ref[2 * p + 1] = blk[:, IDX_DIM:].astype(BF16)
        col += LANES
    kw = n_kv * HEAD_DIM
    first = lane < HEAD_DIM
    ones_lane = jnp.where(lane == HEAD_DIM, 1.0, 0.0)
    for p in range(kw // LANES):
        blk = rope(acc[:, col:col + LANES], cos, sin)
        k_ref[:, p * LANES:(p + 1) * LANES] = blk
        kb_ref[2 * p] = jnp.where(first, blk, 0.0).astype(BF16)
        kb_ref[2 * p + 1] = jnp.where(first, pltpu.roll(blk, HEAD_DIM, 1), 0.0).astype(BF16)
        col += LANES
    v_ref[...] = acc[:, col:col + kw]
    for p in range(kw // LANES):
        blk = acc[:, col:col + LANES]
        vbt_ref[2 * p] = jnp.where(first, blk, ones_lane).T.astype(BF16)
        vbt_ref[2 * p + 1] = jnp.where(first, pltpu.roll(blk, HEAD_DIM, 1), ones_lane).T.astype(BF16)
        col += LANES
    is_kidx = lane < IDX_DIM
    blk = rope(acc[:, col:col + LANES], jnp.where(is_kidx, cos, 1.0), jnp.where(is_kidx, sin, 0.0))
    small_ref[...] = blk
    kidxb_ref[...] = blk[:, :IDX_DIM].astype(BF16)


def attn_proj(x, g, w, cos, sin, n_heads, n_idx, n_kv, tm):
    m, d = x.shape
    n = w.shape[1]
    kw = n_kv * HEAD_DIM
    row = lambda i: (i, 0)
    return pl.pallas_call(
        functools.partial(_attn_proj_kernel, n_heads=n_heads, n_idx=n_idx, n_kv=n_kv),
        grid=(m // tm,),
        in_specs=[pl.BlockSpec((tm, d), row),
                  pl.BlockSpec((1, d), lambda i: (0, 0)),
                  pl.BlockSpec((d, n), lambda i: (0, 0)),
                  pl.BlockSpec((tm, LANES), row),
                  pl.BlockSpec((tm, LANES), row)],
        out_specs=[pl.BlockSpec((n_heads, tm, HEAD_DIM), lambda i: (0, i, 0)),
                   pl.BlockSpec((n_idx, tm, IDX_DIM), lambda i: (0, i, 0)),
                   pl.BlockSpec((tm, kw), row),
                   pl.BlockSpec((tm, kw), row),
                   pl.BlockSpec((tm, LANES), row),
                   pl.BlockSpec((n_kv, tm, LANES), lambda i: (0, i, 0)),
                   pl.BlockSpec((n_kv, LANES, tm), lambda i: (0, 0, i)),
                   pl.BlockSpec((tm, IDX_DIM), row)],
        out_shape=[jax.ShapeDtypeStruct((n_heads, m, HEAD_DIM), BF16),
                   jax.ShapeDtypeStruct((n_idx, m, IDX_DIM), BF16),
                   jax.ShapeDtypeStruct((m, kw), F32),
                   jax.ShapeDtypeStruct((m, kw), F32),
                   jax.ShapeDtypeStruct((m, LANES), F32),
                   jax.ShapeDtypeStruct((n_kv, m, LANES), BF16),
                   jax.ShapeDtypeStruct((n_kv, LANES, m), BF16),
                   jax.ShapeDtypeStruct((m, IDX_DIM), BF16)],
        compiler_params=_cparams(("parallel",)),
        name="inproj_attn",
    )(x, g, w, cos, sin)


def _merge_kernel(ya_ref, yb_ref, yc_ref, g0_ref, g1_ref, g2_ref, x_ref, valid_ref,
                  pa_ref, pb_ref, pc_ref, wo_ref, o_ref):
    def branch(y_ref, p_ref, g_ref):
        return jax.nn.sigmoid(g_ref[...]) * jnp.dot(y_ref[...], p_ref[...], preferred_element_type=F32)

    merged = branch(ya_ref, pa_ref, g0_ref) + branch(yb_ref, pb_ref, g1_ref) + branch(yc_ref, pc_ref, g2_ref)
    xn = x_ref[...] + jnp.dot(merged.astype(BF16), wo_ref[...], preferred_element_type=F32)
    o_ref[...] = jnp.where(valid_ref[...] > 0.0, xn, 0.0)


def merge(ya, yb, yc, main, gate_col, x, valid, pa, pb, pc, wo, tm):
    m, d = x.shape
    row = lambda i: (i, 0)
    wspec = pl.BlockSpec((d, d), lambda i: (0, 0))
    return pl.pallas_call(
        _merge_kernel,
        grid=(m // tm,),
        in_specs=[pl.BlockSpec((tm, d), row), pl.BlockSpec((tm, d), row), pl.BlockSpec((tm, d), row),
                  pl.BlockSpec((tm, d), lambda i: (i, gate_col)),
                  pl.BlockSpec((tm, d), lambda i: (i, gate_col + 1)),
                  pl.BlockSpec((tm, d), lambda i: (i, gate_col + 2)),
                  pl.BlockSpec((tm, d), row),
                  pl.BlockSpec((tm, 1), row),
                  wspec, wspec, wspec, wspec],
        out_specs=pl.BlockSpec((tm, d), row),
        out_shape=jax.ShapeDtypeStruct((m, d), F32),
        compiler_params=_cparams(("parallel",)),
        name="merge",
    )(ya, yb, yc, main, main, main, x, valid, pa, pb, pc, wo)


def _mlp_kernel(x_ref, g_ref, wu_ref, wd_ref, o_ref, h_scr, acc_scr):
    j = pl.program_id(1)

    @pl.when(j == 0)
    def _():
        h_scr[...] = _rmsnorm_rows(x_ref[...], g_ref[...]).astype(BF16)
        acc_scr[...] = x_ref[...]

    u = jnp.dot(h_scr[...], wu_ref[...], preferred_element_type=F32)
    u = jnp.square(jnp.maximum(u, 0.0)).astype(BF16)
    acc_scr[...] += jnp.dot(u, wd_ref[...], preferred_element_type=F32)

    @pl.when(j == pl.num_programs(1) - 1)
    def _():
        o_ref[...] = acc_scr[...]


def mlp(x, g, wu, wd, tm, tf):
    m, d = x.shape
    f = wu.shape[1]
    return pl.pallas_call(
        _mlp_kernel,
        grid=(m // tm, f // tf),
        in_specs=[pl.BlockSpec((tm, d), lambda i, j: (i, 0)),
                  pl.BlockSpec((1, d), lambda i, j: (0, 0)),
                  pl.BlockSpec((d, tf), lambda i, j: (0, j)),
                  pl.BlockSpec((tf, d), lambda i, j: (j, 0))],
        out_specs=pl.BlockSpec((tm, d), lambda i, j: (i, 0)),
        out_shape=jax.ShapeDtypeStruct((m, d), F32),
        scratch_shapes=[pltpu.VMEM((tm, d), BF16), pltpu.VMEM((tm, d), F32)],
        compiler_params=_cparams(("parallel", "arbitrary")),
        name="mlp",
    )(x, g, wu, wd)


def _final_norm_kernel(x_ref, g_ref, o_ref):
    o_ref[...] = _rmsnorm_rows(x_ref[...], g_ref[...])


def final_rmsnorm(x, g, tm):
    m, d = x.shape
    return pl.pallas_call(
        _final_norm_kernel,
        grid=(m // tm,),
        in_specs=[pl.BlockSpec((tm, d), lambda i: (i, 0)), pl.BlockSpec((1, d), lambda i: (0, 0))],
        out_specs=pl.BlockSpec((tm, d), lambda i: (i, 0)),
        out_shape=jax.ShapeDtypeStruct((m, d), F32),
        compiler_params=_cparams(("parallel",)),
        name="final_norm",
    )(x, g)


def _softplus(x):
    return jnp.maximum(x, 0.0) + jnp.log(1.0 + jnp.exp(-jnp.abs(x)))


def _silu(x):
    return x * jax.nn.sigmoid(x)


def _gelu_tanh(x):
    return 0.5 * x * (1.0 + jnp.tanh(math.sqrt(2.0 / math.pi) * (x + 0.044715 * (x * x * x))))


def _causal_conv_rows(ext_ref, x, w_ref, b_ref, rows):
    ext_ref[8:8 + rows, :] = x
    out = b_ref[...] + w_ref[CONV_W - 1:CONV_W, :] * x
    for j in range(CONV_W - 1):
        out = out + w_ref[j:j + 1, :] * ext_ref[5 + j:5 + j + rows, :]
    ext_ref[5:8, :] = ext_ref[5 + rows:8 + rows, :]
    return out


def _lru_gates(u, wg_ref, ba_ref, bx_ref, clam_ref, n_grp):
    gw = u.shape[1] // n_grp
    rs, is_ = [], []
    for g in range(n_grp):
        rg = jnp.dot(u[:, g * gw:(g + 1) * gw].astype(BF16), wg_ref[g], preferred_element_type=F32)
        rs.append(rg[:, :gw])
        is_.append(rg[:, gw:])
    r = jax.nn.sigmoid(jnp.concatenate(rs, axis=1) + ba_ref[...])
    i = jax.nn.sigmoid(jnp.concatenate(is_, axis=1) + bx_ref[...])
    log_a = clam_ref[...] * r
    a = jnp.exp(log_a)
    b = jnp.sqrt(1.0 - jnp.exp(2.0 * log_a)) * (i * u)
    return a, b


def _lru_prompt_kernel(x_ref, g_ref, cw_ref, cb_ref, wg_ref, ba_ref, bx_ref, clam_ref,
                       y_ref, hout_ref, ext_scr, h_scr, *, pad, n_grp):
    c = pl.program_id(1)
    rows = x_ref.shape[0]

    @pl.when(c == 0)
    def _():
        ext_scr[0:8, :] = jnp.zeros((8, ext_scr.shape[1]), F32)
        h_scr[...] = jnp.zeros(h_scr.shape, F32)

    u = _causal_conv_rows(ext_scr, x_ref[...], cw_ref, cb_ref, rows)
    a, b = _lru_gates(u, wg_ref, ba_ref, bx_ref, clam_ref, n_grp)
    ridx = lax.broadcasted_iota(I32, (rows, 1), 0)
    b = jnp.where(c * rows + ridx >= pad, b, 0.0)
    s = 1
    while s < rows:
        keep = ridx >= s
        a_sh = jnp.where(keep, pltpu.roll(a, s, 0), 1.0)
        b_sh = jnp.where(keep, pltpu.roll(b, s, 0), 0.0)
        b = a * b_sh + b
        a = a * a_sh
        s *= 2
    h = a * h_scr[...] + b
    h_scr[...] = h[rows - 1:rows, :]
    hout_ref[0] = h[rows - 1:rows, :]
    y_ref[...] = (h * _gelu_tanh(g_ref[...])).astype(BF16)


def lru_prompt(main, xcol, gcol, nb, tp, pad, cw, cb, wg, ba, bx, clam):
    d = cw.shape[1]
    n_grp = wg.shape[0]
    nt = tp // ROW_TILE
    const2 = lambda b, c: (0, 0)
    return pl.pallas_call(
        functools.partial(_lru_prompt_kernel, pad=pad, n_grp=n_grp),
        grid=(nb, nt),
        in_specs=[pl.BlockSpec((ROW_TILE, d), lambda b, c: (b * nt + c, xcol)),
                  pl.BlockSpec((ROW_TILE, d), lambda b, c: (b * nt + c, gcol)),
                  pl.BlockSpec((CONV_W, d), const2),
                  pl.BlockSpec((1, d), const2),
                  pl.BlockSpec(wg.shape, lambda b, c: (0, 0, 0)),
                  pl.BlockSpec((1, d), const2),
                  pl.BlockSpec((1, d), const2),
                  pl.BlockSpec((1, d), const2)],
        out_specs=[pl.BlockSpec((ROW_TILE, d), lambda b, c: (b * nt + c, 0)),
                   pl.BlockSpec((1, 1, d), lambda b, c: (b, 0, 0))],
        out_shape=[jax.ShapeDtypeStruct((nb * tp, d), BF16),
                   jax.ShapeDtypeStruct((nb, 1, d), F32)],
        scratch_shapes=[pltpu.VMEM((8 + ROW_TILE, d), F32), pltpu.VMEM((1, d), F32)],
        compiler_params=_cparams(("parallel", "arbitrary")),
        name="lru_prompt",
    )(main, main, cw, cb, wg, ba, bx, clam)


DT_LANE = IDX_DIM


def _split3_bf16(x):
    hi = x.astype(BF16)
    r1 = x - hi.astype(F32)
    mid = r1.astype(BF16)
    lo = (r1 - mid.astype(F32)).astype(BF16)
    return hi, mid, lo


def _ssd_prompt_kernel(xbc_ref, z_ref, small_ref, cw_ref, cb_ref, dtb_ref, aneg_ref, dfull_ref, ng_ref,
                       y_ref, hout_ref, ext_scr, ht_scr, y_scr, *, pad, n_heads, n_grp):
    c = pl.program_id(1)
    rows = xbc_ref.shape[0]
    d_ssm = z_ref.shape[1]
    hpg = n_heads // n_grp
    P = SSM_HEAD_DIM
    N = SSM_STATE

    @pl.when(c == 0)
    def _():
        ext_scr[0:8, :] = jnp.zeros((8, ext_scr.shape[1]), F32)
        ht_scr[...] = jnp.zeros(ht_scr.shape, F32)

    xbc = _silu(_causal_conv_rows(ext_scr, xbc_ref[...], cw_ref, cb_ref, rows))
    xs = xbc[:, :d_ssm]
    ridx = lax.broadcasted_iota(I32, (rows, 1), 0)
    dt = _softplus(small_ref[...] + dtb_ref[...])
    dt = jnp.where(c * rows + ridx >= pad, dt, 0.0)
    da = dt * aneg_ref[...]
    ii = lax.broadcasted_iota(I32, (rows, rows), 0)
    jj = lax.broadcasted_iota(I32, (rows, rows), 1)
    causal = jj <= ii
    tri = jnp.where(causal, 1.0, 0.0).astype(BF16)
    cum = sum(jnp.dot(tri, part, preferred_element_type=F32) for part in _split3_bf16(da))
    cum_t = cum.T
    dt_t = dt.T
    cum_last = cum[rows - 1:rows, :]
    dec_end = jnp.exp(cum_last - cum) * dt
    ecum = jnp.exp(cum)
    chunk_dec = jnp.exp(cum_last)
    for g in range(n_grp):
        bg = xbc[:, d_ssm + g * N:d_ssm + (g + 1) * N]
        cg = xbc[:, d_ssm + n_grp * N + g * N:d_ssm + n_grp * N + (g + 1) * N]
        bg16 = bg.astype(BF16)
        cg16 = cg.astype(BF16)
        cb = lax.dot_general(cg16, bg16, (((1,), (1,)), ((), ())), preferred_element_type=F32)
        bgt16 = bg.T.astype(BF16)
        for hh in range(hpg):
            h = g * hpg + hh
            ln = DT_LANE + h
            x_h = xs[:, h * P:(h + 1) * P]
            seg = cum[:, ln:ln + 1] - cum_t[ln:ln + 1, :]
            decay = jnp.where(causal, jnp.exp(jnp.where(causal, seg, 0.0)), 0.0)
            w_intra = (cb * decay * dt_t[ln:ln + 1, :]).astype(BF16)
            y_h = jnp.dot(w_intra, x_h.astype(BF16), preferred_element_type=F32)
            ht = ht_scr[h]
            y_h = y_h + jnp.dot(cg16, ht.astype(BF16), preferred_element_type=F32) * ecum[:, ln:ln + 1]
            xw = (x_h * dec_end[:, ln:ln + 1]).astype(BF16)
            ht_new = chunk_dec[:, ln:ln + 1] * ht + jnp.dot(bgt16, xw, preferred_element_type=F32)
            ht_scr[h] = ht_new
            hout_ref[0, h] = ht_new
            y_scr[:, h * P:(h + 1) * P] = y_h
    y = y_scr[...] + dfull_ref[...] * xs
    y = y * _silu(z_ref[...])
    y_ref[...] = _rmsnorm_rows(y, ng_ref[...]).astype(BF16)


def ssd_prompt(main, xbc_col, z_col, small, nb, tp, pad, cw, cb, dtb, aneg, dfull, ng, n_heads, n_grp):
    d_xbc = cw.shape[1]
    d_ssm = ng.shape[1]
    nt = tp // ROW_TILE
    const2 = lambda b, c: (0, 0)
    return pl.pallas_call(
        functools.partial(_ssd_prompt_kernel, pad=pad, n_heads=n_heads, n_grp=n_grp),
        grid=(nb, nt),
        in_specs=[pl.BlockSpec((ROW_TILE, d_xbc), lambda b, c: (b * nt + c, xbc_col)),
                  pl.BlockSpec((ROW_TILE, d_ssm), lambda b, c: (b * nt + c, z_col)),
                  pl.BlockSpec((ROW_TILE, LANES), lambda b, c: (b * nt + c, 0)),
                  pl.BlockSpec((CONV_W, d_xbc), const2),
                  pl.BlockSpec((1, d_xbc), const2),
                  pl.BlockSpec((1, LANES), const2),
                  pl.BlockSpec((1, LANES), const2),
                  pl.BlockSpec((1, d_ssm), const2),
                  pl.BlockSpec((1, d_ssm), const2)],
        out_specs=[pl.BlockSpec((ROW_TILE, d_ssm), lambda b, c: (b * nt + c, 0)),
                   pl.BlockSpec((1, n_heads, SSM_STATE, SSM_HEAD_DIM), lambda b, c: (b, 0, 0, 0))],
        out_shape=[jax.ShapeDtypeStruct((nb * tp, d_ssm), BF16),
                   jax.ShapeDtypeStruct((nb, n_heads, SSM_STATE, SSM_HEAD_DIM), F32)],
        scratch_shapes=[pltpu.VMEM((8 + ROW_TILE, d_xbc), F32),
                        pltpu.VMEM((n_heads, SSM_STATE, SSM_HEAD_DIM), F32),
                        pltpu.VMEM((ROW_TILE, d_ssm), F32)],
        compiler_params=_cparams(("parallel", "arbitrary")),
        name="ssd_prompt",
    )(main, main, small, cw, cb, dtb, aneg, dfull, ng)


WIDX_LANE = IDX_DIM + 16


def _sortable_key(score):
    score = jnp.where(score == 0.0, 0.0, score)
    bits = pltpu.bitcast(score, I32)
    return jnp.where(bits < 0, bits ^ jnp.int32(0x7FFFFFFF), bits)


def _lane_blocks(x):
    return [x[:, u * LANES:(u + 1) * LANES] for u in range(x.shape[1] // LANES)]


def _row_total(x):
    return jnp.broadcast_to(jnp.sum(x, axis=1, keepdims=True), x.shape)


def _select_threshold(count, like, ktop, col_bits, bits_per_step=1):
    zero = jnp.zeros(like.shape, I32)
    assert 32 % bits_per_step == 0

    def t_step(it, t):
        shift = 32 - bits_per_step * (it + 1)
        passed = zero
        for j in range(1, 2 ** bits_per_step):
            cand = t + lax.shift_left(jnp.int32(j), shift)
            passed = passed + jnp.where(count(lambda keys, idx: keys >= cand) >= ktop, 1, 0)
        return t + lax.shift_left(passed, shift)

    t = lax.fori_loop(0, 32 // bits_per_step, t_step, jnp.full(like.shape, INT_MIN, I32))
    n_ge = count(lambda keys, idx: keys >= t)
    n_gt = count(lambda keys, idx: keys > t)
    live = t != INT_MIN
    excess = live & (n_ge > ktop)
    need = ktop - n_gt

    def j_search():
        def j_step(it, j):
            cand = j + lax.shift_left(jnp.int32(1), col_bits - 1 - it)
            n_tie = count(lambda keys, idx: (keys == t) & (idx < cand))
            return jnp.where(n_tie <= need, cand, j)
        return lax.fori_loop(0, col_bits, j_step, zero)

    any_excess = jnp.max(jnp.where(excess, 1, 0)) > 0
    j_cut = lax.cond(any_excess, j_search, lambda: zero)
    big = jnp.int32(2 ** 30)
    j_cut = jnp.where(excess, j_cut, jnp.where(live, big, 0))
    return t, j_cut


def _selected(keys, idx, t, j_cut):
    return (keys > t) | ((keys == t) & (idx < j_cut))


def _index_scores(s, w):
    rows, n_h = w.shape
    s = jnp.maximum(s, 0.0)
    acc = w[:, 0:1] * s[0:rows]
    for h in range(1, n_h):
        acc = acc + w[:, h:h + 1] * s[h * rows:(h + 1) * rows]
    return acc


def _dsa_prompt_kernel(q_ref, qi_ref, small_ref, kb_ref, vbt_ref, kidx_ref, o_ref,
                       key_scr, qaug_scr, m_scr, acc_scr, *, ktop, pad, tkc, col_bits):
    i = pl.program_id(1)
    n_q, tq, hd = q_ref.shape
    n_idx = qi_ref.shape[0]
    n_kv = kb_ref.shape[0]
    qpk = n_q // n_kv
    r4 = qpk * tq
    q0 = i * tq
    n_chunks = (q0 + tq + tkc - 1) // tkc
    nt_dims = (((1,), (1,)), ((), ()))

    wt = small_ref[...].T[WIDX_LANE:WIDX_LANE + n_idx, :] * ((n_idx * IDX_DIM) ** -0.5)
    qi_all = qi_ref[...].reshape(n_idx * tq, qi_ref.shape[2])
    qcol = q0 + lax.broadcasted_iota(I32, (tkc, tq), 1)
    krow = lax.broadcasted_iota(I32, (tkc, tq), 0)

    def score_chunk(c, carry):
        k0 = pl.multiple_of(c * tkc, tkc)
        s = lax.dot_general(kidx_ref[pl.ds(k0, tkc), :], qi_all, nt_dims, preferred_element_type=F32)
        acc = wt[0:1] * jnp.maximum(s[:, 0:tq], 0.0)
        for h in range(1, n_idx):
            acc = acc + wt[h:h + 1] * jnp.maximum(s[:, h * tq:(h + 1) * tq], 0.0)
        kidx = krow + k0
        admissible = (kidx >= pad) & (kidx <= qcol)
        key_scr[c] = jnp.where(admissible, _sortable_key(acc), INT_MIN)
        return carry

    lax.fori_loop(0, n_chunks, score_chunk, 0)

    fold = 8 * 8

    def count(pred):
        def body(c, cnt):
            hit = jnp.where(pred(key_scr[c], krow + c * tkc), 1, 0)
            return cnt + jnp.sum(hit.reshape(tkc // fold, fold, tq), axis=0)
        cnt = lax.fori_loop(0, n_chunks, body, jnp.zeros((fold, tq), I32))
        return jnp.sum(cnt, axis=0, keepdims=True)

    t, j_cut = _select_threshold(count, jnp.zeros((1, tq), I32), ktop, col_bits)

    rr = lax.broadcasted_iota(I32, (tq, tq), 0)
    cc = lax.broadcasted_iota(I32, (tq, tq), 1)
    onehot = jnp.where(rr == cc, 1.0, 0.0).astype(BF16)
    zeros_tail = jnp.zeros((tq, qaug_scr.shape[1] - tq - hd), BF16)
    for h in range(n_q):
        qaug_scr[h * tq:(h + 1) * tq, :] = jnp.concatenate([onehot, q_ref[h], zeros_tail], axis=1)
    m_scr[...] = jnp.full(m_scr.shape, NEG_BIG, F32)
    acc_scr[...] = jnp.zeros(acc_scr.shape, F32)

    def mask_bias(c):
        idx = krow + pl.multiple_of(c * tkc, tkc)
        return jnp.where(_selected(key_scr[c], idx, t, j_cut), 0.0, NEG_BIG).astype(BF16)

    def qk(g, c, bias_t):
        k0 = pl.multiple_of(c * tkc, tkc)
        lhs = jnp.concatenate([bias_t, kb_ref[g, pl.ds(k0, tkc), :]], axis=1)
        return lax.dot_general(lhs, qaug_scr[g * r4:(g + 1) * r4, :], nt_dims,
                               preferred_element_type=F32)

    def attend(c, carry):
        k0 = pl.multiple_of(c * tkc, tkc)
        bias_t = mask_bias(c)

        def softmax(g, s):
            m_prev = m_scr[g * 8:g * 8 + 1, :]
            m_new = jnp.maximum(m_prev, jnp.max(s, axis=0, keepdims=True))
            m_scr[g * 8:g * 8 + 1, :] = m_new
            return jnp.exp2(s - m_new).astype(BF16), jnp.exp2(m_prev - m_new)

        def pv(g, p, alpha):
            gs = slice(g * LANES, (g + 1) * LANES)
            acc_scr[gs, :] = alpha * acc_scr[gs, :] + jnp.dot(vbt_ref[g, :, pl.ds(k0, tkc)], p,
                                                               preferred_element_type=F32)

        s_q = {0: qk(0, c, bias_t)}
        p_q = {}
        for step in range(n_kv + 1):
            if step + 1 < n_kv:
                s_q[step + 1] = qk(step + 1, c, bias_t)
            if step < n_kv:
                p_q[step] = softmax(step, s_q.pop(step))
            if 1 <= step <= n_kv:
                pv(step - 1, *p_q.pop(step - 1))
        return carry

    lax.fori_loop(0, n_chunks, attend, 0)
    for pair in range(n_q // 2):
        g, hh = (2 * pair) // qpk, (2 * pair) % qpk
        cols = slice(hh * tq, (hh + 2) * tq)
        denom = acc_scr[g * LANES + hd:g * LANES + hd + 1, cols]
        denom = jnp.where(denom > 0.0, denom, 1.0)
        out_t = acc_scr[g * LANES:g * LANES + hd, cols] / denom
        both = jnp.concatenate([out_t[:, :tq], out_t[:, tq:]], axis=0).T
        o_ref[:, 2 * pair * hd:(2 * pair + 2) * hd] = both.astype(BF16)


def dsa_prompt(q, qi, small, kb, vbt, kidxb, nb, tp, pad, ktop, tkc):
    n_q, _, hd = q.shape
    n_idx = qi.shape[0]
    n_kv = kb.shape[0]
    tq = ROW_TILE
    nt = tp // tq
    r4 = (n_q // n_kv) * tq
    col_bits = max(1, int(math.ceil(math.log2(tp))))
    return pl.pallas_call(
        functools.partial(_dsa_prompt_kernel, ktop=ktop, pad=pad, tkc=tkc, col_bits=col_bits),
        grid=(nb, nt),
        in_specs=[pl.BlockSpec((n_q, tq, hd), lambda b, i: (0, b * nt + i, 0)),
                  pl.BlockSpec((n_idx, tq, IDX_DIM), lambda b, i: (0, b * nt + i, 0)),
                  pl.BlockSpec((tq, LANES), lambda b, i: (b * nt + i, 0)),
                  pl.BlockSpec((n_kv, tp, LANES), lambda b, i: (0, b, 0)),
                  pl.BlockSpec((n_kv, LANES, tp), lambda b, i: (0, 0, b)),
                  pl.BlockSpec((tp, IDX_DIM), lambda b, i: (b, 0))],
        out_specs=pl.BlockSpec((tq, n_q * hd), lambda b, i: (b * nt + i, 0)),
        out_shape=jax.ShapeDtypeStruct((nb * tp, n_q * hd), BF16),
        scratch_shapes=[pltpu.VMEM((tp // tkc, tkc, tq), I32),
                        pltpu.VMEM((n_q * tq, 2 * LANES), BF16),
                        pltpu.VMEM((n_kv * 8, r4), F32),
                        pltpu.VMEM((n_kv * LANES, r4), F32)],
        compiler_params=_cparams(("parallel", "arbitrary")),
        name="dsa_prompt",
    )(q, qi, small, kb, vbt, kidxb)


def _lru_sample_kernel(x_ref, g_ref, hist_ref, h0_ref, cw_ref, cb_ref, wg_ref, ba_ref, bx_ref, clam_ref,
                       y_ref, hout_ref, convout_ref, hist_scr, h_scr, *, n_grp):
    t = pl.program_id(0)

    @pl.when(t == 0)
    def _():
        hist_scr[...] = hist_ref[...]
        h_scr[...] = h0_ref[...]

    x = x_ref[...]
    u = cb_ref[...] + cw_ref[CONV_W - 1:CONV_W, :] * x
    for j in range(CONV_W - 1):
        u = u + cw_ref[j:j + 1, :] * hist_scr[j]
    for j in range(CONV_W - 2):
        hist_scr[j] = hist_scr[j + 1]
    hist_scr[CONV_W - 2] = x
    a, b = _lru_gates(u, wg_ref, ba_ref, bx_ref, clam_ref, n_grp)
    h = a * h_scr[...] + b
    h_scr[...] = h
    y_ref[...] = (h * _gelu_tanh(g_ref[...])).astype(BF16)
    hout_ref[...] = h
    convout_ref[...] = hist_scr[...]


def lru_sample(main, xcol, gcol, row0, db, ds, hist, h0, cw, cb, wg, ba, bx, clam):
    d = cw.shape[1]
    n_grp = wg.shape[0]
    blk0 = row0 // db
    const2 = lambda t: (0, 0)
    const3 = lambda t: (0, 0, 0)
    return pl.pallas_call(
        functools.partial(_lru_sample_kernel, n_grp=n_grp),
        grid=(ds,),
        in_specs=[pl.BlockSpec((db, d), lambda t: (blk0 + t, xcol)),
                  pl.BlockSpec((db, d), lambda t: (blk0 + t, gcol)),
                  pl.BlockSpec((CONV_W - 1, db, d), const3),
                  pl.BlockSpec((db, d), const2),
                  pl.BlockSpec((CONV_W, d), const2),
                  pl.BlockSpec((1, d), const2),
                  pl.BlockSpec(wg.shape, const3),
                  pl.BlockSpec((1, d), const2),
                  pl.BlockSpec((1, d), const2),
                  pl.BlockSpec((1, d), const2)],
        out_specs=[pl.BlockSpec((db, d), lambda t: (t, 0)),
                   pl.BlockSpec((db, d), const2),
                   pl.BlockSpec((CONV_W - 1, db, d), const3)],
        out_shape=[jax.ShapeDtypeStruct((ds * db, d), BF16),
                   jax.ShapeDtypeStruct((db, d), F32),
                   jax.ShapeDtypeStruct((CONV_W - 1, db, d), F32)],
        scratch_shapes=[pltpu.VMEM((CONV_W - 1, db, d), F32), pltpu.VMEM((db, d), F32)],
        compiler_params=_cparams(("arbitrary",)),
        name="lru_sample",
    )(main, main, hist, h0, cw, cb, wg, ba, bx, clam)


def _ssd_sample_kernel(*refs, ds, n_heads, n_grp):
    xbc_refs = refs[0:ds]
    z_refs = refs[ds:2 * ds]
    small_refs = refs[2 * ds:3 * ds]
    (hist_ref, h0_ref, cw_ref, cb_ref, dtb_ref, aneg_ref, dfull_ref, ng_ref, exp_ref,
     y_ref, hout_ref, convout_ref, yint_scr) = refs[3 * ds:]
    bb = xbc_refs[0].shape[0]
    d_ssm = z_refs[0].shape[1]
    hpg = n_heads // n_grp
    P = SSM_HEAD_DIM
    N = SSM_STATE
    gw = hpg * P

    ext = [hist_ref[j] for j in range(CONV_W - 1)] + [r[...] for r in xbc_refs]
    for j in range(CONV_W - 1):
        convout_ref[j] = ext[ds + j]
    xbc = []
    for t in range(ds):
        u = cb_ref[...]
        for j in range(CONV_W):
            u = u + cw_ref[j:j + 1, :] * ext[t + j]
        xbc.append(_silu(u))
    xs = [v[:, :d_ssm] for v in xbc]
    bm = [v[:, d_ssm:d_ssm + n_grp * N] for v in xbc]
    cm = [v[:, d_ssm + n_grp * N:] for v in xbc]
    dt = [_softplus(r[...] + dtb_ref[...]) for r in small_refs]
    cum = []
    run = jnp.zeros_like(dt[0])
    for t in range(ds):
        run = run + dt[t] * aneg_ref[...]
        cum.append(run)
    lane = lax.broadcasted_iota(I32, (bb, LANES), 1)
    coefs = []
    for t in range(ds):
        for j in range(t + 1):
            cbh = jnp.zeros((bb, LANES), F32)
            for g in range(n_grp):
                dotg = jnp.sum(cm[t][:, g * N:(g + 1) * N] * bm[j][:, g * N:(g + 1) * N], axis=1, keepdims=True)
                in_g = (lane >= DT_LANE + g * hpg) & (lane < DT_LANE + (g + 1) * hpg)
                cbh = cbh + jnp.where(in_g, dotg, 0.0)
            coefs.append(cbh * jnp.exp(cum[t] - cum[j]) * dt[j])
    n_intra = len(coefs)
    coefs += [jnp.exp(c) for c in cum]
    coefs += [jnp.exp(cum[ds - 1] - cum[j]) * dt[j] for j in range(ds)]
    coefs.append(jnp.exp(cum[ds - 1]))
    stack = jnp.concatenate(coefs, axis=0)
    wide = sum(jnp.dot(part, exp_ref[...], preferred_element_type=F32) for part in _split3_bf16(stack))
    wide = [wide[k * bb:(k + 1) * bb] for k in range(len(coefs))]
    intra_w = wide[:n_intra]
    ecum_w = wide[n_intra:n_intra + ds]
    dend_w = wide[n_intra + ds:n_intra + 2 * ds]
    cdec_w = wide[n_intra + 2 * ds]
    xw = [dend_w[j] * xs[j] for j in range(ds)]
    zpad = 8 - ds
    for b in range(bb):
        for g in range(n_grp):
            def rows_of(arrs, lo, width):
                parts = [a[b:b + 1, lo:lo + width] for a in arrs]
                if zpad:
                    parts.append(jnp.zeros((zpad, width), F32))
                return jnp.concatenate(parts, axis=0).astype(BF16)
            hg = h0_ref[b, g * hpg:(g + 1) * hpg].reshape(gw, N)
            yint = lax.dot_general(rows_of(cm, g * N, N), hg.astype(BF16), (((1,), (1,)), ((), ())),
                                   preferred_element_type=F32)
            for t in range(ds):
                yint_scr[t, b:b + 1, g * gw:(g + 1) * gw] = yint[t:t + 1]
            upd = lax.dot_general(rows_of(xw, g * gw, gw), rows_of(bm, g * N, N), (((0,), (0,)), ((), ())),
                                  preferred_element_type=F32)
            for hh in range(hpg):
                h = g * hpg + hh
                cd = cdec_w[b:b + 1, h * P:h * P + 1]
                hout_ref[b, h] = cd * hg[hh * P:(hh + 1) * P] + upd[hh * P:(hh + 1) * P]
    k = 0
    for t in range(ds):
        y = ecum_w[t] * yint_scr[t] + dfull_ref[...] * xs[t]
        for j in range(t + 1):
            y = y + intra_w[k] * xs[j]
            k += 1
        y = y * _silu(z_refs[t][...])
        y_ref[t] = _rmsnorm_rows(y, ng_ref[...]).astype(BF16)


def ssd_sample(main, xbc_col, z_col, small, row0, db, ds, bb, hist, h0, cw, cb, dtb, aneg, dfull, ng, expand,
               n_heads, n_grp):
    d_xbc = cw.shape[1]
    d_ssm = ng.shape[1]
    const2 = lambda i: (0, 0)

    def at_t(t, width, col):
        blk0 = (row0 + t * db) // bb
        return pl.BlockSpec((bb, width), lambda i: (blk0 + i, col))

    in_specs = ([at_t(t, d_xbc, xbc_col) for t in range(ds)] + [at_t(t, d_ssm, z_col) for t in range(ds)]
                + [at_t(t, LANES, 0) for t in range(ds)]
                + [pl.BlockSpec((CONV_W - 1, bb, d_xbc), lambda i: (0, i, 0)),
                   pl.BlockSpec((bb, n_heads, SSM_HEAD_DIM, SSM_STATE), lambda i: (i, 0, 0, 0)),
                   pl.BlockSpec((CONV_W, d_xbc), const2),
                   pl.BlockSpec((1, d_xbc), const2),
                   pl.BlockSpec((1, LANES), const2),
                   pl.BlockSpec((1, LANES), const2),
                   pl.BlockSpec((1, d_ssm), const2),
                   pl.BlockSpec((1, d_ssm), const2),
                   pl.BlockSpec((LANES, d_ssm), const2)])
    return pl.pallas_call(
        functools.partial(_ssd_sample_kernel, ds=ds, n_heads=n_heads, n_grp=n_grp),
        grid=(db // bb,),
        in_specs=in_specs,
        out_specs=[pl.BlockSpec((ds, bb, d_ssm), lambda i: (0, i, 0)),
                   pl.BlockSpec((bb, n_heads, SSM_HEAD_DIM, SSM_STATE), lambda i: (i, 0, 0, 0)),
                   pl.BlockSpec((CONV_W - 1, bb, d_xbc), lambda i: (0, i, 0))],
        out_shape=[jax.ShapeDtypeStruct((ds, db, d_ssm), BF16),
                   jax.ShapeDtypeStruct((db, n_heads, SSM_HEAD_DIM, SSM_STATE), F32),
                   jax.ShapeDtypeStruct((CONV_W - 1, db, d_xbc), F32)],
        scratch_shapes=[pltpu.VMEM((ds, bb, d_ssm), F32)],
        compiler_params=_cparams(("parallel",)),
        name="ssd_sample",
    )(*([main] * ds + [main] * ds + [small] * ds + [hist, h0, cw, cb, dtb, aneg, dfull, ng, expand]))


def _dsa_sample_kernel(pt_ref, *refs, n_pages, ds, ktop):
    del pt_ref
    q_ref, qi_ref, w_ref, knew_ref, vnew_ref, kxnew_ref = refs[0:6]
    kx_pages = refs[6:6 + n_pages]
    k_pages = refs[6 + n_pages:6 + 2 * n_pages]
    v_pages = refs[6 + 2 * n_pages:6 + 3 * n_pages]
    o_ref, key_scr, logit_scr, new_scr = refs[6 + 3 * n_pages:]
    n_rows, hd = q_ref.shape[1:]
    n_q = n_rows // ds
    n_idx, rq = qi_ref.shape[1:3]
    kw = knew_ref.shape[2]
    n_kv = kw // hd
    qpk = n_q // n_kv
    n_chunks = n_pages + 1

    w = w_ref[0] * ((n_idx * IDX_DIM) ** -0.5)
    qi = qi_ref[0].reshape(n_idx * rq, qi_ref.shape[3]).astype(BF16)
    qrow = lax.broadcasted_iota(I32, (rq, PAGE_SIZE), 0)
    lane = lax.broadcasted_iota(I32, (rq, PAGE_SIZE), 1)
    nt_dims = (((1,), (1,)), ((), ()))
    for p in range(n_pages):
        sc = _index_scores(jnp.dot(qi, kx_pages[p][0].astype(BF16), preferred_element_type=F32), w)
        key_scr[0, :, p * PAGE_SIZE:(p + 1) * PAGE_SIZE] = jnp.where(qrow < ds, _sortable_key(sc), INT_MIN)
    new_scr[...] = jnp.zeros(new_scr.shape, F32)
    new_scr[0:rq, 0:kxnew_ref.shape[2]] = kxnew_ref[0]
    sc = _index_scores(lax.dot_general(qi, new_scr[:, 0:kxnew_ref.shape[2]].astype(BF16), nt_dims,
                                       preferred_element_type=F32), w)
    admissible = (lane <= qrow) & (qrow < ds)
    key_scr[0, :, n_pages * PAGE_SIZE:] = jnp.where(admissible, _sortable_key(sc), INT_MIN)
    col_bits = max(1, int(math.ceil(math.log2(n_chunks * PAGE_SIZE))))

    def count(pred):
        cnt = jnp.zeros((rq, PAGE_SIZE), I32)
        for p in range(n_chunks):
            kk = key_scr[0, :, p * PAGE_SIZE:(p + 1) * PAGE_SIZE]
            cnt = cnt + jnp.where(pred(kk, lane + p * PAGE_SIZE), 1, 0)
        return _row_total(cnt)

    t, j_cut = _select_threshold(count, lane, ktop, col_bits, bits_per_step=4)

    assert n_q & (n_q - 1) == 0 and qpk & (qpk - 1) == 0 and hd & (hd - 1) == 0
    row_group = (lax.broadcasted_iota(I32, (n_rows, kw), 0) & (n_q - 1)) >> (qpk.bit_length() - 1)
    own_lanes = (lax.broadcasted_iota(I32, (n_rows, kw), 1) >> (hd.bit_length() - 1)) == row_group
    qexp = jnp.where(own_lanes, jnp.concatenate([q_ref[0]] * n_kv, axis=1), 0.0).astype(BF16)
    new_scr[0:rq, :] = knew_ref[0]
    knew = new_scr[...].astype(BF16)
    new_scr[0:rq, :] = vnew_ref[0]
    vnew = new_scr[...].astype(BF16)
    for p in range(n_chunks):
        kk = key_scr[0, :, p * PAGE_SIZE:(p + 1) * PAGE_SIZE]
        bias = jnp.where(_selected(kk, lane + p * PAGE_SIZE, t, j_cut), 0.0, NEG_BIG)
        bias_rows = jnp.concatenate([jnp.broadcast_to(bias[ts:ts + 1], (n_q, PAGE_SIZE)) for ts in range(ds)], axis=0)
        if p < n_pages:
            qk = jnp.dot(qexp, k_pages[p][0].astype(BF16), preferred_element_type=F32)
        else:
            qk = lax.dot_general(qexp, knew, nt_dims, preferred_element_type=F32)
        logit_scr[:, p * PAGE_SIZE:(p + 1) * PAGE_SIZE] = bias_rows + qk
    logits = logit_scr[...]
    prob = jnp.exp2(logits - jnp.max(logits, axis=1, keepdims=True))
    denom = jnp.sum(prob, axis=1, keepdims=True)
    prob = prob.astype(BF16)
    acc = jnp.zeros((n_rows, kw), F32)
    for p in range(n_chunks):
        pp = prob[:, p * PAGE_SIZE:(p + 1) * PAGE_SIZE]
        if p < n_pages:
            acc = acc + lax.dot_general(pp, v_pages[p][0].astype(BF16), nt_dims, preferred_element_type=F32)
        else:
            acc = acc + jnp.dot(pp, vnew, preferred_element_type=F32)
    out = jnp.zeros((n_rows, hd), F32)
    for g in range(n_kv):
        out = out + jnp.where(row_group[:, :hd] == g, acc[:, g * hd:(g + 1) * hd], 0.0)
    o_ref[0] = out / denom


def dsa_sample(page_table, q, qi, w, knew, vnew, kxnew, cache_k, cache_v, cache_kidx, layer, n_pool, ds, ktop):
    db, n_rows, hd = q.shape
    n_idx, rq = qi.shape[1:3]
    kw = knew.shape[2]
    n_pages = page_table.shape[1]

    def page_spec(width, p):
        return pl.BlockSpec((1, width, PAGE_SIZE), lambda b, pt: (layer * n_pool + pt[b, p], 0, 0))

    own3 = lambda b, pt: (b, 0, 0)
    own4 = lambda b, pt: (b, 0, 0, 0)
    in_specs = ([pl.BlockSpec((1, n_rows, hd), own3),
                 pl.BlockSpec((1, n_idx, rq, IDX_DIM), own4),
                 pl.BlockSpec((1, rq, n_idx), own3),
                 pl.BlockSpec((1, rq, kw), own3),
                 pl.BlockSpec((1, rq, kw), own3),
                 pl.BlockSpec((1, rq, IDX_DIM), own3)]
                + [page_spec(IDX_DIM, p) for p in range(n_pages)]
                + [page_spec(kw, p) for p in range(n_pages)]
                + [page_spec(kw, p) for p in range(n_pages)])
    grid_spec = pltpu.PrefetchScalarGridSpec(
        num_scalar_prefetch=1,
        grid=(db,),
        in_specs=in_specs,
        out_specs=pl.BlockSpec((1, n_rows, hd), own3),
        scratch_shapes=[pltpu.VMEM((1, rq, (n_pages + 1) * PAGE_SIZE), I32),
                        pltpu.VMEM((n_rows, (n_pages + 1) * PAGE_SIZE), F32),
                        pltpu.VMEM((PAGE_SIZE, kw), F32)])
    return pl.pallas_call(
        functools.partial(_dsa_sample_kernel, n_pages=n_pages, ds=ds, ktop=ktop),
        grid_spec=grid_spec,
        out_shape=jax.ShapeDtypeStruct((db, n_rows, hd), F32),
        compiler_params=_cparams(("arbitrary",)),
        name="dsa_sample",
    )(page_table, q, qi, w, knew, vnew, kxnew,
      *([cache_kidx] * n_pages + [cache_k] * n_pages + [cache_v] * n_pages))


def _round_up(x, m):
    return (x + m - 1) // m * m


def _largest_tile(total, unit, cap):
    best = unit
    for k in range(1, cap // unit + 1):
        if total % (k * unit) == 0:
            best = k * unit
    return best


def _block_diag_groups(w, per_group):
    nb, r, _ = w.shape
    ng = nb // per_group
    w = w.reshape(ng, per_group, r, r)
    eye = jnp.eye(per_group, dtype=w.dtype)
    return jnp.einsum('gbij,bc->gbicj', w, eye).reshape(ng, per_group * r, per_group * r)


def _pad_rows_to(a, axis, n):
    pad = [(0, 0)] * a.ndim
    pad[axis] = (0, n - a.shape[axis])
    return jnp.pad(a, pad)


def kernel(x_prompt, x_sample, cache_k, cache_v, cache_kidx, state_ssm, state_ssm_conv, state_lru_h, state_lru_conv, page_table, meta_tokens, norm1, w_in, lru_conv_w, lru_conv_b, lru_wa, lru_ba, lru_wx, lru_bx, lru_lambda, ssm_conv_w, ssm_conv_b, ssm_dt_bias, ssm_a_log, ssm_d, ssm_norm, p_lru, p_ssm, p_attn, w_o, norm2, w_up, w_down, final_norm):
    nb, seq, d = x_prompt.shape
    db, ds, _ = x_sample.shape
    depth = w_in.shape[0]
    n_pages = page_table.shape[1]
    past = n_pages * PAGE_SIZE
    n_pool = cache_k.shape[1]
    n_kv = cache_k.shape[3]
    kw = n_kv * HEAD_DIM
    d_rnn = lru_conv_w.shape[2]
    d_xbc = ssm_conv_w.shape[2]
    d_ssm = ssm_norm.shape[1]
    n_ssm_heads = ssm_d.shape[1]
    n_heads = p_attn.shape[1] // HEAD_DIM
    n_idx = N_IDX_HEADS
    d_ff = w_up.shape[2]
    assert d_rnn == d and d_ssm == d and d_xbc == 2 * d and n_heads * HEAD_DIM == d
    assert seq % DSA_KEY_UNIT == 0 and ds <= 8 and n_ssm_heads <= LANES - DT_LANE

    pad = (-N_META) % DSA_KEY_UNIT
    tp = pad + N_META + seq
    mp = nb * tp
    ms = db * ds
    mt = _round_up(mp + ms, 2 * DENSE_TM)
    bb = 8
    assert mp % db == 0 and db % bb == 0
    ktop_p = min(TOP_K_MAX, seq // TOP_K_FRACTION)
    ktop_s = min(TOP_K_MAX, (past + ds) // TOP_K_FRACTION)
    tkc = _largest_tile(tp, DSA_KEY_UNIT, DSA_KEY_CHUNK_MAX)

    xp = jnp.concatenate([jnp.zeros((nb, pad, d), F32),
                          jnp.broadcast_to(meta_tokens[None].astype(F32), (nb, N_META, d)),
                          x_prompt], axis=1).reshape(mp, d)
    xs = jnp.swapaxes(x_sample, 0, 1).reshape(ms, d)
    x = jnp.concatenate([xp, xs, jnp.zeros((mt - mp - ms, d), F32)], axis=0)
    r = np.arange(mt)
    in_prompt = r < mp
    rp = r % tp
    valid_np = np.where(in_prompt, rp >= pad, r < mp + ms)
    pos_np = np.where(in_prompt, np.maximum(rp - pad, 0), np.where(r < mp + ms, past + (r - mp) // db, 0))
    valid = jnp.asarray(valid_np.astype(np.float32)).reshape(mt, 1)
    half = HEAD_DIM // 2
    freq = ROPE_THETA ** (-jnp.arange(half, dtype=F32) / half)
    ang = jnp.asarray(pos_np).astype(F32)[:, None] * freq[None, :]
    cos = jnp.cos(ang)
    sin = jnp.sin(ang)
    cos_t = jnp.concatenate([cos] * (LANES // half), axis=1)
    sin_t = jnp.concatenate([-sin, sin] * (LANES // HEAD_DIM), axis=1)

    splits = (d_rnn, d_rnn, d_ssm, d_xbc, n_ssm_heads, n_heads * HEAD_DIM, kw, kw, n_idx * IDX_DIM, IDX_DIM,
              n_idx, N_BRANCH * d)
    assert sum(splits) == w_in.shape[2]
    off = np.concatenate([[0], np.cumsum(splits)])
    seg = lambda w, i: w[:, off[i]:off[i + 1]]
    LRU_X, LRU_G, SSM_Z, SSM_XBC, SSM_DT, Q, K, V, QI, KIDX, WIDX, GATES = range(12)
    XBC_COL, X_COL, G_COL, Z_COL, GATE_COL = 0, 2, 3, 4, 5
    lane_pos = np.arange(LANES)
    head_lanes = (lane_pos >= DT_LANE) & (lane_pos < DT_LANE + n_ssm_heads)
    expand_np = np.zeros((LANES, d_ssm), np.float32)
    for h in range(n_ssm_heads):
        expand_np[DT_LANE + h, h * SSM_HEAD_DIM:(h + 1) * SSM_HEAD_DIM] = 1.0
    expand = jnp.asarray(expand_np).astype(BF16)

    def to_head_lanes(v):
        return jnp.zeros((1, LANES), F32).at[0, DT_LANE:DT_LANE + n_ssm_heads].set(v)

    def pages_t(cache):
        c = jnp.moveaxis(cache, 2, -1)
        return c.reshape(depth * n_pool, -1, PAGE_SIZE)

    outs_p = [[] for _ in range(7)]
    outs_s = [[] for _ in range(7)]
    for l in range(depth):
        w = w_in[l]
        w_main = jnp.concatenate([seg(w, SSM_XBC), seg(w, LRU_X), seg(w, LRU_G), seg(w, SSM_Z), seg(w, GATES)],
                                 axis=1).astype(BF16)
        tail = LANES - IDX_DIM - n_ssm_heads - n_idx
        w_attn = jnp.concatenate([seg(w, Q), seg(w, QI), seg(w, K), seg(w, V), seg(w, KIDX), seg(w, SSM_DT),
                                  seg(w, WIDX), jnp.zeros((d, tail), F32)], axis=1).astype(BF16)
        per_group = 256 // (d_rnn // N_RNN_BLOCKS)
        wg = jnp.concatenate([_block_diag_groups(lru_wa[l], per_group), _block_diag_groups(lru_wx[l], per_group)],
                             axis=2).astype(BF16)
        clam = (-LRU_C * jax.nn.softplus(-lru_lambda[l])).reshape(1, d_rnn)
        lru_args = (lru_conv_w[l], lru_conv_b[l].reshape(1, -1), wg, lru_ba[l].reshape(1, -1),
                    lru_bx[l].reshape(1, -1), clam)
        dtb = to_head_lanes(ssm_dt_bias[l])
        aneg = to_head_lanes(-jnp.exp(ssm_a_log[l]))
        dfull = jnp.repeat(ssm_d[l], SSM_HEAD_DIM).reshape(1, d_ssm)
        ssm_args = (ssm_conv_w[l], ssm_conv_b[l].reshape(1, -1), dtb, aneg, dfull, ssm_norm[l].reshape(1, -1))

        main = norm_matmul(x, norm1[l].reshape(1, d), w_main, 2 * DENSE_TM, 2048)
        q, qi, kf, vf, small, kb, vbt, kidxb = attn_proj(x, norm1[l].reshape(1, d), w_attn, cos_t, sin_t,
                                                        n_heads, n_idx, n_kv, DENSE_TM)
        ya_p, hlru_p = lru_prompt(main, X_COL, G_COL, nb, tp, pad, *lru_args)
        yb_p, hssm_p = ssd_prompt(main, XBC_COL, Z_COL, small, nb, tp, pad, *ssm_args, n_ssm_heads, N_SSM_GROUPS)
        yc_p = dsa_prompt(q, qi, small, kb, vbt, kidxb, nb, tp, pad, ktop_p, tkc)
        ya_s, hlru_s, conv_lru_s = lru_sample(main, X_COL, G_COL, mp, db, ds,
                                              jnp.swapaxes(state_lru_conv[l], 0, 1), state_lru_h[l], *lru_args)
        yb_s, hssm_s, conv_ssm_s = ssd_sample(main, XBC_COL, Z_COL, small, mp, db, ds, bb,
                                              jnp.swapaxes(state_ssm_conv[l], 0, 1), state_ssm[l], *ssm_args,
                                              expand, n_ssm_heads, N_SSM_GROUPS)

        def batch_major(a):
            lead = a.shape[:-2]
            a = a.astype(F32).reshape(lead + (ds, db, a.shape[-1]))
            a = jnp.moveaxis(a, len(lead) + 1, 0)
            return _pad_rows_to(a, a.ndim - 2, 8)

        small_s = batch_major(small[mp:mp + ms])
        q_s = jnp.transpose(q[:, mp:mp + ms].astype(F32).reshape(n_heads, ds, db, HEAD_DIM), (2, 1, 0, 3))
        yc_s = dsa_sample(page_table, q_s.reshape(db, ds * n_heads, HEAD_DIM), batch_major(qi[:, mp:mp + ms]),
                          small_s[:, :, WIDX_LANE:WIDX_LANE + n_idx], batch_major(kf[mp:mp + ms]),
                          batch_major(vf[mp:mp + ms]), small_s[:, :, :IDX_DIM],
                          pages_t(cache_k), pages_t(cache_v), pages_t(cache_kidx), l, n_pool, ds, ktop_s)
        yc_s = jnp.swapaxes(yc_s.reshape(db, ds, n_heads * HEAD_DIM), 0, 1).reshape(ms, n_heads * HEAD_DIM).astype(BF16)

        tail_rows = jnp.zeros((mt - mp - ms, d), BF16)
        ya = jnp.concatenate([ya_p, ya_s, tail_rows], axis=0)
        yb = jnp.concatenate([yb_p, yb_s.reshape(ms, d_ssm), tail_rows], axis=0)
        yc = jnp.concatenate([yc_p, yc_s, tail_rows], axis=0)
        x = merge(ya, yb, yc, main, GATE_COL, x, valid, p_lru[l].astype(BF16), p_ssm[l].astype(BF16),
                  p_attn[l].astype(BF16), w_o[l].astype(BF16), DENSE_TM)
        x = mlp(x, norm2[l].reshape(1, d), w_up[l].astype(BF16), w_down[l].astype(BF16), 2 * DENSE_TM, 1024)

        prm = lambda a: a[:mp].reshape((nb, tp) + a.shape[1:])
        smp = lambda a: jnp.swapaxes(a[mp:mp + ms].reshape((ds, db) + a.shape[1:]), 0, 1)
        last = CONV_W - 1
        conv_tail = lambda c0, width: jnp.stack([main[(b + 1) * tp - last:(b + 1) * tp, c0:c0 + width]
                                                 for b in range(nb)])
        rows_p = (prm(kf)[:, pad:].reshape(nb, tp - pad, n_kv, HEAD_DIM),
                  prm(vf)[:, pad:].reshape(nb, tp - pad, n_kv, HEAD_DIM),
                  prm(small)[:, pad:, :IDX_DIM],
                  jnp.swapaxes(hssm_p, 2, 3),
                  conv_tail(XBC_COL * d, d_xbc),
                  hlru_p[:, 0],
                  conv_tail(X_COL * d, d_rnn))
        rows_s = (smp(kf).reshape(db, ds, n_kv, HEAD_DIM),
                  smp(vf).reshape(db, ds, n_kv, HEAD_DIM),
                  smp(small)[:, :, :IDX_DIM],
                  hssm_s,
                  jnp.swapaxes(conv_ssm_s, 0, 1),
                  hlru_s,
                  jnp.swapaxes(conv_lru_s, 0, 1))
        for acc, val in zip(outs_p, rows_p):
            acc.append(val)
        for acc, val in zip(outs_s, rows_s):
            acc.append(val)

    y = final_rmsnorm(x, final_norm.reshape(1, d), DENSE_TM)
    y_prompt = y[:mp].reshape(nb, tp, d)[:, pad + N_META:]
    y_sample = jnp.swapaxes(y[mp:mp + ms].reshape(ds, db, d), 0, 1)
    return (y_prompt, y_sample) + tuple(jnp.stack(a) for a in outs_p) + tuple(jnp.stack(a) for a in outs_s)
```

```python
import functools
import math

import numpy as np
import jax
import jax.numpy as jnp
from jax import lax
from jax.experimental import pallas as pl
from jax.experimental.pallas import tpu as pltpu

F32 = jnp.float32
BF16 = jnp.bfloat16
I32 = jnp.int32

N_META = 16
CONV_W = 4
EPS = 1e-6
LRU_C = 8.0
N_RNN_BLOCKS = 16
SSM_HEAD_DIM = 64
N_SSM_GROUPS = 4
SSM_STATE = 128
SSM_CHUNK = 128
HEAD_DIM = 64
N_KV_HEADS = 4
N_IDX_HEADS = 8
IDX_DIM = 64
TOP_K_MAX = 256
TOP_K_FRACTION = 4
ROPE_THETA = 10000.0
PAGE_SIZE = 128
N_BRANCH = 3

LANES = 128
ROW_TILE = 128
MXU_DIM = 256
DSA_KEY_UNIT = MXU_DIM
DSA_KEY_CHUNK_MAX = 3 * MXU_DIM
DENSE_TM = 512
VMEM_LIMIT = 56 * 1024 * 1024
INT_MIN = -(2 ** 31)
NEG_BIG = -1e30


def _cparams(sem):
    return pltpu.CompilerParams(dimension_semantics=sem, vmem_limit_bytes=VMEM_LIMIT)


def _rmsnorm_rows(x, g):
    return x * lax.rsqrt(jnp.mean(x * x, axis=-1, keepdims=True) + EPS) * g


def _norm_matmul_kernel(x_ref, g_ref, w_ref, o_ref, h_scr):
    @pl.when(pl.program_id(1) == 0)
    def _():
        h_scr[...] = _rmsnorm_rows(x_ref[...], g_ref[...]).astype(BF16)

    o_ref[...] = jnp.dot(h_scr[...], w_ref[...], preferred_element_type=F32)


def norm_matmul(x, g, w, tm, tn):
    m, d = x.shape
    n = w.shape[1]
    return pl.pallas_call(
        _norm_matmul_kernel,
        grid=(m // tm, n // tn),
        in_specs=[pl.BlockSpec((tm, d), lambda i, j: (i, 0)),
                  pl.BlockSpec((1, d), lambda i, j: (0, 0)),
                  pl.BlockSpec((d, tn), lambda i, j: (0, j))],
        out_specs=pl.BlockSpec((tm, tn), lambda i, j: (i, j)),
        out_shape=jax.ShapeDtypeStruct((m, n), F32),
        scratch_shapes=[pltpu.VMEM((tm, d), BF16)],
        compiler_params=_cparams(("parallel", "arbitrary")),
        name="inproj_main",
    )(x, g, w)


def _attn_proj_kernel(x_ref, g_ref, w_ref, cos_ref, sin_ref,
                      q_ref, qi_ref, k_ref, v_ref, small_ref, kb_ref, vbt_ref, kidxb_ref,
                      *, n_heads, n_idx, n_kv):
    h = _rmsnorm_rows(x_ref[...], g_ref[...]).astype(BF16)
    acc = jnp.dot(h, w_ref[...], preferred_element_type=F32)
    cos = cos_ref[...]
    sin = sin_ref[...]
    lane = lax.broadcasted_iota(I32, cos.shape, 1)
    lo_half = (lane & (HEAD_DIM // 2)) == 0

    def rope(xb, c, s):
        partner = jnp.where(lo_half, pltpu.roll(xb, LANES - HEAD_DIM // 2, 1), pltpu.roll(xb, HEAD_DIM // 2, 1))
        return xb * c + partner * s

    col = 0
    scale = HEAD_DIM ** -0.5 * math.log2(math.e)
    for p in range(n_heads // 2):
        blk = rope(acc[:, col:col + LANES], cos, sin) * scale
        q_ref[2 * p] = blk[:, :HEAD_DIM].astype(BF16)
        q_ref[2 * p + 1] = blk[:, HEAD_DIM:].astype(BF16)
        col += LANES
    for p in range(n_idx // 2):
        blk = rope(acc[:, col:col + LANES], cos, sin)
        qi_ref[2 * p] = blk[:, :IDX_DIM].astype(BF16)
        qi_ref[2 * p + 1] = blk[:, IDX_DIM:].astype(BF16)
        col += LANES
    kw = n_kv * HEAD_DIM
    first = lane < HEAD_DIM
    ones_lane = jnp.where(lane == HEAD_DIM, 1.0, 0.0)
    for p in range(kw // LANES):
        blk = rope(acc[:, col:col + LANES], cos, sin)
        k_ref[:, p * LANES:(p + 1) * LANES] = blk
        kb_ref[2 * p] = jnp.where(first, blk, 0.0).astype(BF16)
        kb_ref[2 * p + 1] = jnp.where(first, pltpu.roll(blk, HEAD_DIM, 1), 0.0).astype(BF16)
        col += LANES
    v_ref[...] = acc[:, col:col + kw]
    for p in range(kw // LANES):
        blk = acc[:, col:col + LANES]
        vbt_ref[2 * p] = jnp.where(first, blk, ones_lane).T.astype(BF16)
        vbt_ref[2 * p + 1] = jnp.where(first, pltpu.roll(blk, HEAD_DIM, 1), ones_lane).T.astype(BF16)
        col += LANES
    is_kidx = lane < IDX_DIM
    blk = rope(acc[:, col:col + LANES], jnp.where(is_kidx, cos, 1.0), jnp.where(is_kidx, sin, 0.0))
    small_ref[...] = blk
    kidxb_ref[...] = blk[:, :IDX_DIM].astype(BF16)


def attn_proj(x, g, w, cos, sin, n_heads, n_idx, n_kv, tm):
    m, d = x.shape
    n = w.shape[1]
    kw = n_kv * HEAD_DIM
    row = lambda i: (i, 0)
    return pl.pallas_call(
        functools.partial(_attn_proj_kernel, n_heads=n_heads, n_idx=n_idx, n_kv=n_kv),
        grid=(m // tm,),
        in_specs=[pl.BlockSpec((tm, d), row),
                  pl.BlockSpec((1, d), lambda i: (0, 0)),
                  pl.BlockSpec((d, n), lambda i: (0, 0)),
                  pl.BlockSpec((tm, LANES), row),
                  pl.BlockSpec((tm, LANES), row)],
        out_specs=[pl.BlockSpec((n_heads, tm, HEAD_DIM), lambda i: (0, i, 0)),
                   pl.BlockSpec((n_idx, tm, IDX_DIM), lambda i: (0, i, 0)),
                   pl.BlockSpec((tm, kw), row),
                   pl.BlockSpec((tm, kw), row),
                   pl.BlockSpec((tm, LANES), row),
                   pl.BlockSpec((n_kv, tm, LANES), lambda i: (0, i, 0)),
                   pl.BlockSpec((n_kv, LANES, tm), lambda i: (0, 0, i)),
                   pl.BlockSpec((tm, IDX_DIM), row)],
        out_shape=[jax.ShapeDtypeStruct((n_heads, m, HEAD_DIM), BF16),
                   jax.ShapeDtypeStruct((n_idx, m, IDX_DIM), BF16),
                   jax.ShapeDtypeStruct((m, kw), F32),
                   jax.ShapeDtypeStruct((m, kw), F32),
                   jax.ShapeDtypeStruct((m, LANES), F32),
                   jax.ShapeDtypeStruct((n_kv, m, LANES), BF16),
                   jax.ShapeDtypeStruct((n_kv, LANES, m), BF16),
                   jax.ShapeDtypeStruct((m, IDX_DIM), BF16)],
        compiler_params=_cparams(("parallel",)),
        name="inproj_attn",
    )(x, g, w, cos, sin)


def _merge_kernel(yap_ref, ybp_ref, ycp_ref, yas_ref, ybs_ref, ycs_ref, g0_ref, g1_ref, g2_ref, x_ref, valid_ref,
                  pa_ref, pb_ref, pc_ref, wo_ref, o_ref, *, n_prompt_tiles):
    in_prompt = pl.program_id(0) < n_prompt_tiles

    def branch(yp_ref, ys_ref, p_ref, g_ref):
        y = jnp.where(in_prompt, yp_ref[...], ys_ref[...])
        return jax.nn.sigmoid(g_ref[...]) * jnp.dot(y, p_ref[...], preferred_element_type=F32)

    merged = (branch(yap_ref, yas_ref, pa_ref, g0_ref) + branch(ybp_ref, ybs_ref, pb_ref, g1_ref)
              + branch(ycp_ref, ycs_ref, pc_ref, g2_ref))
    xn = x_ref[...] + jnp.dot(merged.astype(BF16), wo_ref[...], preferred_element_type=F32)
    o_ref[...] = jnp.where(valid_ref[...] > 0.0, xn, 0.0)


def merge(y_prompt, y_sample, main, gate_col, x, valid, pa, pb, pc, wo, tm):
    m, d = x.shape
    mp = y_prompt[0].shape[0]
    assert mp % tm == 0 and (m - mp) % tm == 0 and all(y.shape[0] == m - mp for y in y_sample)
    n_p = mp // tm
    row = lambda i: (i, 0)
    prow = lambda i: (jnp.minimum(i, n_p - 1), 0)
    srow = lambda i: (jnp.maximum(i - n_p, 0), 0)
    wspec = pl.BlockSpec((d, d), lambda i: (0, 0))
    return pl.pallas_call(
        functools.partial(_merge_kernel, n_prompt_tiles=n_p),
        grid=(m // tm,),
        in_specs=[pl.BlockSpec((tm, d), prow), pl.BlockSpec((tm, d), prow), pl.BlockSpec((tm, d), prow),
                  pl.BlockSpec((tm, d), srow), pl.BlockSpec((tm, d), srow), pl.BlockSpec((tm, d), srow),
                  pl.BlockSpec((tm, d), lambda i: (i, gate_col)),
                  pl.BlockSpec((tm, d), lambda i: (i, gate_col + 1)),
                  pl.BlockSpec((tm, d), lambda i: (i, gate_col + 2)),
                  pl.BlockSpec((tm, d), row),
                  pl.BlockSpec((tm, 1), row),
                  wspec, wspec, wspec, wspec],
        out_specs=pl.BlockSpec((tm, d), row),
        out_shape=jax.ShapeDtypeStruct((m, d), F32),
        compiler_params=_cparams(("parallel",)),
        name="merge",
    )(*y_prompt, *y_sample, main, main, main, x, valid, pa, pb, pc, wo)


def _mlp_kernel(x_ref, g_ref, wu_ref, wd_ref, o_ref, h_scr, acc_scr):
    j = pl.program_id(1)

    @pl.when(j == 0)
    def _():
        h_scr[...] = _rmsnorm_rows(x_ref[...], g_ref[...]).astype(BF16)
        acc_scr[...] = x_ref[...]

    u = jnp.dot(h_scr[...], wu_ref[...], preferred_element_type=F32)
    u = jnp.square(jnp.maximum(u, 0.0)).astype(BF16)
    acc_scr[...] += jnp.dot(u, wd_ref[...], preferred_element_type=F32)

    @pl.when(j == pl.num_programs(1) - 1)
    def _():
        o_ref[...] = acc_scr[...]


def mlp(x, g, wu, wd, tm, tf):
    m, d = x.shape
    f = wu.shape[1]
    return pl.pallas_call(
        _mlp_kernel,
        grid=(m // tm, f // tf),
        in_specs=[pl.BlockSpec((tm, d), lambda i, j: (i, 0)),
                  pl.BlockSpec((1, d), lambda i, j: (0, 0)),
                  pl.BlockSpec((d, tf), lambda i, j: (0, j)),
                  pl.BlockSpec((tf, d), lambda i, j: (j, 0))],
        out_specs=pl.BlockSpec((tm, d), lambda i, j: (i, 0)),
        out_shape=jax.ShapeDtypeStruct((m, d), F32),
        scratch_shapes=[pltpu.VMEM((tm, d), BF16), pltpu.VMEM((tm, d), F32)],
        compiler_params=_cparams(("parallel", "arbitrary")),
        name="mlp",
    )(x, g, wu, wd)


def _final_norm_kernel(x_ref, g_ref, o_ref):
    o_ref[...] = _rmsnorm_rows(x_ref[...], g_ref[...])


def final_rmsnorm(x, g, tm):
    m, d = x.shape
    return pl.pallas_call(
        _final_norm_kernel,
        grid=(m // tm,),
        in_specs=[pl.BlockSpec((tm, d), lambda i: (i, 0)), pl.BlockSpec((1, d), lambda i: (0, 0))],
        out_specs=pl.BlockSpec((tm, d), lambda i: (i, 0)),
        out_shape=jax.ShapeDtypeStruct((m, d), F32),
        compiler_params=_cparams(("parallel",)),
        name="final_norm",
    )(x, g)


def _softplus(x):
    return jnp.maximum(x, 0.0) + jnp.log(1.0 + jnp.exp(-jnp.abs(x)))


def _silu(x):
    return x * jax.nn.sigmoid(x)


def _gelu_tanh(x):
    return 0.5 * x * (1.0 + jnp.tanh(math.sqrt(2.0 / math.pi) * (x + 0.044715 * (x * x * x))))


def _causal_conv_rows(ext_ref, x, w_ref, b_ref, rows):
    ext_ref[8:8 + rows, :] = x
    out = b_ref[...] + w_ref[CONV_W - 1:CONV_W, :] * x
    for j in range(CONV_W - 1):
        out = out + w_ref[j:j + 1, :] * ext_ref[5 + j:5 + j + rows, :]
    ext_ref[5:8, :] = ext_ref[5 + rows:8 + rows, :]
    return out


def _lru_gates(u, wg_ref, ba_ref, bx_ref, clam_ref, n_grp):
    gw = u.shape[1] // n_grp
    rs, is_ = [], []
    for g in range(n_grp):
        rg = jnp.dot(u[:, g * gw:(g + 1) * gw].astype(BF16), wg_ref[g], preferred_element_type=F32)
        rs.append(rg[:, :gw])
        is_.append(rg[:, gw:])
    r = jax.nn.sigmoid(jnp.concatenate(rs, axis=1) + ba_ref[...])
    i = jax.nn.sigmoid(jnp.concatenate(is_, axis=1) + bx_ref[...])
    log_a = clam_ref[...] * r
    a = jnp.exp(log_a)
    b = jnp.sqrt(1.0 - jnp.exp(2.0 * log_a)) * (i * u)
    return a, b


def _lru_prompt_kernel(x_ref, g_ref, cw_ref, cb_ref, wg_ref, ba_ref, bx_ref, clam_ref,
                       y_ref, hout_ref, ext_scr, h_scr, *, pad, n_grp):
    c = pl.program_id(1)
    rows = x_ref.shape[0]

    @pl.when(c == 0)
    def _():
        ext_scr[0:8, :] = jnp.zeros((8, ext_scr.shape[1]), F32)
        h_scr[...] = jnp.zeros(h_scr.shape, F32)

    u = _causal_conv_rows(ext_scr, x_ref[...], cw_ref, cb_ref, rows)
    a, b = _lru_gates(u, wg_ref, ba_ref, bx_ref, clam_ref, n_grp)
    ridx = lax.broadcasted_iota(I32, (rows, 1), 0)
    b = jnp.where(c * rows + ridx >= pad, b, 0.0)
    s = 1
    while s < rows:
        keep = ridx >= s
        a_sh = jnp.where(keep, pltpu.roll(a, s, 0), 1.0)
        b_sh = jnp.where(keep, pltpu.roll(b, s, 0), 0.0)
        b = a * b_sh + b
        a = a * a_sh
        s *= 2
    h = a * h_scr[...] + b
    h_scr[...] = h[rows - 1:rows, :]
    hout_ref[0] = h[rows - 1:rows, :]
    y_ref[...] = (h * _gelu_tanh(g_ref[...])).astype(BF16)


def lru_prompt(main, xcol, gcol, nb, tp, pad, cw, cb, wg, ba, bx, clam):
    d = cw.shape[1]
    n_grp = wg.shape[0]
    nt = tp // ROW_TILE
    const2 = lambda b, c: (0, 0)
    return pl.pallas_call(
        functools.partial(_lru_prompt_kernel, pad=pad, n_grp=n_grp),
        grid=(nb, nt),
        in_specs=[pl.BlockSpec((ROW_TILE, d), lambda b, c: (b * nt + c, xcol)),
                  pl.BlockSpec((ROW_TILE, d), lambda b, c: (b * nt + c, gcol)),
                  pl.BlockSpec((CONV_W, d), const2),
                  pl.BlockSpec((1, d), const2),
                  pl.BlockSpec(wg.shape, lambda b, c: (0, 0, 0)),
                  pl.BlockSpec((1, d), const2),
                  pl.BlockSpec((1, d), const2),
                  pl.BlockSpec((1, d), const2)],
        out_specs=[pl.BlockSpec((ROW_TILE, d), lambda b, c: (b * nt + c, 0)),
                   pl.BlockSpec((1, 1, d), lambda b, c: (b, 0, 0))],
        out_shape=[jax.ShapeDtypeStruct((nb * tp, d), BF16),
                   jax.ShapeDtypeStruct((nb, 1, d), F32)],
        scratch_shapes=[pltpu.VMEM((8 + ROW_TILE, d), F32), pltpu.VMEM((1, d), F32)],
        compiler_params=_cparams(("parallel", "arbitrary")),
        name="lru_prompt",
    )(main, main, cw, cb, wg, ba, bx, clam)


DT_LANE = IDX_DIM


def _split3_bf16(x):
    hi = x.astype(BF16)
    r1 = x - hi.astype(F32)
    mid = r1.astype(BF16)
    lo = (r1 - mid.astype(F32)).astype(BF16)
    return hi, mid, lo


def _ssd_prompt_kernel(xbc_ref, z_ref, small_ref, cw_ref, cb_ref, dtb_ref, aneg_ref, dfull_ref, ng_ref,
                       y_ref, hout_ref, ext_scr, ht_scr, y_scr, *, pad, n_heads, n_grp):
    c = pl.program_id(1)
    rows = xbc_ref.shape[0]
    d_ssm = z_ref.shape[1]
    hpg = n_heads // n_grp
    P = SSM_HEAD_DIM
    N = SSM_STATE

    @pl.when(c == 0)
    def _():
        ext_scr[0:8, :] = jnp.zeros((8, ext_scr.shape[1]), F32)
        ht_scr[...] = jnp.zeros(ht_scr.shape, F32)

    xbc = _silu(_causal_conv_rows(ext_scr, xbc_ref[...], cw_ref, cb_ref, rows))
    xs = xbc[:, :d_ssm]
    ridx = lax.broadcasted_iota(I32, (rows, 1), 0)
    dt = _softplus(small_ref[...] + dtb_ref[...])
    dt = jnp.where(c * rows + ridx >= pad, dt, 0.0)
    da = dt * aneg_ref[...]
    ii = lax.broadcasted_iota(I32, (rows, rows), 0)
    jj = lax.broadcasted_iota(I32, (rows, rows), 1)
    causal = jj <= ii
    tri = jnp.where(causal, 1.0, 0.0).astype(BF16)
    cum = sum(jnp.dot(tri, part, preferred_element_type=F32) for part in _split3_bf16(da))
    cum_t = cum.T
    dt_t = dt.T
    cum_last = cum[rows - 1:rows, :]
    dec_end = jnp.exp(cum_last - cum) * dt
    ecum = jnp.exp(cum)
    chunk_dec = jnp.exp(cum_last)
    for g in range(n_grp):
        bg = xbc[:, d_ssm + g * N:d_ssm + (g + 1) * N]
        cg = xbc[:, d_ssm + n_grp * N + g * N:d_ssm + n_grp * N + (g + 1) * N]
        bg16 = bg.astype(BF16)
        cg16 = cg.astype(BF16)
        cb = lax.dot_general(cg16, bg16, (((1,), (1,)), ((), ())), preferred_element_type=F32)
        bgt16 = bg.T.astype(BF16)
        for hh in range(hpg):
            h = g * hpg + hh
            ln = DT_LANE + h
            x_h = xs[:, h * P:(h + 1) * P]
            seg = cum[:, ln:ln + 1] - cum_t[ln:ln + 1, :]
            decay = jnp.where(causal, jnp.exp(jnp.where(causal, seg, 0.0)), 0.0)
            w_intra = (cb * decay * dt_t[ln:ln + 1, :]).astype(BF16)
            y_h = jnp.dot(w_intra, x_h.astype(BF16), preferred_element_type=F32)
            ht = ht_scr[h]
            y_h = y_h + jnp.dot(cg16, ht.astype(BF16), preferred_element_type=F32) * ecum[:, ln:ln + 1]
            xw = (x_h * dec_end[:, ln:ln + 1]).astype(BF16)
            ht_new = chunk_dec[:, ln:ln + 1] * ht + jnp.dot(bgt16, xw, preferred_element_type=F32)
            ht_scr[h] = ht_new
            hout_ref[0, h] = ht_new
            y_scr[:, h * P:(h + 1) * P] = y_h
    y = y_scr[...] + dfull_ref[...] * xs
    y = y * _silu(z_ref[...])
    y_ref[...] = _rmsnorm_rows(y, ng_ref[...]).astype(BF16)


def ssd_prompt(main, xbc_col, z_col, small, nb, tp, pad, cw, cb, dtb, aneg, dfull, ng, n_heads, n_grp):
    d_xbc = cw.shape[1]
    d_ssm = ng.shape[1]
    nt = tp // ROW_TILE
    const2 = lambda b, c: (0, 0)
    return pl.pallas_call(
        functools.partial(_ssd_prompt_kernel, pad=pad, n_heads=n_heads, n_grp=n_grp),
        grid=(nb, nt),
        in_specs=[pl.BlockSpec((ROW_TILE, d_xbc), lambda b, c: (b * nt + c, xbc_col)),
                  pl.BlockSpec((ROW_TILE, d_ssm), lambda b, c: (b * nt + c, z_col)),
                  pl.BlockSpec((ROW_TILE, LANES), lambda b, c: (b * nt + c, 0)),
                  pl.BlockSpec((CONV_W, d_xbc), const2),
                  pl.BlockSpec((1, d_xbc), const2),
                  pl.BlockSpec((1, LANES), const2),
                  pl.BlockSpec((1, LANES), const2),
                  pl.BlockSpec((1, d_ssm), const2),
                  pl.BlockSpec((1, d_ssm), const2)],
        out_specs=[pl.BlockSpec((ROW_TILE, d_ssm), lambda b, c: (b * nt + c, 0)),
                   pl.BlockSpec((1, n_heads, SSM_STATE, SSM_HEAD_DIM), lambda b, c: (b, 0, 0, 0))],
        out_shape=[jax.ShapeDtypeStruct((nb * tp, d_ssm), BF16),
                   jax.ShapeDtypeStruct((nb, n_heads, SSM_STATE, SSM_HEAD_DIM), F32)],
        scratch_shapes=[pltpu.VMEM((8 + ROW_TILE, d_xbc), F32),
                        pltpu.VMEM((n_heads, SSM_STATE, SSM_HEAD_DIM), F32),
                        pltpu.VMEM((ROW_TILE, d_ssm), F32)],
        compiler_params=_cparams(("parallel", "arbitrary")),
        name="ssd_prompt",
    )(main, main, small, cw, cb, dtb, aneg, dfull, ng)


WIDX_LANE = IDX_DIM + 16


def _sortable_key(score):
    score = jnp.where(score == 0.0, 0.0, score)
    bits = pltpu.bitcast(score, I32)
    return jnp.where(bits < 0, bits ^ jnp.int32(0x7FFFFFFF), bits)


def _lane_blocks(x):
    return [x[:, u * LANES:(u + 1) * LANES] for u in range(x.shape[1] // LANES)]


def _row_total(x):
    return jnp.broadcast_to(jnp.sum(x, axis=1, keepdims=True), x.shape)


def _select_threshold(count, like, ktop, col_bits, bits_per_step=1):
    zero = jnp.zeros(like.shape, I32)
    assert 32 % bits_per_step == 0

    def t_step(it, t):
        shift = 32 - bits_per_step * (it + 1)
        passed = zero
        for j in range(1, 2 ** bits_per_step):
            cand = t + lax.shift_left(jnp.int32(j), shift)
            passed = passed + jnp.where(count(lambda keys, idx: keys >= cand) >= ktop, 1, 0)
        return t + lax.shift_left(passed, shift)

    t = lax.fori_loop(0, 32 // bits_per_step, t_step, jnp.full(like.shape, INT_MIN, I32))
    return t, _tie_cutoff(count, t, ktop, col_bits)


def _tie_cutoff(count, t, ktop, col_bits):
    zero = jnp.zeros(t.shape, I32)
    n_ge = count(lambda keys, idx: keys >= t)
    n_gt = count(lambda keys, idx: keys > t)
    live = t != INT_MIN
    excess = live & (n_ge > ktop)
    need = ktop - n_gt

    def j_search():
        def j_step(it, j):
            cand = j + lax.shift_left(jnp.int32(1), col_bits - 1 - it)
            n_tie = count(lambda keys, idx: (keys == t) & (idx < cand))
            return jnp.where(n_tie <= need, cand, j)
        return lax.fori_loop(0, col_bits, j_step, zero)

    any_excess = jnp.max(jnp.where(excess, 1, 0)) > 0
    j_cut = lax.cond(any_excess, j_search, lambda: zero)
    big = jnp.int32(2 ** 30)
    return jnp.where(excess, j_cut, jnp.where(live, big, 0))


def _selected(keys, idx, t, j_cut):
    return (keys > t) | ((keys == t) & (idx < j_cut))


def _index_scores(s, w):
    rows, n_h = w.shape
    s = jnp.maximum(s, 0.0)
    acc = w[:, 0:1] * s[0:rows]
    for h in range(1, n_h):
        acc = acc + w[:, h:h + 1] * s[h * rows:(h + 1) * rows]
    return acc


def _dsa_prompt_kernel(q_ref, qi_ref, small_ref, kb_ref, vbt_ref, kidx_ref, o_ref,
                       key_scr, qaug_scr, m_scr, acc_scr, *, ktop, pad, tkc, col_bits):
    i = pl.program_id(1)
    n_q, tq, hd = q_ref.shape
    n_idx = qi_ref.shape[0]
    n_kv = kb_ref.shape[0]
    qpk = n_q // n_kv
    r4 = qpk * tq
    q0 = i * tq
    n_chunks = (q0 + tq + tkc - 1) // tkc
    nt_dims = (((1,), (1,)), ((), ()))

    wt = small_ref[...].T[WIDX_LANE:WIDX_LANE + n_idx, :] * ((n_idx * IDX_DIM) ** -0.5)
    qi_all = qi_ref[...].reshape(n_idx * tq, qi_ref.shape[2])
    qcol = q0 + lax.broadcasted_iota(I32, (tkc, tq), 1)
    krow = lax.broadcasted_iota(I32, (tkc, tq), 0)

    def idx_products(c):
        k0 = pl.multiple_of(c * tkc, tkc)
        return lax.dot_general(kidx_ref[pl.ds(k0, tkc), :], qi_all, nt_dims, preferred_element_type=F32)

    def store_keys(c, s):
        acc = wt[0:1] * jnp.maximum(s[:, 0:tq], 0.0)
        for h in range(1, n_idx):
            acc = acc + wt[h:h + 1] * jnp.maximum(s[:, h * tq:(h + 1) * tq], 0.0)
        kidx = krow + c * tkc
        admissible = (kidx >= pad) & (kidx <= qcol)
        key_scr[c] = jnp.where(admissible, _sortable_key(acc), INT_MIN)

    def score_pair(i2, carry):
        ca = 2 * i2
        cb = jnp.minimum(ca + 1, n_chunks - 1)
        sa = idx_products(ca)
        sb = idx_products(cb)
        store_keys(ca, sa)
        store_keys(cb, sb)
        return carry

    lax.fori_loop(0, (n_chunks + 1) // 2, score_pair, 0)

    fold = 8 * 8

    def count(pred):
        def body(c, cnt):
            hit = jnp.where(pred(key_scr[c], krow + c * tkc), 1, 0)
            return cnt + jnp.sum(hit.reshape(tkc // fold, fold, tq), axis=0)
        cnt = lax.fori_loop(0, n_chunks, body, jnp.zeros((fold, tq), I32))
        return jnp.sum(cnt, axis=0, keepdims=True)

    t, j_cut = _select_threshold(count, jnp.zeros((1, tq), I32), ktop, col_bits)

    rr = lax.broadcasted_iota(I32, (tq, tq), 0)
    cc = lax.broadcasted_iota(I32, (tq, tq), 1)
    onehot = jnp.where(rr == cc, 1.0, 0.0).astype(BF16)
    zeros_tail = jnp.zeros((tq, qaug_scr.shape[1] - tq - hd), BF16)
    for h in range(n_q):
        qaug_scr[h * tq:(h + 1) * tq, :] = jnp.concatenate([onehot, q_ref[h], zeros_tail], axis=1)
    m_scr[...] = jnp.full(m_scr.shape, NEG_BIG, F32)
    acc_scr[...] = jnp.zeros(acc_scr.shape, F32)

    def mask_bias(c):
        idx = krow + pl.multiple_of(c * tkc, tkc)
        return jnp.where(_selected(key_scr[c], idx, t, j_cut), 0.0, NEG_BIG).astype(BF16)

    def qk(g, c, bias_t):
        k0 = pl.multiple_of(c * tkc, tkc)
        lhs = jnp.concatenate([bias_t, kb_ref[g, pl.ds(k0, tkc), :]], axis=1)
        return lax.dot_general(lhs, qaug_scr[g * r4:(g + 1) * r4, :], nt_dims,
                               preferred_element_type=F32)

    def attend(c, carry):
        k0 = pl.multiple_of(c * tkc, tkc)
        bias_t = mask_bias(c)

        def softmax(g, s):
            m_prev = m_scr[g * 8:g * 8 + 1, :]
            m_new = jnp.maximum(m_prev, jnp.max(s, axis=0, keepdims=True))
            m_scr[g * 8:g * 8 + 1, :] = m_new
            return jnp.exp2(s - m_new).astype(BF16), jnp.exp2(m_prev - m_new)

        def pv(g, p, alpha):
            gs = slice(g * LANES, (g + 1) * LANES)
            acc_scr[gs, :] = alpha * acc_scr[gs, :] + jnp.dot(vbt_ref[g, :, pl.ds(k0, tkc)], p,
                                                               preferred_element_type=F32)

        s_q = {0: qk(0, c, bias_t)}
        p_q = {}
        for step in range(n_kv + 1):
            if step + 1 < n_kv:
                s_q[step + 1] = qk(step + 1, c, bias_t)
            if step < n_kv:
                p_q[step] = softmax(step, s_q.pop(step))
            if 1 <= step <= n_kv:
                pv(step - 1, *p_q.pop(step - 1))
        return carry

    lax.fori_loop(0, n_chunks, attend, 0)
    for pair in range(n_q // 2):
        g, hh = (2 * pair) // qpk, (2 * pair) % qpk
        cols = slice(hh * tq, (hh + 2) * tq)
        denom = acc_scr[g * LANES + hd:g * LANES + hd + 1, cols]
        denom = jnp.where(denom > 0.0, denom, 1.0)
        out_t = acc_scr[g * LANES:g * LANES + hd, cols] / denom
        both = jnp.concatenate([out_t[:, :tq], out_t[:, tq:]], axis=0).T
        o_ref[:, 2 * pair * hd:(2 * pair + 2) * hd] = both.astype(BF16)


def dsa_prompt(q, qi, small, kb, vbt, kidxb, nb, tp, pad, ktop, tkc):
    n_q, _, hd = q.shape
    n_idx = qi.shape[0]
    n_kv = kb.shape[0]
    tq = ROW_TILE
    nt = tp // tq
    r4 = (n_q // n_kv) * tq
    col_bits = max(1, int(math.ceil(math.log2(tp))))
    return pl.pallas_call(
        functools.partial(_dsa_prompt_kernel, ktop=ktop, pad=pad, tkc=tkc, col_bits=col_bits),
        grid=(nb, nt),
        in_specs=[pl.BlockSpec((n_q, tq, hd), lambda b, i: (0, b * nt + i, 0)),
                  pl.BlockSpec((n_idx, tq, IDX_DIM), lambda b, i: (0, b * nt + i, 0)),
                  pl.BlockSpec((tq, LANES), lambda b, i: (b * nt + i, 0)),
                  pl.BlockSpec((n_kv, tp, LANES), lambda b, i: (0, b, 0)),
                  pl.BlockSpec((n_kv, LANES, tp), lambda b, i: (0, 0, b)),
                  pl.BlockSpec((tp, IDX_DIM), lambda b, i: (b, 0))],
        out_specs=pl.BlockSpec((tq, n_q * hd), lambda b, i: (b * nt + i, 0)),
        out_shape=jax.ShapeDtypeStruct((nb * tp, n_q * hd), BF16),
        scratch_shapes=[pltpu.VMEM((tp // tkc, tkc, tq), I32),
                        pltpu.VMEM((n_q * tq, 2 * LANES), BF16),
                        pltpu.VMEM((n_kv * 8, r4), F32),
                        pltpu.VMEM((n_kv * LANES, r4), F32)],
        compiler_params=_cparams(("parallel", "arbitrary")),
        name="dsa_prompt",
    )(q, qi, small, kb, vbt, kidxb)


def _lru_sample_kernel(x_ref, g_ref, hist_ref, h0_ref, cw_ref, cb_ref, wg_ref, ba_ref, bx_ref, clam_ref,
                       y_ref, hout_ref, convout_ref, hist_scr, h_scr, *, n_grp):
    t = pl.program_id(0)

    @pl.when(t == 0)
    def _():
        hist_scr[...] = hist_ref[...]
        h_scr[...] = h0_ref[...]

    x = x_ref[...]
    u = cb_ref[...] + cw_ref[CONV_W - 1:CONV_W, :] * x
    for j in range(CONV_W - 1):
        u = u + cw_ref[j:j + 1, :] * hist_scr[j]
    for j in range(CONV_W - 2):
        hist_scr[j] = hist_scr[j + 1]
    hist_scr[CONV_W - 2] = x
    a, b = _lru_gates(u, wg_ref, ba_ref, bx_ref, clam_ref, n_grp)
    h = a * h_scr[...] + b
    h_scr[...] = h
    y_ref[...] = (h * _gelu_tanh(g_ref[...])).astype(BF16)
    hout_ref[...] = h
    convout_ref[...] = hist_scr[...]


def lru_sample(main, xcol, gcol, row0, db, ds, hist, h0, cw, cb, wg, ba, bx, clam):
    d = cw.shape[1]
    n_grp = wg.shape[0]
    blk0 = row0 // db
    const2 = lambda t: (0, 0)
    const3 = lambda t: (0, 0, 0)
    return pl.pallas_call(
        functools.partial(_lru_sample_kernel, n_grp=n_grp),
        grid=(ds,),
        in_specs=[pl.BlockSpec((db, d), lambda t: (blk0 + t, xcol)),
                  pl.BlockSpec((db, d), lambda t: (blk0 + t, gcol)),
                  pl.BlockSpec((CONV_W - 1, db, d), const3),
                  pl.BlockSpec((db, d), const2),
                  pl.BlockSpec((CONV_W, d), const2),
                  pl.BlockSpec((1, d), const2),
                  pl.BlockSpec(wg.shape, const3),
                  pl.BlockSpec((1, d), const2),
                  pl.BlockSpec((1, d), const2),
                  pl.BlockSpec((1, d), const2)],
        out_specs=[pl.BlockSpec((db, d), lambda t: (t, 0)),
                   pl.BlockSpec((db, d), const2),
                   pl.BlockSpec((CONV_W - 1, db, d), const3)],
        out_shape=[jax.ShapeDtypeStruct((ds * db, d), BF16),
                   jax.ShapeDtypeStruct((db, d), F32),
                   jax.ShapeDtypeStruct((CONV_W - 1, db, d), F32)],
        scratch_shapes=[pltpu.VMEM((CONV_W - 1, db, d), F32), pltpu.VMEM((db, d), F32)],
        compiler_params=_cparams(("arbitrary",)),
        name="lru_sample",
    )(main, main, hist, h0, cw, cb, wg, ba, bx, clam)


def _ssd_sample_kernel(*refs, ds, n_heads, n_grp):
    xbc_refs = refs[0:ds]
    z_refs = refs[ds:2 * ds]
    small_refs = refs[2 * ds:3 * ds]
    (hist_ref, h0_ref, cw_ref, cb_ref, dtb_ref, aneg_ref, dfull_ref, ng_ref, exp_ref,
     y_ref, hout_ref, convout_ref, yint_scr) = refs[3 * ds:]
    bb = xbc_refs[0].shape[0]
    d_ssm = z_refs[0].shape[1]
    hpg = n_heads // n_grp
    P = SSM_HEAD_DIM
    N = SSM_STATE
    gw = hpg * P

    ext = [hist_ref[j] for j in range(CONV_W - 1)] + [r[...] for r in xbc_refs]
    for j in range(CONV_W - 1):
        convout_ref[j] = ext[ds + j]
    xbc = []
    for t in range(ds):
        u = cb_ref[...]
        for j in range(CONV_W):
            u = u + cw_ref[j:j + 1, :] * ext[t + j]
        xbc.append(_silu(u))
    xs = [v[:, :d_ssm] for v in xbc]
    bm = [v[:, d_ssm:d_ssm + n_grp * N] for v in xbc]
    cm = [v[:, d_ssm + n_grp * N:] for v in xbc]
    dt = [_softplus(r[...] + dtb_ref[...]) for r in small_refs]
    cum = []
    run = jnp.zeros_like(dt[0])
    for t in range(ds):
        run = run + dt[t] * aneg_ref[...]
        cum.append(run)
    lane = lax.broadcasted_iota(I32, (bb, LANES), 1)
    coefs = []
    for t in range(ds):
        for j in range(t + 1):
            cbh = jnp.zeros((bb, LANES), F32)
            for g in range(n_grp):
                dotg = jnp.sum(cm[t][:, g * N:(g + 1) * N] * bm[j][:, g * N:(g + 1) * N], axis=1, keepdims=True)
                in_g = (lane >= DT_LANE + g * hpg) & (lane < DT_LANE + (g + 1) * hpg)
                cbh = cbh + jnp.where(in_g, dotg, 0.0)
            coefs.append(cbh * jnp.exp(cum[t] - cum[j]) * dt[j])
    n_intra = len(coefs)
    coefs += [jnp.exp(c) for c in cum]
    coefs += [jnp.exp(cum[ds - 1] - cum[j]) * dt[j] for j in range(ds)]
    coefs.append(jnp.exp(cum[ds - 1]))
    stack = jnp.concatenate(coefs, axis=0)
    wide = sum(jnp.dot(part, exp_ref[...], preferred_element_type=F32) for part in _split3_bf16(stack))
    wide = [wide[k * bb:(k + 1) * bb] for k in range(len(coefs))]
    intra_w = wide[:n_intra]
    ecum_w = wide[n_intra:n_intra + ds]
    dend_w = wide[n_intra + ds:n_intra + 2 * ds]
    cdec_w = wide[n_intra + 2 * ds]
    xw = [dend_w[j] * xs[j] for j in range(ds)]
    zpad = 8 - ds
    for b in range(bb):
        for g in range(n_grp):
            def rows_of(arrs, lo, width):
                parts = [a[b:b + 1, lo:lo + width] for a in arrs]
                if zpad:
                    parts.append(jnp.zeros((zpad, width), F32))
                return jnp.concatenate(parts, axis=0).astype(BF16)
            hg = h0_ref[b, g * hpg:(g + 1) * hpg].reshape(gw, N)
            yint = lax.dot_general(rows_of(cm, g * N, N), hg.astype(BF16), (((1,), (1,)), ((), ())),
                                   preferred_element_type=F32)
            for t in range(ds):
                yint_scr[t, b:b + 1, g * gw:(g + 1) * gw] = yint[t:t + 1]
            upd = lax.dot_general(rows_of(xw, g * gw, gw), rows_of(bm, g * N, N), (((0,), (0,)), ((), ())),
                                  preferred_element_type=F32)
            for hh in range(hpg):
                h = g * hpg + hh
                cd = cdec_w[b:b + 1, h * P:h * P + 1]
                hout_ref[b, h] = cd * hg[hh * P:(hh + 1) * P] + upd[hh * P:(hh + 1) * P]
    k = 0
    for t in range(ds):
        y = ecum_w[t] * yint_scr[t] + dfull_ref[...] * xs[t]
        for j in range(t + 1):
            y = y + intra_w[k] * xs[j]
            k += 1
        y = y * _silu(z_refs[t][...])
        y_ref[t] = _rmsnorm_rows(y, ng_ref[...]).astype(BF16)


def ssd_sample(main, xbc_col, z_col, small, row0, db, ds, bb, hist, h0, cw, cb, dtb, aneg, dfull, ng, expand,
               n_heads, n_grp):
    d_xbc = cw.shape[1]
    d_ssm = ng.shape[1]
    const2 = lambda i: (0, 0)

    def at_t(t, width, col):
        blk0 = (row0 + t * db) // bb
        return pl.BlockSpec((bb, width), lambda i: (blk0 + i, col))

    in_specs = ([at_t(t, d_xbc, xbc_col) for t in range(ds)] + [at_t(t, d_ssm, z_col) for t in range(ds)]
                + [at_t(t, LANES, 0) for t in range(ds)]
                + [pl.BlockSpec((CONV_W - 1, bb, d_xbc), lambda i: (0, i, 0)),
                   pl.BlockSpec((bb, n_heads, SSM_HEAD_DIM, SSM_STATE), lambda i: (i, 0, 0, 0)),
                   pl.BlockSpec((CONV_W, d_xbc), const2),
                   pl.BlockSpec((1, d_xbc), const2),
                   pl.BlockSpec((1, LANES), const2),
                   pl.BlockSpec((1, LANES), const2),
                   pl.BlockSpec((1, d_ssm), const2),
                   pl.BlockSpec((1, d_ssm), const2),
                   pl.BlockSpec((LANES, d_ssm), const2)])
    return pl.pallas_call(
        functools.partial(_ssd_sample_kernel, ds=ds, n_heads=n_heads, n_grp=n_grp),
        grid=(db // bb,),
        in_specs=in_specs,
        out_specs=[pl.BlockSpec((ds, bb, d_ssm), lambda i: (0, i, 0)),
                   pl.BlockSpec((bb, n_heads, SSM_HEAD_DIM, SSM_STATE), lambda i: (i, 0, 0, 0)),
                   pl.BlockSpec((CONV_W - 1, bb, d_xbc), lambda i: (0, i, 0))],
        out_shape=[jax.ShapeDtypeStruct((ds, db, d_ssm), BF16),
                   jax.ShapeDtypeStruct((db, n_heads, SSM_HEAD_DIM, SSM_STATE), F32),
                   jax.ShapeDtypeStruct((CONV_W - 1, db, d_xbc), F32)],
        scratch_shapes=[pltpu.VMEM((ds, bb, d_ssm), F32)],
        compiler_params=_cparams(("parallel",)),
        name="ssd_sample",
    )(*([main] * ds + [main] * ds + [small] * ds + [hist, h0, cw, cb, dtb, aneg, dfull, ng, expand]))


def _dsa_sample_kernel(pt_ref, *refs, n_pages, ds, ktop):
    del pt_ref
    q_ref, qi_ref, w_ref, knew_ref, vnew_ref, kxnew_ref = refs[0:6]
    kx_pages = refs[6:6 + n_pages]
    k_pages = refs[6 + n_pages:6 + 2 * n_pages]
    v_pages = refs[6 + 2 * n_pages:6 + 3 * n_pages]
    o_ref, key_scr, logit_scr, new_scr = refs[6 + 3 * n_pages:]
    n_rows, hd = q_ref.shape[1:]
    n_q = n_rows // ds
    n_idx, rq = qi_ref.shape[1:3]
    kw = knew_ref.shape[2]
    n_kv = kw // hd
    qpk = n_q // n_kv
    n_chunks = n_pages + 1

    w = w_ref[0] * ((n_idx * IDX_DIM) ** -0.5)
    qi = qi_ref[0].reshape(n_idx * rq, qi_ref.shape[3]).astype(BF16)
    qrow = lax.broadcasted_iota(I32, (rq, PAGE_SIZE), 0)
    lane = lax.broadcasted_iota(I32, (rq, PAGE_SIZE), 1)
    nt_dims = (((1,), (1,)), ((), ()))
    for p in range(n_pages):
        sc = _index_scores(jnp.dot(qi, kx_pages[p][0].astype(BF16), preferred_element_type=F32), w)
        key_scr[0, :, p * PAGE_SIZE:(p + 1) * PAGE_SIZE] = jnp.where(qrow < ds, _sortable_key(sc), INT_MIN)
    new_scr[...] = jnp.zeros(new_scr.shape, F32)
    new_scr[0:rq, 0:kxnew_ref.shape[2]] = kxnew_ref[0]
    sc = _index_scores(lax.dot_general(qi, new_scr[:, 0:kxnew_ref.shape[2]].astype(BF16), nt_dims,
                                       preferred_element_type=F32), w)
    admissible = (lane <= qrow) & (qrow < ds)
    key_scr[0, :, n_pages * PAGE_SIZE:] = jnp.where(admissible, _sortable_key(sc), INT_MIN)
    col_bits = max(1, int(math.ceil(math.log2(n_chunks * PAGE_SIZE))))

    def count(pred):
        cnt = jnp.zeros((rq, PAGE_SIZE), I32)
        for p in range(n_chunks):
            kk = key_scr[0, :, p * PAGE_SIZE:(p + 1) * PAGE_SIZE]
            cnt = cnt + jnp.where(pred(kk, lane + p * PAGE_SIZE), 1, 0)
        return _row_total(cnt)

    t, j_cut = _select_threshold(count, lane, ktop, col_bits, bits_per_step=4)

    assert n_q & (n_q - 1) == 0 and qpk & (qpk - 1) == 0 and hd & (hd - 1) == 0
    row_group = (lax.broadcasted_iota(I32, (n_rows, kw), 0) & (n_q - 1)) >> (qpk.bit_length() - 1)
    own_lanes = (lax.broadcasted_iota(I32, (n_rows, kw), 1) >> (hd.bit_length() - 1)) == row_group
    qexp = jnp.where(own_lanes, jnp.concatenate([q_ref[0]] * n_kv, axis=1), 0.0).astype(BF16)
    new_scr[0:rq, :] = knew_ref[0]
    knew = new_scr[...].astype(BF16)
    new_scr[0:rq, :] = vnew_ref[0]
    vnew = new_scr[...].astype(BF16)
    for p in range(n_chunks):
        kk = key_scr[0, :, p * PAGE_SIZE:(p + 1) * PAGE_SIZE]
        bias = jnp.where(_selected(kk, lane + p * PAGE_SIZE, t, j_cut), 0.0, NEG_BIG)
        bias_rows = jnp.concatenate([jnp.broadcast_to(bias[ts:ts + 1], (n_q, PAGE_SIZE)) for ts in range(ds)], axis=0)
        if p < n_pages:
            qk = jnp.dot(qexp, k_pages[p][0].astype(BF16), preferred_element_type=F32)
        else:
            qk = lax.dot_general(qexp, knew, nt_dims, preferred_element_type=F32)
        logit_scr[:, p * PAGE_SIZE:(p + 1) * PAGE_SIZE] = bias_rows + qk
    logits = logit_scr[...]
    prob = jnp.exp2(logits - jnp.max(logits, axis=1, keepdims=True))
    denom = jnp.sum(prob, axis=1, keepdims=True)
    prob = prob.astype(BF16)
    acc = jnp.zeros((n_rows, kw), F32)
    for p in range(n_chunks):
        pp = prob[:, p * PAGE_SIZE:(p + 1) * PAGE_SIZE]
        if p < n_pages:
            acc = acc + lax.dot_general(pp, v_pages[p][0].astype(BF16), nt_dims, preferred_element_type=F32)
        else:
            acc = acc + jnp.dot(pp, vnew, preferred_element_type=F32)
    out = jnp.zeros((n_rows, hd), F32)
    for g in range(n_kv):
        out = out + jnp.where(row_group[:, :hd] == g, acc[:, g * hd:(g + 1) * hd], 0.0)
    o_ref[0] = out / denom


def dsa_sample(page_table, q, qi, w, knew, vnew, kxnew, cache_k, cache_v, cache_kidx, layer, n_pool, ds, ktop):
    db, n_rows, hd = q.shape
    n_idx, rq = qi.shape[1:3]
    kw = knew.shape[2]
    n_pages = page_table.shape[1]

    def page_spec(width, p):
        return pl.BlockSpec((1, width, PAGE_SIZE), lambda b, pt: (layer * n_pool + pt[b, p], 0, 0))

    own3 = lambda b, pt: (b, 0, 0)
    own4 = lambda b, pt: (b, 0, 0, 0)
    in_specs = ([pl.BlockSpec((1, n_rows, hd), own3),
                 pl.BlockSpec((1, n_idx, rq, IDX_DIM), own4),
                 pl.BlockSpec((1, rq, n_idx), own3),
                 pl.BlockSpec((1, rq, kw), own3),
                 pl.BlockSpec((1, rq, kw), own3),
                 pl.BlockSpec((1, rq, IDX_DIM), own3)]
                + [page_spec(IDX_DIM, p) for p in range(n_pages)]
                + [page_spec(kw, p) for p in range(n_pages)]
                + [page_spec(kw, p) for p in range(n_pages)])
    grid_spec = pltpu.PrefetchScalarGridSpec(
        num_scalar_prefetch=1,
        grid=(db,),
        in_specs=in_specs,
        out_specs=pl.BlockSpec((1, n_rows, hd), own3),
        scratch_shapes=[pltpu.VMEM((1, rq, (n_pages + 1) * PAGE_SIZE), I32),
                        pltpu.VMEM((n_rows, (n_pages + 1) * PAGE_SIZE), F32),
                        pltpu.VMEM((PAGE_SIZE, kw), F32)])
    return pl.pallas_call(
        functools.partial(_dsa_sample_kernel, n_pages=n_pages, ds=ds, ktop=ktop),
        grid_spec=grid_spec,
        out_shape=jax.ShapeDtypeStruct((db, n_rows, hd), F32),
        compiler_params=_cparams(("arbitrary",)),
        name="dsa_sample",
    )(page_table, q, qi, w, knew, vnew, kxnew,
      *([cache_kidx] * n_pages + [cache_k] * n_pages + [cache_v] * n_pages))


def _round_up(x, m):
    return (x + m - 1) // m * m


def _largest_tile(total, unit, cap):
    best = unit
    for k in range(1, cap // unit + 1):
        if total % (k * unit) == 0:
            best = k * unit
    return best


def _block_diag_groups(w, per_group):
    nb, r, _ = w.shape
    ng = nb // per_group
    w = w.reshape(ng, per_group, r, r)
    eye = jnp.eye(per_group, dtype=w.dtype)
    return jnp.einsum('gbij,bc->gbicj', w, eye).reshape(ng, per_group * r, per_group * r)


def _pad_rows_to(a, axis, n):
    pad = [(0, 0)] * a.ndim
    pad[axis] = (0, n - a.shape[axis])
    return jnp.pad(a, pad)


def kernel(x_prompt, x_sample, cache_k, cache_v, cache_kidx, state_ssm, state_ssm_conv, state_lru_h, state_lru_conv, page_table, meta_tokens, norm1, w_in, lru_conv_w, lru_conv_b, lru_wa, lru_ba, lru_wx, lru_bx, lru_lambda, ssm_conv_w, ssm_conv_b, ssm_dt_bias, ssm_a_log, ssm_d, ssm_norm, p_lru, p_ssm, p_attn, w_o, norm2, w_up, w_down, final_norm):
    nb, seq, d = x_prompt.shape
    db, ds, _ = x_sample.shape
    depth = w_in.shape[0]
    n_pages = page_table.shape[1]
    past = n_pages * PAGE_SIZE
    n_pool = cache_k.shape[1]
    n_kv = cache_k.shape[3]
    kw = n_kv * HEAD_DIM
    d_rnn = lru_conv_w.shape[2]
    d_xbc = ssm_conv_w.shape[2]
    d_ssm = ssm_norm.shape[1]
    n_ssm_heads = ssm_d.shape[1]
    n_heads = p_attn.shape[1] // HEAD_DIM
    n_idx = N_IDX_HEADS
    d_ff = w_up.shape[2]
    assert d_rnn == d and d_ssm == d and d_xbc == 2 * d and n_heads * HEAD_DIM == d
    assert seq % DSA_KEY_UNIT == 0 and ds <= 8 and n_ssm_heads <= LANES - DT_LANE

    pad = (-N_META) % DSA_KEY_UNIT
    tp = pad + N_META + seq
    mp = nb * tp
    ms = db * ds
    mt = _round_up(mp + ms, 2 * DENSE_TM)
    bb = 8
    assert mp % db == 0 and db % bb == 0
    ktop_p = min(TOP_K_MAX, seq // TOP_K_FRACTION)
    ktop_s = min(TOP_K_MAX, (past + ds) // TOP_K_FRACTION)
    tkc = _largest_tile(tp, DSA_KEY_UNIT, DSA_KEY_CHUNK_MAX)

    xp = jnp.concatenate([jnp.zeros((nb, pad, d), F32),
                          jnp.broadcast_to(meta_tokens[None].astype(F32), (nb, N_META, d)),
                          x_prompt], axis=1).reshape(mp, d)
    xs = jnp.swapaxes(x_sample, 0, 1).reshape(ms, d)
    x = jnp.concatenate([xp, xs, jnp.zeros((mt - mp - ms, d), F32)], axis=0)
    r = np.arange(mt)
    in_prompt = r < mp
    rp = r % tp
    valid_np = np.where(in_prompt, rp >= pad, r < mp + ms)
    pos_np = np.where(in_prompt, np.maximum(rp - pad, 0), np.where(r < mp + ms, past + (r - mp) // db, 0))
    valid = jnp.asarray(valid_np.astype(np.float32)).reshape(mt, 1)
    half = HEAD_DIM // 2
    freq = ROPE_THETA ** (-jnp.arange(half, dtype=F32) / half)
    ang = jnp.asarray(pos_np).astype(F32)[:, None] * freq[None, :]
    cos = jnp.cos(ang)
    sin = jnp.sin(ang)
    cos_t = jnp.concatenate([cos] * (LANES // half), axis=1)
    sin_t = jnp.concatenate([-sin, sin] * (LANES // HEAD_DIM), axis=1)

    splits = (d_rnn, d_rnn, d_ssm, d_xbc, n_ssm_heads, n_heads * HEAD_DIM, kw, kw, n_idx * IDX_DIM, IDX_DIM,
              n_idx, N_BRANCH * d)
    assert sum(splits) == w_in.shape[2]
    off = np.concatenate([[0], np.cumsum(splits)])
    seg = lambda w, i: w[:, off[i]:off[i + 1]]
    LRU_X, LRU_G, SSM_Z, SSM_XBC, SSM_DT, Q, K, V, QI, KIDX, WIDX, GATES = range(12)
    XBC_COL, X_COL, G_COL, Z_COL, GATE_COL = 0, 2, 3, 4, 5
    lane_pos = np.arange(LANES)
    head_lanes = (lane_pos >= DT_LANE) & (lane_pos < DT_LANE + n_ssm_heads)
    expand_np = np.zeros((LANES, d_ssm), np.float32)
    for h in range(n_ssm_heads):
        expand_np[DT_LANE + h, h * SSM_HEAD_DIM:(h + 1) * SSM_HEAD_DIM] = 1.0
    expand = jnp.asarray(expand_np).astype(BF16)

    def to_head_lanes(v):
        return jnp.zeros((1, LANES), F32).at[0, DT_LANE:DT_LANE + n_ssm_heads].set(v)

    def pages_t(cache):
        c = jnp.moveaxis(cache, 2, -1)
        return c.reshape(depth * n_pool, -1, PAGE_SIZE)

    outs_p = [[] for _ in range(7)]
    outs_s = [[] for _ in range(7)]
    for l in range(depth):
        w = w_in[l]
        w_main = jnp.concatenate([seg(w, SSM_XBC), seg(w, LRU_X), seg(w, LRU_G), seg(w, SSM_Z), seg(w, GATES)],
                                 axis=1).astype(BF16)
        tail = LANES - IDX_DIM - n_ssm_heads - n_idx
        w_attn = jnp.concatenate([seg(w, Q), seg(w, QI), seg(w, K), seg(w, V), seg(w, KIDX), seg(w, SSM_DT),
                                  seg(w, WIDX), jnp.zeros((d, tail), F32)], axis=1).astype(BF16)
        per_group = 256 // (d_rnn // N_RNN_BLOCKS)
        wg = jnp.concatenate([_block_diag_groups(lru_wa[l], per_group), _block_diag_groups(lru_wx[l], per_group)],
                             axis=2).astype(BF16)
        clam = (-LRU_C * jax.nn.softplus(-lru_lambda[l])).reshape(1, d_rnn)
        lru_args = (lru_conv_w[l], lru_conv_b[l].reshape(1, -1), wg, lru_ba[l].reshape(1, -1),
                    lru_bx[l].reshape(1, -1), clam)
        dtb = to_head_lanes(ssm_dt_bias[l])
        aneg = to_head_lanes(-jnp.exp(ssm_a_log[l]))
        dfull = jnp.repeat(ssm_d[l], SSM_HEAD_DIM).reshape(1, d_ssm)
        ssm_args = (ssm_conv_w[l], ssm_conv_b[l].reshape(1, -1), dtb, aneg, dfull, ssm_norm[l].reshape(1, -1))

        main = norm_matmul(x, norm1[l].reshape(1, d), w_main, 2 * DENSE_TM, 2048)
        q, qi, kf, vf, small, kb, vbt, kidxb = attn_proj(x, norm1[l].reshape(1, d), w_attn, cos_t, sin_t,
                                                        n_heads, n_idx, n_kv, DENSE_TM)
        ya_p, hlru_p = lru_prompt(main, X_COL, G_COL, nb, tp, pad, *lru_args)
        yb_p, hssm_p = ssd_prompt(main, XBC_COL, Z_COL, small, nb, tp, pad, *ssm_args, n_ssm_heads, N_SSM_GROUPS)
        yc_p = dsa_prompt(q, qi, small, kb, vbt, kidxb, nb, tp, pad, ktop_p, tkc)
        ya_s, hlru_s, conv_lru_s = lru_sample(main, X_COL, G_COL, mp, db, ds,
                                              jnp.swapaxes(state_lru_conv[l], 0, 1), state_lru_h[l], *lru_args)
        yb_s, hssm_s, conv_ssm_s = ssd_sample(main, XBC_COL, Z_COL, small, mp, db, ds, bb,
                                              jnp.swapaxes(state_ssm_conv[l], 0, 1), state_ssm[l], *ssm_args,
                                              expand, n_ssm_heads, N_SSM_GROUPS)

        def batch_major(a):
            lead = a.shape[:-2]
            a = a.astype(F32).reshape(lead + (ds, db, a.shape[-1]))
            a = jnp.moveaxis(a, len(lead) + 1, 0)
            return _pad_rows_to(a, a.ndim - 2, 8)

        small_s = batch_major(small[mp:mp + ms])
        q_s = jnp.transpose(q[:, mp:mp + ms].astype(F32).reshape(n_heads, ds, db, HEAD_DIM), (2, 1, 0, 3))
        yc_s = dsa_sample(page_table, q_s.reshape(db, ds * n_heads, HEAD_DIM), batch_major(qi[:, mp:mp + ms]),
                          small_s[:, :, WIDX_LANE:WIDX_LANE + n_idx], batch_major(kf[mp:mp + ms]),
                          batch_major(vf[mp:mp + ms]), small_s[:, :, :IDX_DIM],
                          pages_t(cache_k), pages_t(cache_v), pages_t(cache_kidx), l, n_pool, ds, ktop_s)
        yc_s = jnp.swapaxes(yc_s.reshape(db, ds, n_heads * HEAD_DIM), 0, 1).reshape(ms, n_heads * HEAD_DIM).astype(BF16)

        to_tail = lambda y: _pad_rows_to(y, 0, mt - mp)
        x = merge((ya_p, yb_p, yc_p), (to_tail(ya_s), to_tail(yb_s.reshape(ms, d_ssm)), to_tail(yc_s)),
                  main, GATE_COL, x, valid, p_lru[l].astype(BF16), p_ssm[l].astype(BF16),
                  p_attn[l].astype(BF16), w_o[l].astype(BF16), DENSE_TM)
        x = mlp(x, norm2[l].reshape(1, d), w_up[l].astype(BF16), w_down[l].astype(BF16), 2 * DENSE_TM, 1024)

        prm = lambda a: a[:mp].reshape((nb, tp) + a.shape[1:])
        smp = lambda a: jnp.swapaxes(a[mp:mp + ms].reshape((ds, db) + a.shape[1:]), 0, 1)
        last = CONV_W - 1
        conv_tail = lambda c0, width: jnp.stack([main[(b + 1) * tp - last:(b + 1) * tp, c0:c0 + width]
                                                 for b in range(nb)])
        rows_p = (prm(kf)[:, pad:].reshape(nb, tp - pad, n_kv, HEAD_DIM),
                  prm(vf)[:, pad:].reshape(nb, tp - pad, n_kv, HEAD_DIM),
                  prm(small)[:, pad:, :IDX_DIM],
                  jnp.swapaxes(hssm_p, 2, 3),
                  conv_tail(XBC_COL * d, d_xbc),
                  hlru_p[:, 0],
                  conv_tail(X_COL * d, d_rnn))
        rows_s = (smp(kf).reshape(db, ds, n_kv, HEAD_DIM),
                  smp(vf).reshape(db, ds, n_kv, HEAD_DIM),
                  smp(small)[:, :, :IDX_DIM],
                  hssm_s,
                  jnp.swapaxes(conv_ssm_s, 0, 1),
                  hlru_s,
                  jnp.swapaxes(conv_lru_s, 0, 1))
        for acc, val in zip(outs_p, rows_p):
            acc.append(val)
        for acc, val in zip(outs_s, rows_s):
            acc.append(val)

    y = final_rmsnorm(x, final_norm.reshape(1, d), DENSE_TM)
    y_prompt = y[:mp].reshape(nb, tp, d)[:, pad + N_META:]
    y_sample = jnp.swapaxes(y[mp:mp + ms].reshape(ds, db, d), 0, 1)
    return (y_prompt, y_sample) + tuple(jnp.stack(a) for a in outs_p) + tuple(jnp.stack(a) for a in outs_s)
```

```python
import functools
import math

import numpy as np
import jax
import jax.numpy as jnp
from jax import lax
from jax.experimental import pallas as pl
from jax.experimental.pallas import tpu as pltpu

F32 = jnp.float32
BF16 = jnp.bfloat16
I32 = jnp.int32

N_META = 16
CONV_W = 4
EPS = 1e-6
LRU_C = 8.0
N_RNN_BLOCKS = 16
SSM_HEAD_DIM = 64
N_SSM_GROUPS = 4
SSM_STATE = 128
SSM_CHUNK = 128
HEAD_DIM = 64
N_KV_HEADS = 4
N_IDX_HEADS = 8
IDX_DIM = 64
TOP_K_MAX = 256
TOP_K_FRACTION = 4
ROPE_THETA = 10000.0
PAGE_SIZE = 128
N_BRANCH = 3

LANES = 128
ROW_TILE = 128
MXU_DIM = 256
DSA_KEY_UNIT = MXU_DIM
DSA_KEY_CHUNK_MAX = 3 * MXU_DIM
DENSE_TM = 512
VMEM_LIMIT = 56 * 1024 * 1024
INT_MIN = -(2 ** 31)
NEG_BIG = -1e30


def _cparams(sem):
    return pltpu.CompilerParams(dimension_semantics=sem, vmem_limit_bytes=VMEM_LIMIT)


def _rmsnorm_rows(x, g):
    return x * lax.rsqrt(jnp.mean(x * x, axis=-1, keepdims=True) + EPS) * g


def _norm_matmul_kernel(x_ref, g_ref, w_ref, o_ref, h_scr):
    @pl.when(pl.program_id(1) == 0)
    def _():
        h_scr[...] = _rmsnorm_rows(x_ref[...], g_ref[...]).astype(BF16)

    o_ref[...] = jnp.dot(h_scr[...], w_ref[...], preferred_element_type=F32)


def norm_matmul(x, g, w, tm, tn):
    m, d = x.shape
    n = w.shape[1]
    return pl.pallas_call(
        _norm_matmul_kernel,
        grid=(m // tm, n // tn),
        in_specs=[pl.BlockSpec((tm, d), lambda i, j: (i, 0)),
                  pl.BlockSpec((1, d), lambda i, j: (0, 0)),
                  pl.BlockSpec((d, tn), lambda i, j: (0, j))],
        out_specs=pl.BlockSpec((tm, tn), lambda i, j: (i, j)),
        out_shape=jax.ShapeDtypeStruct((m, n), F32),
        scratch_shapes=[pltpu.VMEM((tm, d), BF16)],
        compiler_params=_cparams(("parallel", "arbitrary")),
        name="inproj_main",
    )(x, g, w)


def _attn_proj_kernel(x_ref, g_ref, w_ref, cos_ref, sin_ref,
                      q_ref, qi_ref, k_ref, v_ref, small_ref, kb_ref, vbt_ref, kidxb_ref,
                      *, n_heads, n_idx, n_kv):
    h = _rmsnorm_rows(x_ref[...], g_ref[...]).astype(BF16)
    acc = jnp.dot(h, w_ref[...], preferred_element_type=F32)
    cos = cos_ref[...]
    sin = sin_ref[...]
    lane = lax.broadcasted_iota(I32, cos.shape, 1)
    lo_half = (lane & (HEAD_DIM // 2)) == 0

    def rope(xb, c, s):
        partner = jnp.where(lo_half, pltpu.roll(xb, LANES - HEAD_DIM // 2, 1), pltpu.roll(xb, HEAD_DIM // 2, 1))
        return xb * c + partner * s

    col = 0
    scale = HEAD_DIM ** -0.5 * math.log2(math.e)
    for p in range(n_heads // 2):
        blk = rope(acc[:, col:col + LANES], cos, sin) * scale
        q_ref[2 * p] = blk[:, :HEAD_DIM].astype(BF16)
        q_ref[2 * p + 1] = blk[:, HEAD_DIM:].astype(BF16)
        col += LANES
    for p in range(n_idx // 2):
        blk = rope(acc[:, col:col + LANES], cos, sin)
        qi_ref[2 * p] = blk[:, :IDX_DIM].astype(BF16)
        qi_ref[2 * p + 1] = blk[:, IDX_DIM:].astype(BF16)
        col += LANES
    kw = n_kv * HEAD_DIM
    first = lane < HEAD_DIM
    ones_lane = jnp.where(lane == HEAD_DIM, 1.0, 0.0)
    for p in range(kw // LANES):
        blk = rope(acc[:, col:col + LANES], cos, sin)
        k_ref[:, p * LANES:(p + 1) * LANES] = blk
        kb_ref[2 * p] = jnp.where(first, blk, 0.0).astype(BF16)
        kb_ref[2 * p + 1] = jnp.where(first, pltpu.roll(blk, HEAD_DIM, 1), 0.0).astype(BF16)
        col += LANES
    v_ref[...] = acc[:, col:col + kw]
    for p in range(kw // LANES):
        blk = acc[:, col:col + LANES]
        vbt_ref[2 * p] = jnp.where(first, blk, ones_lane).T.astype(BF16)
        vbt_ref[2 * p + 1] = jnp.where(first, pltpu.roll(blk, HEAD_DIM, 1), ones_lane).T.astype(BF16)
        col += LANES
    is_kidx = lane < IDX_DIM
    blk = rope(acc[:, col:col + LANES], jnp.where(is_kidx, cos, 1.0), jnp.where(is_kidx, sin, 0.0))
    small_ref[...] = blk
    kidxb_ref[...] = blk[:, :IDX_DIM].astype(BF16)


def attn_proj(x, g, w, cos, sin, n_heads, n_idx, n_kv, tm):
    m, d = x.shape
    n = w.shape[1]
    kw = n_kv * HEAD_DIM
    row = lambda i: (i, 0)
    return pl.pallas_call(
        functools.partial(_attn_proj_kernel, n_heads=n_heads, n_idx=n_idx, n_kv=n_kv),
        grid=(m // tm,),
        in_specs=[pl.BlockSpec((tm, d), row),
                  pl.BlockSpec((1, d), lambda i: (0, 0)),
                  pl.BlockSpec((d, n), lambda i: (0, 0)),
                  pl.BlockSpec((tm, LANES), row),
                  pl.BlockSpec((tm, LANES), row)],
        out_specs=[pl.BlockSpec((n_heads, tm, HEAD_DIM), lambda i: (0, i, 0)),
                   pl.BlockSpec((n_idx, tm, IDX_DIM), lambda i: (0, i, 0)),
                   pl.BlockSpec((tm, kw), row),
                   pl.BlockSpec((tm, kw), row),
                   pl.BlockSpec((tm, LANES), row),
                   pl.BlockSpec((n_kv, tm, LANES), lambda i: (0, i, 0)),
                   pl.BlockSpec((n_kv, LANES, tm), lambda i: (0, 0, i)),
                   pl.BlockSpec((tm, IDX_DIM), row)],
        out_shape=[jax.ShapeDtypeStruct((n_heads, m, HEAD_DIM), BF16),
                   jax.ShapeDtypeStruct((n_idx, m, IDX_DIM), BF16),
                   jax.ShapeDtypeStruct((m, kw), F32),
                   jax.ShapeDtypeStruct((m, kw), F32),
                   jax.ShapeDtypeStruct((m, LANES), F32),
                   jax.ShapeDtypeStruct((n_kv, m, LANES), BF16),
                   jax.ShapeDtypeStruct((n_kv, LANES, m), BF16),
                   jax.ShapeDtypeStruct((m, IDX_DIM), BF16)],
        compiler_params=_cparams(("parallel",)),
        name="inproj_attn",
    )(x, g, w, cos, sin)


def _merge_kernel(yap_ref, ybp_ref, ycp_ref, yas_ref, ybs_ref, ycs_ref, g0_ref, g1_ref, g2_ref, x_ref, valid_ref,
                  pa_ref, pb_ref, pc_ref, wo_ref, o_ref, *, n_prompt_tiles):
    in_prompt = pl.program_id(0) < n_prompt_tiles

    def branch(yp_ref, ys_ref, p_ref, g_ref):
        y = jnp.where(in_prompt, yp_ref[...], ys_ref[...])
        return jax.nn.sigmoid(g_ref[...]) * jnp.dot(y, p_ref[...], preferred_element_type=F32)

    merged = (branch(yap_ref, yas_ref, pa_ref, g0_ref) + branch(ybp_ref, ybs_ref, pb_ref, g1_ref)
              + branch(ycp_ref, ycs_ref, pc_ref, g2_ref))
    xn = x_ref[...] + jnp.dot(merged.astype(BF16), wo_ref[...], preferred_element_type=F32)
    o_ref[...] = jnp.where(valid_ref[...] > 0.0, xn, 0.0)


def merge(y_prompt, y_sample, main, gate_col, x, valid, pa, pb, pc, wo, tm):
    m, d = x.shape
    mp = y_prompt[0].shape[0]
    assert mp % tm == 0 and (m - mp) % tm == 0 and all(y.shape[0] == m - mp for y in y_sample)
    n_p = mp // tm
    row = lambda i: (i, 0)
    prow = lambda i: (jnp.minimum(i, n_p - 1), 0)
    srow = lambda i: (jnp.maximum(i - n_p, 0), 0)
    wspec = pl.BlockSpec((d, d), lambda i: (0, 0))
    return pl.pallas_call(
        functools.partial(_merge_kernel, n_prompt_tiles=n_p),
        grid=(m // tm,),
        in_specs=[pl.BlockSpec((tm, d), prow), pl.BlockSpec((tm, d), prow), pl.BlockSpec((tm, d), prow),
                  pl.BlockSpec((tm, d), srow), pl.BlockSpec((tm, d), srow), pl.BlockSpec((tm, d), srow),
                  pl.BlockSpec((tm, d), lambda i: (i, gate_col)),
                  pl.BlockSpec((tm, d), lambda i: (i, gate_col + 1)),
                  pl.BlockSpec((tm, d), lambda i: (i, gate_col + 2)),
                  pl.BlockSpec((tm, d), row),
                  pl.BlockSpec((tm, 1), row),
                  wspec, wspec, wspec, wspec],
        out_specs=pl.BlockSpec((tm, d), row),
        out_shape=jax.ShapeDtypeStruct((m, d), F32),
        compiler_params=_cparams(("parallel",)),
        name="merge",
    )(*y_prompt, *y_sample, main, main, main, x, valid, pa, pb, pc, wo)


def _mlp_kernel(x_ref, g_ref, wu_ref, wd_ref, o_ref, h_scr, acc_scr):
    j = pl.program_id(1)

    @pl.when(j == 0)
    def _():
        h_scr[...] = _rmsnorm_rows(x_ref[...], g_ref[...]).astype(BF16)
        acc_scr[...] = x_ref[...]

    u = jnp.dot(h_scr[...], wu_ref[...], preferred_element_type=F32)
    u = jnp.square(jnp.maximum(u, 0.0)).astype(BF16)
    acc_scr[...] += jnp.dot(u, wd_ref[...], preferred_element_type=F32)

    @pl.when(j == pl.num_programs(1) - 1)
    def _():
        o_ref[...] = acc_scr[...]


def mlp(x, g, wu, wd, tm, tf):
    m, d = x.shape
    f = wu.shape[1]
    return pl.pallas_call(
        _mlp_kernel,
        grid=(m // tm, f // tf),
        in_specs=[pl.BlockSpec((tm, d), lambda i, j: (i, 0)),
                  pl.BlockSpec((1, d), lambda i, j: (0, 0)),
                  pl.BlockSpec((d, tf), lambda i, j: (0, j)),
                  pl.BlockSpec((tf, d), lambda i, j: (j, 0))],
        out_specs=pl.BlockSpec((tm, d), lambda i, j: (i, 0)),
        out_shape=jax.ShapeDtypeStruct((m, d), F32),
        scratch_shapes=[pltpu.VMEM((tm, d), BF16), pltpu.VMEM((tm, d), F32)],
        compiler_params=_cparams(("parallel", "arbitrary")),
        name="mlp",
    )(x, g, wu, wd)


def _final_norm_kernel(x_ref, g_ref, o_ref):
    o_ref[...] = _rmsnorm_rows(x_ref[...], g_ref[...])


def final_rmsnorm(x, g, tm):
    m, d = x.shape
    return pl.pallas_call(
        _final_norm_kernel,
        grid=(m // tm,),
        in_specs=[pl.BlockSpec((tm, d), lambda i: (i, 0)), pl.BlockSpec((1, d), lambda i: (0, 0))],
        out_specs=pl.BlockSpec((tm, d), lambda i: (i, 0)),
        out_shape=jax.ShapeDtypeStruct((m, d), F32),
        compiler_params=_cparams(("parallel",)),
        name="final_norm",
    )(x, g)


def _softplus(x):
    return jnp.maximum(x, 0.0) + jnp.log(1.0 + jnp.exp(-jnp.abs(x)))


def _silu(x):
    return x * jax.nn.sigmoid(x)


def _gelu_tanh(x):
    return 0.5 * x * (1.0 + jnp.tanh(math.sqrt(2.0 / math.pi) * (x + 0.044715 * (x * x * x))))


def _causal_conv_rows(ext_ref, x, w_ref, b_ref, rows):
    ext_ref[8:8 + rows, :] = x
    out = b_ref[...] + w_ref[CONV_W - 1:CONV_W, :] * x
    for j in range(CONV_W - 1):
        out = out + w_ref[j:j + 1, :] * ext_ref[5 + j:5 + j + rows, :]
    ext_ref[5:8, :] = ext_ref[5 + rows:8 + rows, :]
    return out


def _lru_gates(u, wg_ref, ba_ref, bx_ref, clam_ref, n_grp):
    gw = u.shape[1] // n_grp
    rs, is_ = [], []
    for g in range(n_grp):
        rg = jnp.dot(u[:, g * gw:(g + 1) * gw].astype(BF16), wg_ref[g], preferred_element_type=F32)
        rs.append(rg[:, :gw])
        is_.append(rg[:, gw:])
    r = jax.nn.sigmoid(jnp.concatenate(rs, axis=1) + ba_ref[...])
    i = jax.nn.sigmoid(jnp.concatenate(is_, axis=1) + bx_ref[...])
    log_a = clam_ref[...] * r
    a = jnp.exp(log_a)
    b = jnp.sqrt(1.0 - jnp.exp(2.0 * log_a)) * (i * u)
    return a, b


def _lru_prompt_kernel(x_ref, g_ref, cw_ref, cb_ref, wg_ref, ba_ref, bx_ref, clam_ref,
                       y_ref, hout_ref, ext_scr, h_scr, *, pad, n_grp):
    c = pl.program_id(1)
    rows = x_ref.shape[0]

    @pl.when(c == 0)
    def _():
        ext_scr[0:8, :] = jnp.zeros((8, ext_scr.shape[1]), F32)
        h_scr[...] = jnp.zeros(h_scr.shape, F32)

    u = _causal_conv_rows(ext_scr, x_ref[...], cw_ref, cb_ref, rows)
    a, b = _lru_gates(u, wg_ref, ba_ref, bx_ref, clam_ref, n_grp)
    ridx = lax.broadcasted_iota(I32, (rows, 1), 0)
    b = jnp.where(c * rows + ridx >= pad, b, 0.0)
    s = 1
    while s < rows:
        keep = ridx >= s
        a_sh = jnp.where(keep, pltpu.roll(a, s, 0), 1.0)
        b_sh = jnp.where(keep, pltpu.roll(b, s, 0), 0.0)
        b = a * b_sh + b
        a = a * a_sh
        s *= 2
    h = a * h_scr[...] + b
    h_scr[...] = h[rows - 1:rows, :]
    hout_ref[0] = h[rows - 1:rows, :]
    y_ref[...] = (h * _gelu_tanh(g_ref[...])).astype(BF16)


def lru_prompt(main, xcol, gcol, nb, tp, pad, cw, cb, wg, ba, bx, clam):
    d = cw.shape[1]
    n_grp = wg.shape[0]
    nt = tp // ROW_TILE
    const2 = lambda b, c: (0, 0)
    return pl.pallas_call(
        functools.partial(_lru_prompt_kernel, pad=pad, n_grp=n_grp),
        grid=(nb, nt),
        in_specs=[pl.BlockSpec((ROW_TILE, d), lambda b, c: (b * nt + c, xcol)),
                  pl.BlockSpec((ROW_TILE, d), lambda b, c: (b * nt + c, gcol)),
                  pl.BlockSpec((CONV_W, d), const2),
                  pl.BlockSpec((1, d), const2),
                  pl.BlockSpec(wg.shape, lambda b, c: (0, 0, 0)),
                  pl.BlockSpec((1, d), const2),
                  pl.BlockSpec((1, d), const2),
                  pl.BlockSpec((1, d), const2)],
        out_specs=[pl.BlockSpec((ROW_TILE, d), lambda b, c: (b * nt + c, 0)),
                   pl.BlockSpec((1, 1, d), lambda b, c: (b, 0, 0))],
        out_shape=[jax.ShapeDtypeStruct((nb * tp, d), BF16),
                   jax.ShapeDtypeStruct((nb, 1, d), F32)],
        scratch_shapes=[pltpu.VMEM((8 + ROW_TILE, d), F32), pltpu.VMEM((1, d), F32)],
        compiler_params=_cparams(("parallel", "arbitrary")),
        name="lru_prompt",
    )(main, main, cw, cb, wg, ba, bx, clam)


DT_LANE = IDX_DIM


def _split3_bf16(x):
    hi = x.astype(BF16)
    r1 = x - hi.astype(F32)
    mid = r1.astype(BF16)
    lo = (r1 - mid.astype(F32)).astype(BF16)
    return hi, mid, lo


def _ssd_prompt_kernel(xbc_ref, z_ref, small_ref, cw_ref, cb_ref, dtb_ref, aneg_ref, dfull_ref, ng_ref,
                       y_ref, hout_ref, ext_scr, ht_scr, y_scr, *, pad, n_heads, n_grp):
    c = pl.program_id(1)
    rows = xbc_ref.shape[0]
    d_ssm = z_ref.shape[1]
    hpg = n_heads // n_grp
    P = SSM_HEAD_DIM
    N = SSM_STATE

    @pl.when(c == 0)
    def _():
        ext_scr[0:8, :] = jnp.zeros((8, ext_scr.shape[1]), F32)
        ht_scr[...] = jnp.zeros(ht_scr.shape, F32)

    xbc = _silu(_causal_conv_rows(ext_scr, xbc_ref[...], cw_ref, cb_ref, rows))
    xs = xbc[:, :d_ssm]
    ridx = lax.broadcasted_iota(I32, (rows, 1), 0)
    dt = _softplus(small_ref[...] + dtb_ref[...])
    dt = jnp.where(c * rows + ridx >= pad, dt, 0.0)
    da = dt * aneg_ref[...]
    ii = lax.broadcasted_iota(I32, (rows, rows), 0)
    jj = lax.broadcasted_iota(I32, (rows, rows), 1)
    causal = jj <= ii
    tri = jnp.where(causal, 1.0, 0.0).astype(BF16)
    cum = sum(jnp.dot(tri, part, preferred_element_type=F32) for part in _split3_bf16(da))
    cum_t = cum.T
    dt_t = dt.T
    cum_last = cum[rows - 1:rows, :]
    dec_end = jnp.exp(cum_last - cum) * dt
    ecum = jnp.exp(cum)
    chunk_dec = jnp.exp(cum_last)
    for g in range(n_grp):
        bg = xbc[:, d_ssm + g * N:d_ssm + (g + 1) * N]
        cg = xbc[:, d_ssm + n_grp * N + g * N:d_ssm + n_grp * N + (g + 1) * N]
        bg16 = bg.astype(BF16)
        cg16 = cg.astype(BF16)
        cb = lax.dot_general(cg16, bg16, (((1,), (1,)), ((), ())), preferred_element_type=F32)
        bgt16 = bg.T.astype(BF16)
        for hh in range(hpg):
            h = g * hpg + hh
            ln = DT_LANE + h
            x_h = xs[:, h * P:(h + 1) * P]
            seg = cum[:, ln:ln + 1] - cum_t[ln:ln + 1, :]
            decay = jnp.where(causal, jnp.exp(jnp.where(causal, seg, 0.0)), 0.0)
            w_intra = (cb * decay * dt_t[ln:ln + 1, :]).astype(BF16)
            y_h = jnp.dot(w_intra, x_h.astype(BF16), preferred_element_type=F32)
            ht = ht_scr[h]
            y_h = y_h + jnp.dot(cg16, ht.astype(BF16), preferred_element_type=F32) * ecum[:, ln:ln + 1]
            xw = (x_h * dec_end[:, ln:ln + 1]).astype(BF16)
            ht_new = chunk_dec[:, ln:ln + 1] * ht + jnp.dot(bgt16, xw, preferred_element_type=F32)
            ht_scr[h] = ht_new
            hout_ref[0, h] = ht_new
            y_scr[:, h * P:(h + 1) * P] = y_h
    y = y_scr[...] + dfull_ref[...] * xs
    y = y * _silu(z_ref[...])
    y_ref[...] = _rmsnorm_rows(y, ng_ref[...]).astype(BF16)


def ssd_prompt(main, xbc_col, z_col, small, nb, tp, pad, cw, cb, dtb, aneg, dfull, ng, n_heads, n_grp):
    d_xbc = cw.shape[1]
    d_ssm = ng.shape[1]
    nt = tp // ROW_TILE
    const2 = lambda b, c: (0, 0)
    return pl.pallas_call(
        functools.partial(_ssd_prompt_kernel, pad=pad, n_heads=n_heads, n_grp=n_grp),
        grid=(nb, nt),
        in_specs=[pl.BlockSpec((ROW_TILE, d_xbc), lambda b, c: (b * nt + c, xbc_col)),
                  pl.BlockSpec((ROW_TILE, d_ssm), lambda b, c: (b * nt + c, z_col)),
                  pl.BlockSpec((ROW_TILE, LANES), lambda b, c: (b * nt + c, 0)),
                  pl.BlockSpec((CONV_W, d_xbc), const2),
                  pl.BlockSpec((1, d_xbc), const2),
                  pl.BlockSpec((1, LANES), const2),
                  pl.BlockSpec((1, LANES), const2),
                  pl.BlockSpec((1, d_ssm), const2),
                  pl.BlockSpec((1, d_ssm), const2)],
        out_specs=[pl.BlockSpec((ROW_TILE, d_ssm), lambda b, c: (b * nt + c, 0)),
                   pl.BlockSpec((1, n_heads, SSM_STATE, SSM_HEAD_DIM), lambda b, c: (b, 0, 0, 0))],
        out_shape=[jax.ShapeDtypeStruct((nb * tp, d_ssm), BF16),
                   jax.ShapeDtypeStruct((nb, n_heads, SSM_STATE, SSM_HEAD_DIM), F32)],
        scratch_shapes=[pltpu.VMEM((8 + ROW_TILE, d_xbc), F32),
                        pltpu.VMEM((n_heads, SSM_STATE, SSM_HEAD_DIM), F32),
                        pltpu.VMEM((ROW_TILE, d_ssm), F32)],
        compiler_params=_cparams(("parallel", "arbitrary")),
        name="ssd_prompt",
    )(main, main, small, cw, cb, dtb, aneg, dfull, ng)


WIDX_LANE = IDX_DIM + 16


def _sortable_key(score):
    score = jnp.where(score == 0.0, 0.0, score)
    bits = pltpu.bitcast(score, I32)
    return jnp.where(bits < 0, bits ^ jnp.int32(0x7FFFFFFF), bits)


def _lane_blocks(x):
    return [x[:, u * LANES:(u + 1) * LANES] for u in range(x.shape[1] // LANES)]


def _row_total(x):
    return jnp.broadcast_to(jnp.sum(x, axis=1, keepdims=True), x.shape)


def _select_threshold(count, like, ktop, col_bits, bits_per_step=1):
    zero = jnp.zeros(like.shape, I32)
    assert 32 % bits_per_step == 0

    def t_step(it, t):
        shift = 32 - bits_per_step * (it + 1)
        passed = zero
        for j in range(1, 2 ** bits_per_step):
            cand = t + lax.shift_left(jnp.int32(j), shift)
            passed = passed + jnp.where(count(lambda keys, idx: keys >= cand) >= ktop, 1, 0)
        return t + lax.shift_left(passed, shift)

    t = lax.fori_loop(0, 32 // bits_per_step, t_step, jnp.full(like.shape, INT_MIN, I32))
    return t, _tie_cutoff(count, t, ktop, col_bits)


KEY_PREFIX_SHIFT = 23
KEY_PREFIX_MIN = -(2 ** 8)
KEY_PREFIX_BRACKET = 4


def _select_threshold_from_max(count, key_max, ktop, col_bits):
    pfx = lambda p: lax.shift_left(p, KEY_PREFIX_SHIFT)
    hi = lax.shift_right_arithmetic(key_max, KEY_PREFIX_SHIFT)
    near = jnp.maximum(hi - (KEY_PREFIX_BRACKET - 1), KEY_PREFIX_MIN)
    bracketed = count(lambda keys, idx: keys >= pfx(near)) >= ktop
    lo = jnp.where(bracketed, near, KEY_PREFIX_MIN)
    all_bracketed = jnp.min(jnp.where(bracketed, 1, 0)) > 0
    n_bisect = jnp.where(all_bracketed, int(math.log2(KEY_PREFIX_BRACKET)), 32 - KEY_PREFIX_SHIFT)

    def bisect(it, bounds):
        lo, hi = bounds
        mid = lax.shift_right_arithmetic(lo + hi + 1, 1)
        ok = count(lambda keys, idx: keys >= pfx(mid)) >= ktop
        return jnp.where(ok, mid, lo), jnp.where(ok, hi, mid - 1)

    lo, hi = lax.fori_loop(0, n_bisect, bisect, (lo, hi))

    def t_step(it, t):
        cand = t + lax.shift_left(jnp.int32(1), KEY_PREFIX_SHIFT - 1 - it)
        return jnp.where(count(lambda keys, idx: keys >= cand) >= ktop, cand, t)

    t = lax.fori_loop(0, KEY_PREFIX_SHIFT, t_step, pfx(lo))
    return t, _tie_cutoff(count, t, ktop, col_bits)


def _tie_cutoff(count, t, ktop, col_bits):
    zero = jnp.zeros(t.shape, I32)
    n_ge = count(lambda keys, idx: keys >= t)
    n_gt = count(lambda keys, idx: keys > t)
    live = t != INT_MIN
    excess = live & (n_ge > ktop)
    need = ktop - n_gt

    def j_search():
        def j_step(it, j):
            cand = j + lax.shift_left(jnp.int32(1), col_bits - 1 - it)
            n_tie = count(lambda keys, idx: (keys == t) & (idx < cand))
            return jnp.where(n_tie <= need, cand, j)
        return lax.fori_loop(0, col_bits, j_step, zero)

    any_excess = jnp.max(jnp.where(excess, 1, 0)) > 0
    j_cut = lax.cond(any_excess, j_search, lambda: zero)
    big = jnp.int32(2 ** 30)
    return jnp.where(excess, j_cut, jnp.where(live, big, 0))


def _selected(keys, idx, t, j_cut):
    return (keys > t) | ((keys == t) & (idx < j_cut))


def _index_scores(s, w):
    rows, n_h = w.shape
    s = jnp.maximum(s, 0.0)
    acc = w[:, 0:1] * s[0:rows]
    for h in range(1, n_h):
        acc = acc + w[:, h:h + 1] * s[h * rows:(h + 1) * rows]
    return acc


def _dsa_prompt_kernel(q_ref, qi_ref, small_ref, kb_ref, vbt_ref, kidx_ref, o_ref,
                       key_scr, qaug_scr, m_scr, acc_scr, *, ktop, pad, tkc, col_bits):
    i = pl.program_id(1)
    n_q, tq, hd = q_ref.shape
    n_idx = qi_ref.shape[0]
    n_kv = kb_ref.shape[0]
    qpk = n_q // n_kv
    r4 = qpk * tq
    q0 = i * tq
    n_chunks = (q0 + tq + tkc - 1) // tkc
    nt_dims = (((1,), (1,)), ((), ()))

    wt = small_ref[...].T[WIDX_LANE:WIDX_LANE + n_idx, :] * ((n_idx * IDX_DIM) ** -0.5)
    qi_all = qi_ref[...].reshape(n_idx * tq, qi_ref.shape[2])
    qcol = q0 + lax.broadcasted_iota(I32, (tkc, tq), 1)
    krow = lax.broadcasted_iota(I32, (tkc, tq), 0)

    def idx_products(c):
        k0 = pl.multiple_of(c * tkc, tkc)
        return lax.dot_general(kidx_ref[pl.ds(k0, tkc), :], qi_all, nt_dims, preferred_element_type=F32)

    def store_keys(c, s):
        acc = wt[0:1] * jnp.maximum(s[:, 0:tq], 0.0)
        for h in range(1, n_idx):
            acc = acc + wt[h:h + 1] * jnp.maximum(s[:, h * tq:(h + 1) * tq], 0.0)
        kidx = krow + c * tkc
        admissible = (kidx >= pad) & (kidx <= qcol)
        key_scr[c] = jnp.where(admissible, _sortable_key(acc), INT_MIN)

    def score_pair(i2, carry):
        ca = 2 * i2
        cb = jnp.minimum(ca + 1, n_chunks - 1)
        sa = idx_products(ca)
        sb = idx_products(cb)
        store_keys(ca, sa)
        store_keys(cb, sb)
        return carry

    lax.fori_loop(0, (n_chunks + 1) // 2, score_pair, 0)

    fold = 8 * 8

    def count(pred):
        def body(c, cnt):
            hit = jnp.where(pred(key_scr[c], krow + c * tkc), 1, 0)
            return cnt + jnp.sum(hit.reshape(tkc // fold, fold, tq), axis=0)
        cnt = lax.fori_loop(0, n_chunks, body, jnp.zeros((fold, tq), I32))
        return jnp.sum(cnt, axis=0, keepdims=True)

    def chunk_max(c, mx):
        return jnp.maximum(mx, jnp.max(key_scr[c].reshape(tkc // fold, fold, tq), axis=0))

    key_max = jnp.max(lax.fori_loop(0, n_chunks, chunk_max, jnp.full((fold, tq), INT_MIN, I32)),
                      axis=0, keepdims=True)
    t, j_cut = _select_threshold_from_max(count, key_max, ktop, col_bits)

    rr = lax.broadcasted_iota(I32, (tq, tq), 0)
    cc = lax.broadcasted_iota(I32, (tq, tq), 1)
    onehot = jnp.where(rr == cc, 1.0, 0.0).astype(BF16)
    zeros_tail = jnp.zeros((tq, qaug_scr.shape[1] - tq - hd), BF16)
    for h in range(n_q):
        qaug_scr[h * tq:(h + 1) * tq, :] = jnp.concatenate([onehot, q_ref[h], zeros_tail], axis=1)
    m_scr[...] = jnp.full(m_scr.shape, NEG_BIG, F32)
    acc_scr[...] = jnp.zeros(acc_scr.shape, F32)

    def mask_bias(c):
        idx = krow + pl.multiple_of(c * tkc, tkc)
        return jnp.where(_selected(key_scr[c], idx, t, j_cut), 0.0, NEG_BIG).astype(BF16)

    def qk(g, c, bias_t):
        k0 = pl.multiple_of(c * tkc, tkc)
        lhs = jnp.concatenate([bias_t, kb_ref[g, pl.ds(k0, tkc), :]], axis=1)
        return lax.dot_general(lhs, qaug_scr[g * r4:(g + 1) * r4, :], nt_dims,
                               preferred_element_type=F32)

    def attend(c, carry):
        k0 = pl.multiple_of(c * tkc, tkc)
        bias_t = mask_bias(c)

        def softmax(g, s):
            m_prev = m_scr[g * 8:g * 8 + 1, :]
            m_new = jnp.maximum(m_prev, jnp.max(s, axis=0, keepdims=True))
            m_scr[g * 8:g * 8 + 1, :] = m_new
            return jnp.exp2(s - m_new).astype(BF16), jnp.exp2(m_prev - m_new)

        def pv(g, p, alpha):
            gs = slice(g * LANES, (g + 1) * LANES)
            acc_scr[gs, :] = alpha * acc_scr[gs, :] + jnp.dot(vbt_ref[g, :, pl.ds(k0, tkc)], p,
                                                               preferred_element_type=F32)

        s_q = {0: qk(0, c, bias_t)}
        p_q = {}
        for step in range(n_kv + 1):
            if step + 1 < n_kv:
                s_q[step + 1] = qk(step + 1, c, bias_t)
            if step < n_kv:
                p_q[step] = softmax(step, s_q.pop(step))
            if 1 <= step <= n_kv:
                pv(step - 1, *p_q.pop(step - 1))
        return carry

    lax.fori_loop(0, n_chunks, attend, 0)
    for pair in range(n_q // 2):
        g, hh = (2 * pair) // qpk, (2 * pair) % qpk
        cols = slice(hh * tq, (hh + 2) * tq)
        denom = acc_scr[g * LANES + hd:g * LANES + hd + 1, cols]
        denom = jnp.where(denom > 0.0, denom, 1.0)
        out_t = acc_scr[g * LANES:g * LANES + hd, cols] / denom
        both = jnp.concatenate([out_t[:, :tq], out_t[:, tq:]], axis=0).T
        o_ref[:, 2 * pair * hd:(2 * pair + 2) * hd] = both.astype(BF16)


def dsa_prompt(q, qi, small, kb, vbt, kidxb, nb, tp, pad, ktop, tkc):
    n_q, _, hd = q.shape
    n_idx = qi.shape[0]
    n_kv = kb.shape[0]
    tq = ROW_TILE
    nt = tp // tq
    r4 = (n_q // n_kv) * tq
    col_bits = max(1, int(math.ceil(math.log2(tp))))
    return pl.pallas_call(
        functools.partial(_dsa_prompt_kernel, ktop=ktop, pad=pad, tkc=tkc, col_bits=col_bits),
        grid=(nb, nt),
        in_specs=[pl.BlockSpec((n_q, tq, hd), lambda b, i: (0, b * nt + i, 0)),
                  pl.BlockSpec((n_idx, tq, IDX_DIM), lambda b, i: (0, b * nt + i, 0)),
                  pl.BlockSpec((tq, LANES), lambda b, i: (b * nt + i, 0)),
                  pl.BlockSpec((n_kv, tp, LANES), lambda b, i: (0, b, 0)),
                  pl.BlockSpec((n_kv, LANES, tp), lambda b, i: (0, 0, b)),
                  pl.BlockSpec((tp, IDX_DIM), lambda b, i: (b, 0))],
        out_specs=pl.BlockSpec((tq, n_q * hd), lambda b, i: (b * nt + i, 0)),
        out_shape=jax.ShapeDtypeStruct((nb * tp, n_q * hd), BF16),
        scratch_shapes=[pltpu.VMEM((tp // tkc, tkc, tq), I32),
                        pltpu.VMEM((n_q * tq, 2 * LANES), BF16),
                        pltpu.VMEM((n_kv * 8, r4), F32),
                        pltpu.VMEM((n_kv * LANES, r4), F32)],
        compiler_params=_cparams(("parallel", "arbitrary")),
        name="dsa_prompt",
    )(q, qi, small, kb, vbt, kidxb)


def _lru_sample_kernel(x_ref, g_ref, hist_ref, h0_ref, cw_ref, cb_ref, wg_ref, ba_ref, bx_ref, clam_ref,
                       y_ref, hout_ref, convout_ref, hist_scr, h_scr, *, n_grp):
    t = pl.program_id(0)

    @pl.when(t == 0)
    def _():
        hist_scr[...] = hist_ref[...]
        h_scr[...] = h0_ref[...]

    x = x_ref[...]
    u = cb_ref[...] + cw_ref[CONV_W - 1:CONV_W, :] * x
    for j in range(CONV_W - 1):
        u = u + cw_ref[j:j + 1, :] * hist_scr[j]
    for j in range(CONV_W - 2):
        hist_scr[j] = hist_scr[j + 1]
    hist_scr[CONV_W - 2] = x
    a, b = _lru_gates(u, wg_ref, ba_ref, bx_ref, clam_ref, n_grp)
    h = a * h_scr[...] + b
    h_scr[...] = h
    y_ref[...] = (h * _gelu_tanh(g_ref[...])).astype(BF16)
    hout_ref[...] = h
    convout_ref[...] = hist_scr[...]


def lru_sample(main, xcol, gcol, row0, db, ds, hist, h0, cw, cb, wg, ba, bx, clam):
    d = cw.shape[1]
    n_grp = wg.shape[0]
    blk0 = row0 // db
    const2 = lambda t: (0, 0)
    const3 = lambda t: (0, 0, 0)
    return pl.pallas_call(
        functools.partial(_lru_sample_kernel, n_grp=n_grp),
        grid=(ds,),
        in_specs=[pl.BlockSpec((db, d), lambda t: (blk0 + t, xcol)),
                  pl.BlockSpec((db, d), lambda t: (blk0 + t, gcol)),
                  pl.BlockSpec((CONV_W - 1, db, d), const3),
                  pl.BlockSpec((db, d), const2),
                  pl.BlockSpec((CONV_W, d), const2),
                  pl.BlockSpec((1, d), const2),
                  pl.BlockSpec(wg.shape, const3),
                  pl.BlockSpec((1, d), const2),
                  pl.BlockSpec((1, d), const2),
                  pl.BlockSpec((1, d), const2)],
        out_specs=[pl.BlockSpec((db, d), lambda t: (t, 0)),
                   pl.BlockSpec((db, d), const2),
                   pl.BlockSpec((CONV_W - 1, db, d), const3)],
        out_shape=[jax.ShapeDtypeStruct((ds * db, d), BF16),
                   jax.ShapeDtypeStruct((db, d), F32),
                   jax.ShapeDtypeStruct((CONV_W - 1, db, d), F32)],
        scratch_shapes=[pltpu.VMEM((CONV_W - 1, db, d), F32), pltpu.VMEM((db, d), F32)],
        compiler_params=_cparams(("arbitrary",)),
        name="lru_sample",
    )(main, main, hist, h0, cw, cb, wg, ba, bx, clam)


def _ssd_sample_kernel(*refs, ds, n_heads, n_grp):
    xbc_refs = refs[0:ds]
    z_refs = refs[ds:2 * ds]
    small_refs = refs[2 * ds:3 * ds]
    (hist_ref, h0_ref, cw_ref, cb_ref, dtb_ref, aneg_ref, dfull_ref, ng_ref, exp_ref,
     y_ref, hout_ref, convout_ref, yint_scr) = refs[3 * ds:]
    bb = xbc_refs[0].shape[0]
    d_ssm = z_refs[0].shape[1]
    hpg = n_heads // n_grp
    P = SSM_HEAD_DIM
    N = SSM_STATE
    gw = hpg * P

    ext = [hist_ref[j] for j in range(CONV_W - 1)] + [r[...] for r in xbc_refs]
    for j in range(CONV_W - 1):
        convout_ref[j] = ext[ds + j]
    xbc = []
    for t in range(ds):
        u = cb_ref[...]
        for j in range(CONV_W):
            u = u + cw_ref[j:j + 1, :] * ext[t + j]
        xbc.append(_silu(u))
    xs = [v[:, :d_ssm] for v in xbc]
    bm = [v[:, d_ssm:d_ssm + n_grp * N] for v in xbc]
    cm = [v[:, d_ssm + n_grp * N:] for v in xbc]
    dt = [_softplus(r[...] + dtb_ref[...]) for r in small_refs]
    cum = []
    run = jnp.zeros_like(dt[0])
    for t in range(ds):
        run = run + dt[t] * aneg_ref[...]
        cum.append(run)
    lane = lax.broadcasted_iota(I32, (bb, LANES), 1)
    coefs = []
    for t in range(ds):
        for j in range(t + 1):
            cbh = jnp.zeros((bb, LANES), F32)
            for g in range(n_grp):
                dotg = jnp.sum(cm[t][:, g * N:(g + 1) * N] * bm[j][:, g * N:(g + 1) * N], axis=1, keepdims=True)
                in_g = (lane >= DT_LANE + g * hpg) & (lane < DT_LANE + (g + 1) * hpg)
                cbh = cbh + jnp.where(in_g, dotg, 0.0)
            coefs.append(cbh * jnp.exp(cum[t] - cum[j]) * dt[j])
    n_intra = len(coefs)
    coefs += [jnp.exp(c) for c in cum]
    coefs += [jnp.exp(cum[ds - 1] - cum[j]) * dt[j] for j in range(ds)]
    coefs.append(jnp.exp(cum[ds - 1]))
    stack = jnp.concatenate(coefs, axis=0)
    wide = sum(jnp.dot(part, exp_ref[...], preferred_element_type=F32) for part in _split3_bf16(stack))
    wide = [wide[k * bb:(k + 1) * bb] for k in range(len(coefs))]
    intra_w = wide[:n_intra]
    ecum_w = wide[n_intra:n_intra + ds]
    dend_w = wide[n_intra + ds:n_intra + 2 * ds]
    cdec_w = wide[n_intra + 2 * ds]
    xw = [dend_w[j] * xs[j] for j in range(ds)]
    zpad = 8 - ds
    for b in range(bb):
        for g in range(n_grp):
            def rows_of(arrs, lo, width):
                parts = [a[b:b + 1, lo:lo + width] for a in arrs]
                if zpad:
                    parts.append(jnp.zeros((zpad, width), F32))
                return jnp.concatenate(parts, axis=0).astype(BF16)
            hg = h0_ref[b, g * hpg:(g + 1) * hpg].reshape(gw, N)
            yint = lax.dot_general(rows_of(cm, g * N, N), hg.astype(BF16), (((1,), (1,)), ((), ())),
                                   preferred_element_type=F32)
            for t in range(ds):
                yint_scr[t, b:b + 1, g * gw:(g + 1) * gw] = yint[t:t + 1]
            upd = lax.dot_general(rows_of(xw, g * gw, gw), rows_of(bm, g * N, N), (((0,), (0,)), ((), ())),
                                  preferred_element_type=F32)
            for hh in range(hpg):
                h = g * hpg + hh
                cd = cdec_w[b:b + 1, h * P:h * P + 1]
                hout_ref[b, h] = cd * hg[hh * P:(hh + 1) * P] + upd[hh * P:(hh + 1) * P]
    k = 0
    for t in range(ds):
        y = ecum_w[t] * yint_scr[t] + dfull_ref[...] * xs[t]
        for j in range(t + 1):
            y = y + intra_w[k] * xs[j]
            k += 1
        y = y * _silu(z_refs[t][...])
        y_ref[t] = _rmsnorm_rows(y, ng_ref[...]).astype(BF16)


def ssd_sample(main, xbc_col, z_col, small, row0, db, ds, bb, hist, h0, cw, cb, dtb, aneg, dfull, ng, expand,
               n_heads, n_grp):
    d_xbc = cw.shape[1]
    d_ssm = ng.shape[1]
    const2 = lambda i: (0, 0)

    def at_t(t, width, col):
        blk0 = (row0 + t * db) // bb
        return pl.BlockSpec((bb, width), lambda i: (blk0 + i, col))

    in_specs = ([at_t(t, d_xbc, xbc_col) for t in range(ds)] + [at_t(t, d_ssm, z_col) for t in range(ds)]
                + [at_t(t, LANES, 0) for t in range(ds)]
                + [pl.BlockSpec((CONV_W - 1, bb, d_xbc), lambda i: (0, i, 0)),
                   pl.BlockSpec((bb, n_heads, SSM_HEAD_DIM, SSM_STATE), lambda i: (i, 0, 0, 0)),
                   pl.BlockSpec((CONV_W, d_xbc), const2),
                   pl.BlockSpec((1, d_xbc), const2),
                   pl.BlockSpec((1, LANES), const2),
                   pl.BlockSpec((1, LANES), const2),
                   pl.BlockSpec((1, d_ssm), const2),
                   pl.BlockSpec((1, d_ssm), const2),
                   pl.BlockSpec((LANES, d_ssm), const2)])
    return pl.pallas_call(
        functools.partial(_ssd_sample_kernel, ds=ds, n_heads=n_heads, n_grp=n_grp),
        grid=(db // bb,),
        in_specs=in_specs,
        out_specs=[pl.BlockSpec((ds, bb, d_ssm), lambda i: (0, i, 0)),
                   pl.BlockSpec((bb, n_heads, SSM_HEAD_DIM, SSM_STATE), lambda i: (i, 0, 0, 0)),
                   pl.BlockSpec((CONV_W - 1, bb, d_xbc), lambda i: (0, i, 0))],
        out_shape=[jax.ShapeDtypeStruct((ds, db, d_ssm), BF16),
                   jax.ShapeDtypeStruct((db, n_heads, SSM_HEAD_DIM, SSM_STATE), F32),
                   jax.ShapeDtypeStruct((CONV_W - 1, db, d_xbc), F32)],
        scratch_shapes=[pltpu.VMEM((ds, bb, d_ssm), F32)],
        compiler_params=_cparams(("parallel",)),
        name="ssd_sample",
    )(*([main] * ds + [main] * ds + [small] * ds + [hist, h0, cw, cb, dtb, aneg, dfull, ng, expand]))


def _dsa_sample_kernel(pt_ref, *refs, n_pages, ds, ktop):
    del pt_ref
    q_ref, qi_ref, w_ref, knew_ref, vnew_ref, kxnew_ref = refs[0:6]
    kx_pages = refs[6:6 + n_pages]
    k_pages = refs[6 + n_pages:6 + 2 * n_pages]
    v_pages = refs[6 + 2 * n_pages:6 + 3 * n_pages]
    o_ref, key_scr, logit_scr, new_scr = refs[6 + 3 * n_pages:]
    n_rows, hd = q_ref.shape[1:]
    n_q = n_rows // ds
    n_idx, rq = qi_ref.shape[1:3]
    kw = knew_ref.shape[2]
    n_kv = kw // hd
    qpk = n_q // n_kv
    n_chunks = n_pages + 1

    w = w_ref[0] * ((n_idx * IDX_DIM) ** -0.5)
    qi = qi_ref[0].reshape(n_idx * rq, qi_ref.shape[3]).astype(BF16)
    qrow = lax.broadcasted_iota(I32, (rq, PAGE_SIZE), 0)
    lane = lax.broadcasted_iota(I32, (rq, PAGE_SIZE), 1)
    nt_dims = (((1,), (1,)), ((), ()))
    for p in range(n_pages):
        sc = _index_scores(jnp.dot(qi, kx_pages[p][0].astype(BF16), preferred_element_type=F32), w)
        key_scr[0, :, p * PAGE_SIZE:(p + 1) * PAGE_SIZE] = jnp.where(qrow < ds, _sortable_key(sc), INT_MIN)
    new_scr[...] = jnp.zeros(new_scr.shape, F32)
    new_scr[0:rq, 0:kxnew_ref.shape[2]] = kxnew_ref[0]
    sc = _index_scores(lax.dot_general(qi, new_scr[:, 0:kxnew_ref.shape[2]].astype(BF16), nt_dims,
                                       preferred_element_type=F32), w)
    admissible = (lane <= qrow) & (qrow < ds)
    key_scr[0, :, n_pages * PAGE_SIZE:] = jnp.where(admissible, _sortable_key(sc), INT_MIN)
    col_bits = max(1, int(math.ceil(math.log2(n_chunks * PAGE_SIZE))))

    def count(pred):
        cnt = jnp.zeros((rq, PAGE_SIZE), I32)
        for p in range(n_chunks):
            kk = key_scr[0, :, p * PAGE_SIZE:(p + 1) * PAGE_SIZE]
            cnt = cnt + jnp.where(pred(kk, lane + p * PAGE_SIZE), 1, 0)
        return _row_total(cnt)

    t, j_cut = _select_threshold(count, lane, ktop, col_bits, bits_per_step=4)

    assert n_q & (n_q - 1) == 0 and qpk & (qpk - 1) == 0 and hd & (hd - 1) == 0
    row_group = (lax.broadcasted_iota(I32, (n_rows, kw), 0) & (n_q - 1)) >> (qpk.bit_length() - 1)
    own_lanes = (lax.broadcasted_iota(I32, (n_rows, kw), 1) >> (hd.bit_length() - 1)) == row_group
    qexp = jnp.where(own_lanes, jnp.concatenate([q_ref[0]] * n_kv, axis=1), 0.0).astype(BF16)
    new_scr[0:rq, :] = knew_ref[0]
    knew = new_scr[...].astype(BF16)
    new_scr[0:rq, :] = vnew_ref[0]
    vnew = new_scr[...].astype(BF16)
    for p in range(n_chunks):
        kk = key_scr[0, :, p * PAGE_SIZE:(p + 1) * PAGE_SIZE]
        bias = jnp.where(_selected(kk, lane + p * PAGE_SIZE, t, j_cut), 0.0, NEG_BIG)
        bias_rows = jnp.concatenate([jnp.broadcast_to(bias[ts:ts + 1], (n_q, PAGE_SIZE)) for ts in range(ds)], axis=0)
        if p < n_pages:
            qk = jnp.dot(qexp, k_pages[p][0].astype(BF16), preferred_element_type=F32)
        else:
            qk = lax.dot_general(qexp, knew, nt_dims, preferred_element_type=F32)
        logit_scr[:, p * PAGE_SIZE:(p + 1) * PAGE_SIZE] = bias_rows + qk
    logits = logit_scr[...]
    prob = jnp.exp2(logits - jnp.max(logits, axis=1, keepdims=True))
    denom = jnp.sum(prob, axis=1, keepdims=True)
    prob = prob.astype(BF16)
    acc = jnp.zeros((n_rows, kw), F32)
    for p in range(n_chunks):
        pp = prob[:, p * PAGE_SIZE:(p + 1) * PAGE_SIZE]
        if p < n_pages:
            acc = acc + lax.dot_general(pp, v_pages[p][0].astype(BF16), nt_dims, preferred_element_type=F32)
        else:
            acc = acc + jnp.dot(pp, vnew, preferred_element_type=F32)
    out = jnp.zeros((n_rows, hd), F32)
    for g in range(n_kv):
        out = out + jnp.where(row_group[:, :hd] == g, acc[:, g * hd:(g + 1) * hd], 0.0)
    o_ref[0] = out / denom


def dsa_sample(page_table, q, qi, w, knew, vnew, kxnew, cache_k, cache_v, cache_kidx, layer, n_pool, ds, ktop):
    db, n_rows, hd = q.shape
    n_idx, rq = qi.shape[1:3]
    kw = knew.shape[2]
    n_pages = page_table.shape[1]

    def page_spec(width, p):
        return pl.BlockSpec((1, width, PAGE_SIZE), lambda b, pt: (layer * n_pool + pt[b, p], 0, 0))

    own3 = lambda b, pt: (b, 0, 0)
    own4 = lambda b, pt: (b, 0, 0, 0)
    in_specs = ([pl.BlockSpec((1, n_rows, hd), own3),
                 pl.BlockSpec((1, n_idx, rq, IDX_DIM), own4),
                 pl.BlockSpec((1, rq, n_idx), own3),
                 pl.BlockSpec((1, rq, kw), own3),
                 pl.BlockSpec((1, rq, kw), own3),
                 pl.BlockSpec((1, rq, IDX_DIM), own3)]
                + [page_spec(IDX_DIM, p) for p in range(n_pages)]
                + [page_spec(kw, p) for p in range(n_pages)]
                + [page_spec(kw, p) for p in range(n_pages)])
    grid_spec = pltpu.PrefetchScalarGridSpec(
        num_scalar_prefetch=1,
        grid=(db,),
        in_specs=in_specs,
        out_specs=pl.BlockSpec((1, n_rows, hd), own3),
        scratch_shapes=[pltpu.VMEM((1, rq, (n_pages + 1) * PAGE_SIZE), I32),
                        pltpu.VMEM((n_rows, (n_pages + 1) * PAGE_SIZE), F32),
                        pltpu.VMEM((PAGE_SIZE, kw), F32)])
    return pl.pallas_call(
        functools.partial(_dsa_sample_kernel, n_pages=n_pages, ds=ds, ktop=ktop),
        grid_spec=grid_spec,
        out_shape=jax.ShapeDtypeStruct((db, n_rows, hd), F32),
        compiler_params=_cparams(("arbitrary",)),
        name="dsa_sample",
    )(page_table, q, qi, w, knew, vnew, kxnew,
      *([cache_kidx] * n_pages + [cache_k] * n_pages + [cache_v] * n_pages))


def _round_up(x, m):
    return (x + m - 1) // m * m


def _largest_tile(total, unit, cap):
    best = unit
    for k in range(1, cap // unit + 1):
        if total % (k * unit) == 0:
            best = k * unit
    return best


def _block_diag_groups(w, per_group):
    nb, r, _ = w.shape
    ng = nb // per_group
    w = w.reshape(ng, per_group, r, r)
    eye = jnp.eye(per_group, dtype=w.dtype)
    return jnp.einsum('gbij,bc->gbicj', w, eye).reshape(ng, per_group * r, per_group * r)


def _pad_rows_to(a, axis, n):
    pad = [(0, 0)] * a.ndim
    pad[axis] = (0, n - a.shape[axis])
    return jnp.pad(a, pad)


def kernel(x_prompt, x_sample, cache_k, cache_v, cache_kidx, state_ssm, state_ssm_conv, state_lru_h, state_lru_conv, page_table, meta_tokens, norm1, w_in, lru_conv_w, lru_conv_b, lru_wa, lru_ba, lru_wx, lru_bx, lru_lambda, ssm_conv_w, ssm_conv_b, ssm_dt_bias, ssm_a_log, ssm_d, ssm_norm, p_lru, p_ssm, p_attn, w_o, norm2, w_up, w_down, final_norm):
    nb, seq, d = x_prompt.shape
    db, ds, _ = x_sample.shape
    depth = w_in.shape[0]
    n_pages = page_table.shape[1]
    past = n_pages * PAGE_SIZE
    n_pool = cache_k.shape[1]
    n_kv = cache_k.shape[3]
    kw = n_kv * HEAD_DIM
    d_rnn = lru_conv_w.shape[2]
    d_xbc = ssm_conv_w.shape[2]
    d_ssm = ssm_norm.shape[1]
    n_ssm_heads = ssm_d.shape[1]
    n_heads = p_attn.shape[1] // HEAD_DIM
    n_idx = N_IDX_HEADS
    d_ff = w_up.shape[2]
    assert d_rnn == d and d_ssm == d and d_xbc == 2 * d and n_heads * HEAD_DIM == d
    assert seq % DSA_KEY_UNIT == 0 and ds <= 8 and n_ssm_heads <= LANES - DT_LANE

    pad = (-N_META) % DSA_KEY_UNIT
    tp = pad + N_META + seq
    mp = nb * tp
    ms = db * ds
    mt = _round_up(mp + ms, 2 * DENSE_TM)
    bb = 8
    assert mp % db == 0 and db % bb == 0
    ktop_p = min(TOP_K_MAX, seq // TOP_K_FRACTION)
    ktop_s = min(TOP_K_MAX, (past + ds) // TOP_K_FRACTION)
    tkc = _largest_tile(tp, DSA_KEY_UNIT, DSA_KEY_CHUNK_MAX)

    xp = jnp.concatenate([jnp.zeros((nb, pad, d), F32),
                          jnp.broadcast_to(meta_tokens[None].astype(F32), (nb, N_META, d)),
                          x_prompt], axis=1).reshape(mp, d)
    xs = jnp.swapaxes(x_sample, 0, 1).reshape(ms, d)
    x = jnp.concatenate([xp, xs, jnp.zeros((mt - mp - ms, d), F32)], axis=0)
    r = np.arange(mt)
    in_prompt = r < mp
    rp = r % tp
    valid_np = np.where(in_prompt, rp >= pad, r < mp + ms)
    pos_np = np.where(in_prompt, np.maximum(rp - pad, 0), np.where(r < mp + ms, past + (r - mp) // db, 0))
    valid = jnp.asarray(valid_np.astype(np.float32)).reshape(mt, 1)
    half = HEAD_DIM // 2
    freq = ROPE_THETA ** (-jnp.arange(half, dtype=F32) / half)
    ang = jnp.asarray(pos_np).astype(F32)[:, None] * freq[None, :]
    cos = jnp.cos(ang)
    sin = jnp.sin(ang)
    cos_t = jnp.concatenate([cos] * (LANES // half), axis=1)
    sin_t = jnp.concatenate([-sin, sin] * (LANES // HEAD_DIM), axis=1)

    splits = (d_rnn, d_rnn, d_ssm, d_xbc, n_ssm_heads, n_heads * HEAD_DIM, kw, kw, n_idx * IDX_DIM, IDX_DIM,
              n_idx, N_BRANCH * d)
    assert sum(splits) == w_in.shape[2]
    off = np.concatenate([[0], np.cumsum(splits)])
    seg = lambda w, i: w[:, off[i]:off[i + 1]]
    LRU_X, LRU_G, SSM_Z, SSM_XBC, SSM_DT, Q, K, V, QI, KIDX, WIDX, GATES = range(12)
    XBC_COL, X_COL, G_COL, Z_COL, GATE_COL = 0, 2, 3, 4, 5
    lane_pos = np.arange(LANES)
    head_lanes = (lane_pos >= DT_LANE) & (lane_pos < DT_LANE + n_ssm_heads)
    expand_np = np.zeros((LANES, d_ssm), np.float32)
    for h in range(n_ssm_heads):
        expand_np[DT_LANE + h, h * SSM_HEAD_DIM:(h + 1) * SSM_HEAD_DIM] = 1.0
    expand = jnp.asarray(expand_np).astype(BF16)

    def to_head_lanes(v):
        return jnp.zeros((1, LANES), F32).at[0, DT_LANE:DT_LANE + n_ssm_heads].set(v)

    def pages_t(cache):
        c = jnp.moveaxis(cache, 2, -1)
        return c.reshape(depth * n_pool, -1, PAGE_SIZE)

    outs_p = [[] for _ in range(7)]
    outs_s = [[] for _ in range(7)]
    for l in range(depth):
        w = w_in[l]
        w_main = jnp.concatenate([seg(w, SSM_XBC), seg(w, LRU_X), seg(w, LRU_G), seg(w, SSM_Z), seg(w, GATES)],
                                 axis=1).astype(BF16)
        tail = LANES - IDX_DIM - n_ssm_heads - n_idx
        w_attn = jnp.concatenate([seg(w, Q), seg(w, QI), seg(w, K), seg(w, V), seg(w, KIDX), seg(w, SSM_DT),
                                  seg(w, WIDX), jnp.zeros((d, tail), F32)], axis=1).astype(BF16)
        per_group = 256 // (d_rnn // N_RNN_BLOCKS)
        wg = jnp.concatenate([_block_diag_groups(lru_wa[l], per_group), _block_diag_groups(lru_wx[l], per_group)],
                             axis=2).astype(BF16)
        clam = (-LRU_C * jax.nn.softplus(-lru_lambda[l])).reshape(1, d_rnn)
        lru_args = (lru_conv_w[l], lru_conv_b[l].reshape(1, -1), wg, lru_ba[l].reshape(1, -1),
                    lru_bx[l].reshape(1, -1), clam)
        dtb = to_head_lanes(ssm_dt_bias[l])
        aneg = to_head_lanes(-jnp.exp(ssm_a_log[l]))
        dfull = jnp.repeat(ssm_d[l], SSM_HEAD_DIM).reshape(1, d_ssm)
        ssm_args = (ssm_conv_w[l], ssm_conv_b[l].reshape(1, -1), dtb, aneg, dfull, ssm_norm[l].reshape(1, -1))

        main = norm_matmul(x, norm1[l].reshape(1, d), w_main, 2 * DENSE_TM, 2048)
        q, qi, kf, vf, small, kb, vbt, kidxb = attn_proj(x, norm1[l].reshape(1, d), w_attn, cos_t, sin_t,
                                                        n_heads, n_idx, n_kv, DENSE_TM)
        ya_p, hlru_p = lru_prompt(main, X_COL, G_COL, nb, tp, pad, *lru_args)
        yb_p, hssm_p = ssd_prompt(main, XBC_COL, Z_COL, small, nb, tp, pad, *ssm_args, n_ssm_heads, N_SSM_GROUPS)
        yc_p = dsa_prompt(q, qi, small, kb, vbt, kidxb, nb, tp, pad, ktop_p, tkc)
        ya_s, hlru_s, conv_lru_s = lru_sample(main, X_COL, G_COL, mp, db, ds,
                                              jnp.swapaxes(state_lru_conv[l], 0, 1), state_lru_h[l], *lru_args)
        yb_s, hssm_s, conv_ssm_s = ssd_sample(main, XBC_COL, Z_COL, small, mp, db, ds, bb,
                                              jnp.swapaxes(state_ssm_conv[l], 0, 1), state_ssm[l], *ssm_args,
                                              expand, n_ssm_heads, N_SSM_GROUPS)

        def batch_major(a):
            lead = a.shape[:-2]
            a = a.astype(F32).reshape(lead + (ds, db, a.shape[-1]))
            a = jnp.moveaxis(a, len(lead) + 1, 0)
            return _pad_rows_to(a, a.ndim - 2, 8)

        small_s = batch_major(small[mp:mp + ms])
        q_s = jnp.transpose(q[:, mp:mp + ms].astype(F32).reshape(n_heads, ds, db, HEAD_DIM), (2, 1, 0, 3))
        yc_s = dsa_sample(page_table, q_s.reshape(db, ds * n_heads, HEAD_DIM), batch_major(qi[:, mp:mp + ms]),
                          small_s[:, :, WIDX_LANE:WIDX_LANE + n_idx], batch_major(kf[mp:mp + ms]),
                          batch_major(vf[mp:mp + ms]), small_s[:, :, :IDX_DIM],
                          pages_t(cache_k), pages_t(cache_v), pages_t(cache_kidx), l, n_pool, ds, ktop_s)
        yc_s = jnp.swapaxes(yc_s.reshape(db, ds, n_heads * HEAD_DIM), 0, 1).reshape(ms, n_heads * HEAD_DIM).astype(BF16)

        to_tail = lambda y: _pad_rows_to(y, 0, mt - mp)
        x = merge((ya_p, yb_p, yc_p), (to_tail(ya_s), to_tail(yb_s.reshape(ms, d_ssm)), to_tail(yc_s)),
                  main, GATE_COL, x, valid, p_lru[l].astype(BF16), p_ssm[l].astype(BF16),
                  p_attn[l].astype(BF16), w_o[l].astype(BF16), DENSE_TM)
        x = mlp(x, norm2[l].reshape(1, d), w_up[l].astype(BF16), w_down[l].astype(BF16), 2 * DENSE_TM, 1024)

        prm = lambda a: a[:mp].reshape((nb, tp) + a.shape[1:])
        smp = lambda a: jnp.swapaxes(a[mp:mp + ms].reshape((ds, db) + a.shape[1:]), 0, 1)
        last = CONV_W - 1
        conv_tail = lambda c0, width: jnp.stack([main[(b + 1) * tp - last:(b + 1) * tp, c0:c0 + width]
                                                 for b in range(nb)])
        rows_p = (prm(kf)[:, pad:].reshape(nb, tp - pad, n_kv, HEAD_DIM),
                  prm(vf)[:, pad:].reshape(nb, tp - pad, n_kv, HEAD_DIM),
                  prm(small)[:, pad:, :IDX_DIM],
                  jnp.swapaxes(hssm_p, 2, 3),
                  conv_tail(XBC_COL * d, d_xbc),
                  hlru_p[:, 0],
                  conv_tail(X_COL * d, d_rnn))
        rows_s = (smp(kf).reshape(db, ds, n_kv, HEAD_DIM),
                  smp(vf).reshape(db, ds, n_kv, HEAD_DIM),
                  smp(small)[:, :, :IDX_DIM],
                  hssm_s,
                  jnp.swapaxes(conv_ssm_s, 0, 1),
                  hlru_s,
                  jnp.swapaxes(conv_lru_s, 0, 1))
        for acc, val in zip(outs_p, rows_p):
            acc.append(val)
        for acc, val in zip(outs_s, rows_s):
            acc.append(val)

    y = final_rmsnorm(x, final_norm.reshape(1, d), DENSE_TM)
    y_prompt = y[:mp].reshape(nb, tp, d)[:, pad + N_META:]
    y_sample = jnp.swapaxes(y[mp:mp + ms].reshape(ds, db, d), 0, 1)
    return (y_prompt, y_sample) + tuple(jnp.stack(a) for a in outs_p) + tuple(jnp.stack(a) for a in outs_s)
```

```python
import functools
import math

import numpy as np
import jax
import jax.numpy as jnp
from jax import lax
from jax.experimental import pallas as pl
from jax.experimental.pallas import tpu as pltpu

F32 = jnp.float32
BF16 = jnp.bfloat16
I32 = jnp.int32

N_META = 16
CONV_W = 4
EPS = 1e-6
LRU_C = 8.0
N_RNN_BLOCKS = 16
SSM_HEAD_DIM = 64
N_SSM_GROUPS = 4
SSM_STATE = 128
SSM_CHUNK = 128
HEAD_DIM = 64
N_KV_HEADS = 4
N_IDX_HEADS = 8
IDX_DIM = 64
TOP_K_MAX = 256
TOP_K_FRACTION = 4
ROPE_THETA = 10000.0
PAGE_SIZE = 128
N_BRANCH = 3

LANES = 128
ROW_TILE = 128
MXU_DIM = 256
DSA_KEY_UNIT = MXU_DIM
DSA_KEY_CHUNK_MAX = 3 * MXU_DIM
VT_ROWS = HEAD_DIM + 16
DENSE_TM = 512
VMEM_LIMIT = 56 * 1024 * 1024
INT_MIN = -(2 ** 31)
NEG_BIG = -1e30


def _cparams(sem):
    return pltpu.CompilerParams(dimension_semantics=sem, vmem_limit_bytes=VMEM_LIMIT)


def _rmsnorm_rows(x, g):
    return x * lax.rsqrt(jnp.mean(x * x, axis=-1, keepdims=True) + EPS) * g


def _norm_matmul_kernel(x_ref, g_ref, w_ref, o_ref, h_scr):
    @pl.when(pl.program_id(1) == 0)
    def _():
        h_scr[...] = _rmsnorm_rows(x_ref[...], g_ref[...]).astype(BF16)

    o_ref[...] = jnp.dot(h_scr[...], w_ref[...], preferred_element_type=F32)


def norm_matmul(x, g, w, tm, tn):
    m, d = x.shape
    n = w.shape[1]
    return pl.pallas_call(
        _norm_matmul_kernel,
        grid=(m // tm, n // tn),
        in_specs=[pl.BlockSpec((tm, d), lambda i, j: (i, 0)),
                  pl.BlockSpec((1, d), lambda i, j: (0, 0)),
                  pl.BlockSpec((d, tn), lambda i, j: (0, j))],
        out_specs=pl.BlockSpec((tm, tn), lambda i, j: (i, j)),
        out_shape=jax.ShapeDtypeStruct((m, n), F32),
        scratch_shapes=[pltpu.VMEM((tm, d), BF16)],
        compiler_params=_cparams(("parallel", "arbitrary")),
        name="inproj_main",
    )(x, g, w)


def _attn_proj_kernel(x_ref, g_ref, w_ref, cos_ref, sin_ref,
                      q_ref, qi_ref, k_ref, v_ref, small_ref, kb_ref, vbt_ref, kidxb_ref,
                      *, n_heads, n_idx, n_kv):
    h = _rmsnorm_rows(x_ref[...], g_ref[...]).astype(BF16)
    acc = jnp.dot(h, w_ref[...], preferred_element_type=F32)
    cos = cos_ref[...]
    sin = sin_ref[...]
    lane = lax.broadcasted_iota(I32, cos.shape, 1)
    lo_half = (lane & (HEAD_DIM // 2)) == 0

    def rope(xb, c, s):
        partner = jnp.where(lo_half, pltpu.roll(xb, LANES - HEAD_DIM // 2, 1), pltpu.roll(xb, HEAD_DIM // 2, 1))
        return xb * c + partner * s

    col = 0
    scale = HEAD_DIM ** -0.5 * math.log2(math.e)
    for p in range(n_heads // 2):
        blk = rope(acc[:, col:col + LANES], cos, sin) * scale
        q_ref[2 * p] = blk[:, :HEAD_DIM].astype(BF16)
        q_ref[2 * p + 1] = blk[:, HEAD_DIM:].astype(BF16)
        col += LANES
    for p in range(n_idx // 2):
        blk = rope(acc[:, col:col + LANES], cos, sin)
        qi_ref[2 * p] = blk[:, :IDX_DIM].astype(BF16)
        qi_ref[2 * p + 1] = blk[:, IDX_DIM:].astype(BF16)
        col += LANES
    kw = n_kv * HEAD_DIM
    first = lane < HEAD_DIM
    ones_lane = jnp.where(lane == HEAD_DIM, 1.0, 0.0)
    for p in range(kw // LANES):
        blk = rope(acc[:, col:col + LANES], cos, sin)
        k_ref[:, p * LANES:(p + 1) * LANES] = blk
        kb_ref[2 * p] = jnp.where(first, blk, 0.0).astype(BF16)
        kb_ref[2 * p + 1] = jnp.where(first, pltpu.roll(blk, HEAD_DIM, 1), 0.0).astype(BF16)
        col += LANES
    v_ref[...] = acc[:, col:col + kw]
    for p in range(kw // LANES):
        blk = acc[:, col:col + LANES]
        vbt_ref[2 * p] = jnp.where(first, blk, ones_lane).T[:VT_ROWS].astype(BF16)
        vbt_ref[2 * p + 1] = jnp.where(first, pltpu.roll(blk, HEAD_DIM, 1), ones_lane).T[:VT_ROWS].astype(BF16)
        col += LANES
    is_kidx = lane < IDX_DIM
    blk = rope(acc[:, col:col + LANES], jnp.where(is_kidx, cos, 1.0), jnp.where(is_kidx, sin, 0.0))
    small_ref[...] = blk
    kidxb_ref[...] = blk[:, :IDX_DIM].astype(BF16)


def attn_proj(x, g, w, cos, sin, n_heads, n_idx, n_kv, tm):
    m, d = x.shape
    n = w.shape[1]
    kw = n_kv * HEAD_DIM
    row = lambda i: (i, 0)
    return pl.pallas_call(
        functools.partial(_attn_proj_kernel, n_heads=n_heads, n_idx=n_idx, n_kv=n_kv),
        grid=(m // tm,),
        in_specs=[pl.BlockSpec((tm, d), row),
                  pl.BlockSpec((1, d), lambda i: (0, 0)),
                  pl.BlockSpec((d, n), lambda i: (0, 0)),
                  pl.BlockSpec((tm, LANES), row),
                  pl.BlockSpec((tm, LANES), row)],
        out_specs=[pl.BlockSpec((n_heads, tm, HEAD_DIM), lambda i: (0, i, 0)),
                   pl.BlockSpec((n_idx, tm, IDX_DIM), lambda i: (0, i, 0)),
                   pl.BlockSpec((tm, kw), row),
                   pl.BlockSpec((tm, kw), row),
                   pl.BlockSpec((tm, LANES), row),
                   pl.BlockSpec((n_kv, tm, LANES), lambda i: (0, i, 0)),
                   pl.BlockSpec((n_kv, VT_ROWS, tm), lambda i: (0, 0, i)),
                   pl.BlockSpec((tm, IDX_DIM), row)],
        out_shape=[jax.ShapeDtypeStruct((n_heads, m, HEAD_DIM), BF16),
                   jax.ShapeDtypeStruct((n_idx, m, IDX_DIM), BF16),
                   jax.ShapeDtypeStruct((m, kw), F32),
                   jax.ShapeDtypeStruct((m, kw), F32),
                   jax.ShapeDtypeStruct((m, LANES), F32),
                   jax.ShapeDtypeStruct((n_kv, m, LANES), BF16),
                   jax.ShapeDtypeStruct((n_kv, VT_ROWS, m), BF16),
                   jax.ShapeDtypeStruct((m, IDX_DIM), BF16)],
        compiler_params=_cparams(("parallel",)),
        name="inproj_attn",
    )(x, g, w, cos, sin)


def _merge_kernel(yap_ref, ybp_ref, ycp_ref, yas_ref, ybs_ref, ycs_ref, g0_ref, g1_ref, g2_ref, x_ref, valid_ref,
                  pa_ref, pb_ref, pc_ref, wo_ref, o_ref, *, n_prompt_tiles):
    in_prompt = pl.program_id(0) < n_prompt_tiles

    def branch(yp_ref, ys_ref, p_ref, g_ref):
        y = jnp.where(in_prompt, yp_ref[...], ys_ref[...])
        return jax.nn.sigmoid(g_ref[...]) * jnp.dot(y, p_ref[...], preferred_element_type=F32)

    merged = (branch(yap_ref, yas_ref, pa_ref, g0_ref) + branch(ybp_ref, ybs_ref, pb_ref, g1_ref)
              + branch(ycp_ref, ycs_ref, pc_ref, g2_ref))
    xn = x_ref[...] + jnp.dot(merged.astype(BF16), wo_ref[...], preferred_element_type=F32)
    o_ref[...] = jnp.where(valid_ref[...] > 0.0, xn, 0.0)


def merge(y_prompt, y_sample, main, gate_col, x, valid, pa, pb, pc, wo, tm):
    m, d = x.shape
    mp = y_prompt[0].shape[0]
    assert mp % tm == 0 and (m - mp) % tm == 0 and all(y.shape[0] == m - mp for y in y_sample)
    n_p = mp // tm
    row = lambda i: (i, 0)
    prow = lambda i: (jnp.minimum(i, n_p - 1), 0)
    srow = lambda i: (jnp.maximum(i - n_p, 0), 0)
    wspec = pl.BlockSpec((d, d), lambda i: (0, 0))
    return pl.pallas_call(
        functools.partial(_merge_kernel, n_prompt_tiles=n_p),
        grid=(m // tm,),
        in_specs=[pl.BlockSpec((tm, d), prow), pl.BlockSpec((tm, d), prow), pl.BlockSpec((tm, d), prow),
                  pl.BlockSpec((tm, d), srow), pl.BlockSpec((tm, d), srow), pl.BlockSpec((tm, d), srow),
                  pl.BlockSpec((tm, d), lambda i: (i, gate_col)),
                  pl.BlockSpec((tm, d), lambda i: (i, gate_col + 1)),
                  pl.BlockSpec((tm, d), lambda i: (i, gate_col + 2)),
                  pl.BlockSpec((tm, d), row),
                  pl.BlockSpec((tm, 1), row),
                  wspec, wspec, wspec, wspec],
        out_specs=pl.BlockSpec((tm, d), row),
        out_shape=jax.ShapeDtypeStruct((m, d), F32),
        compiler_params=_cparams(("parallel",)),
        name="merge",
    )(*y_prompt, *y_sample, main, main, main, x, valid, pa, pb, pc, wo)


def _mlp_kernel(x_ref, g_ref, wu_ref, wd_ref, o_ref, h_scr, acc_scr):
    j = pl.program_id(1)

    @pl.when(j == 0)
    def _():
        h_scr[...] = _rmsnorm_rows(x_ref[...], g_ref[...]).astype(BF16)
        acc_scr[...] = x_ref[...]

    u = jnp.dot(h_scr[...], wu_ref[...], preferred_element_type=F32)
    u = jnp.square(jnp.maximum(u, 0.0)).astype(BF16)
    acc_scr[...] += jnp.dot(u, wd_ref[...], preferred_element_type=F32)

    @pl.when(j == pl.num_programs(1) - 1)
    def _():
        o_ref[...] = acc_scr[...]


def mlp(x, g, wu, wd, tm, tf):
    m, d = x.shape
    f = wu.shape[1]
    return pl.pallas_call(
        _mlp_kernel,
        grid=(m // tm, f // tf),
        in_specs=[pl.BlockSpec((tm, d), lambda i, j: (i, 0)),
                  pl.BlockSpec((1, d), lambda i, j: (0, 0)),
                  pl.BlockSpec((d, tf), lambda i, j: (0, j)),
                  pl.BlockSpec((tf, d), lambda i, j: (j, 0))],
        out_specs=pl.BlockSpec((tm, d), lambda i, j: (i, 0)),
        out_shape=jax.ShapeDtypeStruct((m, d), F32),
        scratch_shapes=[pltpu.VMEM((tm, d), BF16), pltpu.VMEM((tm, d), F32)],
        compiler_params=_cparams(("parallel", "arbitrary")),
        name="mlp",
    )(x, g, wu, wd)


def _final_norm_kernel(x_ref, g_ref, o_ref):
    o_ref[...] = _rmsnorm_rows(x_ref[...], g_ref[...])


def final_rmsnorm(x, g, tm):
    m, d = x.shape
    return pl.pallas_call(
        _final_norm_kernel,
        grid=(m // tm,),
        in_specs=[pl.BlockSpec((tm, d), lambda i: (i, 0)), pl.BlockSpec((1, d), lambda i: (0, 0))],
        out_specs=pl.BlockSpec((tm, d), lambda i: (i, 0)),
        out_shape=jax.ShapeDtypeStruct((m, d), F32),
        compiler_params=_cparams(("parallel",)),
        name="final_norm",
    )(x, g)


def _softplus(x):
    return jnp.maximum(x, 0.0) + jnp.log(1.0 + jnp.exp(-jnp.abs(x)))


def _silu(x):
    return x * jax.nn.sigmoid(x)


def _gelu_tanh(x):
    return 0.5 * x * (1.0 + jnp.tanh(math.sqrt(2.0 / math.pi) * (x + 0.044715 * (x * x * x))))


def _causal_conv_rows(ext_ref, x, w_ref, b_ref, rows):
    ext_ref[8:8 + rows, :] = x
    out = b_ref[...] + w_ref[CONV_W - 1:CONV_W, :] * x
    for j in range(CONV_W - 1):
        out = out + w_ref[j:j + 1, :] * ext_ref[5 + j:5 + j + rows, :]
    ext_ref[5:8, :] = ext_ref[5 + rows:8 + rows, :]
    return out


def _lru_gates(u, wg_ref, ba_ref, bx_ref, clam_ref, n_grp):
    gw = u.shape[1] // n_grp
    rs, is_ = [], []
    for g in range(n_grp):
        rg = jnp.dot(u[:, g * gw:(g + 1) * gw].astype(BF16), wg_ref[g], preferred_element_type=F32)
        rs.append(rg[:, :gw])
        is_.append(rg[:, gw:])
    r = jax.nn.sigmoid(jnp.concatenate(rs, axis=1) + ba_ref[...])
    i = jax.nn.sigmoid(jnp.concatenate(is_, axis=1) + bx_ref[...])
    log_a = clam_ref[...] * r
    a = jnp.exp(log_a)
    b = jnp.sqrt(1.0 - jnp.exp(2.0 * log_a)) * (i * u)
    return a, b


def _lru_prompt_kernel(x_ref, g_ref, cw_ref, cb_ref, wg_ref, ba_ref, bx_ref, clam_ref,
                       y_ref, hout_ref, ext_scr, h_scr, *, pad, n_grp):
    c = pl.program_id(1)
    rows = x_ref.shape[0]

    @pl.when(c == 0)
    def _():
        ext_scr[0:8, :] = jnp.zeros((8, ext_scr.shape[1]), F32)
        h_scr[...] = jnp.zeros(h_scr.shape, F32)

    u = _causal_conv_rows(ext_scr, x_ref[...], cw_ref, cb_ref, rows)
    a, b = _lru_gates(u, wg_ref, ba_ref, bx_ref, clam_ref, n_grp)
    ridx = lax.broadcasted_iota(I32, (rows, 1), 0)
    b = jnp.where(c * rows + ridx >= pad, b, 0.0)
    s = 1
    while s < rows:
        keep = ridx >= s
        a_sh = jnp.where(keep, pltpu.roll(a, s, 0), 1.0)
        b_sh = jnp.where(keep, pltpu.roll(b, s, 0), 0.0)
        b = a * b_sh + b
        a = a * a_sh
        s *= 2
    h = a * h_scr[...] + b
    h_scr[...] = h[rows - 1:rows, :]
    hout_ref[0] = h[rows - 1:rows, :]
    y_ref[...] = (h * _gelu_tanh(g_ref[...])).astype(BF16)


def lru_prompt(main, xcol, gcol, nb, tp, pad, cw, cb, wg, ba, bx, clam):
    d = cw.shape[1]
    n_grp = wg.shape[0]
    nt = tp // ROW_TILE
    const2 = lambda b, c: (0, 0)
    return pl.pallas_call(
        functools.partial(_lru_prompt_kernel, pad=pad, n_grp=n_grp),
        grid=(nb, nt),
        in_specs=[pl.BlockSpec((ROW_TILE, d), lambda b, c: (b * nt + c, xcol)),
                  pl.BlockSpec((ROW_TILE, d), lambda b, c: (b * nt + c, gcol)),
                  pl.BlockSpec((CONV_W, d), const2),
                  pl.BlockSpec((1, d), const2),
                  pl.BlockSpec(wg.shape, lambda b, c: (0, 0, 0)),
                  pl.BlockSpec((1, d), const2),
                  pl.BlockSpec((1, d), const2),
                  pl.BlockSpec((1, d), const2)],
        out_specs=[pl.BlockSpec((ROW_TILE, d), lambda b, c: (b * nt + c, 0)),
                   pl.BlockSpec((1, 1, d), lambda b, c: (b, 0, 0))],
        out_shape=[jax.ShapeDtypeStruct((nb * tp, d), BF16),
                   jax.ShapeDtypeStruct((nb, 1, d), F32)],
        scratch_shapes=[pltpu.VMEM((8 + ROW_TILE, d), F32), pltpu.VMEM((1, d), F32)],
        compiler_params=_cparams(("parallel", "arbitrary")),
        name="lru_prompt",
    )(main, main, cw, cb, wg, ba, bx, clam)


DT_LANE = IDX_DIM


def _split3_bf16(x):
    hi = x.astype(BF16)
    r1 = x - hi.astype(F32)
    mid = r1.astype(BF16)
    lo = (r1 - mid.astype(F32)).astype(BF16)
    return hi, mid, lo


def _ssd_prompt_kernel(xbc_ref, z_ref, small_ref, cw_ref, cb_ref, dtb_ref, aneg_ref, dfull_ref, ng_ref,
                       y_ref, hout_ref, ext_scr, ht_scr, y_scr, *, pad, n_heads, n_grp):
    c = pl.program_id(1)
    rows = xbc_ref.shape[0]
    d_ssm = z_ref.shape[1]
    hpg = n_heads // n_grp
    P = SSM_HEAD_DIM
    N = SSM_STATE

    @pl.when(c == 0)
    def _():
        ext_scr[0:8, :] = jnp.zeros((8, ext_scr.shape[1]), F32)
        ht_scr[...] = jnp.zeros(ht_scr.shape, F32)

    xbc = _silu(_causal_conv_rows(ext_scr, xbc_ref[...], cw_ref, cb_ref, rows))
    xs = xbc[:, :d_ssm]
    ridx = lax.broadcasted_iota(I32, (rows, 1), 0)
    dt = _softplus(small_ref[...] + dtb_ref[...])
    dt = jnp.where(c * rows + ridx >= pad, dt, 0.0)
    da = dt * aneg_ref[...]
    ii = lax.broadcasted_iota(I32, (rows, rows), 0)
    jj = lax.broadcasted_iota(I32, (rows, rows), 1)
    causal = jj <= ii
    tri = jnp.where(causal, 1.0, 0.0).astype(BF16)
    cum = sum(jnp.dot(tri, part, preferred_element_type=F32) for part in _split3_bf16(da))
    cum_t = cum.T
    dt_t = dt.T
    cum_last = cum[rows - 1:rows, :]
    dec_end = jnp.exp(cum_last - cum) * dt
    ecum = jnp.exp(cum)
    chunk_dec = jnp.exp(cum_last)
    for g in range(n_grp):
        bg = xbc[:, d_ssm + g * N:d_ssm + (g + 1) * N]
        cg = xbc[:, d_ssm + n_grp * N + g * N:d_ssm + n_grp * N + (g + 1) * N]
        bg16 = bg.astype(BF16)
        cg16 = cg.astype(BF16)
        cb = lax.dot_general(cg16, bg16, (((1,), (1,)), ((), ())), preferred_element_type=F32)
        bgt16 = bg.T.astype(BF16)
        for hh in range(hpg):
            h = g * hpg + hh
            ln = DT_LANE + h
            x_h = xs[:, h * P:(h + 1) * P]
            seg = cum[:, ln:ln + 1] - cum_t[ln:ln + 1, :]
            decay = jnp.where(causal, jnp.exp(jnp.where(causal, seg, 0.0)), 0.0)
            w_intra = (cb * decay * dt_t[ln:ln + 1, :]).astype(BF16)
            y_h = jnp.dot(w_intra, x_h.astype(BF16), preferred_element_type=F32)
            ht = ht_scr[h]
            y_h = y_h + jnp.dot(cg16, ht.astype(BF16), preferred_element_type=F32) * ecum[:, ln:ln + 1]
            xw = (x_h * dec_end[:, ln:ln + 1]).astype(BF16)
            ht_new = chunk_dec[:, ln:ln + 1] * ht + jnp.dot(bgt16, xw, preferred_element_type=F32)
            ht_scr[h] = ht_new
            hout_ref[0, h] = ht_new
            y_scr[:, h * P:(h + 1) * P] = y_h
    y = y_scr[...] + dfull_ref[...] * xs
    y = y * _silu(z_ref[...])
    y_ref[...] = _rmsnorm_rows(y, ng_ref[...]).astype(BF16)


def ssd_prompt(main, xbc_col, z_col, small, nb, tp, pad, cw, cb, dtb, aneg, dfull, ng, n_heads, n_grp):
    d_xbc = cw.shape[1]
    d_ssm = ng.shape[1]
    nt = tp // ROW_TILE
    const2 = lambda b, c: (0, 0)
    return pl.pallas_call(
        functools.partial(_ssd_prompt_kernel, pad=pad, n_heads=n_heads, n_grp=n_grp),
        grid=(nb, nt),
        in_specs=[pl.BlockSpec((ROW_TILE, d_xbc), lambda b, c: (b * nt + c, xbc_col)),
                  pl.BlockSpec((ROW_TILE, d_ssm), lambda b, c: (b * nt + c, z_col)),
                  pl.BlockSpec((ROW_TILE, LANES), lambda b, c: (b * nt + c, 0)),
                  pl.BlockSpec((CONV_W, d_xbc), const2),
                  pl.BlockSpec((1, d_xbc), const2),
                  pl.BlockSpec((1, LANES), const2),
                  pl.BlockSpec((1, LANES), const2),
                  pl.BlockSpec((1, d_ssm), const2),
                  pl.BlockSpec((1, d_ssm), const2)],
        out_specs=[pl.BlockSpec((ROW_TILE, d_ssm), lambda b, c: (b * nt + c, 0)),
                   pl.BlockSpec((1, n_heads, SSM_STATE, SSM_HEAD_DIM), lambda b, c: (b, 0, 0, 0))],
        out_shape=[jax.ShapeDtypeStruct((nb * tp, d_ssm), BF16),
                   jax.ShapeDtypeStruct((nb, n_heads, SSM_STATE, SSM_HEAD_DIM), F32)],
        scratch_shapes=[pltpu.VMEM((8 + ROW_TILE, d_xbc), F32),
                        pltpu.VMEM((n_heads, SSM_STATE, SSM_HEAD_DIM), F32),
                        pltpu.VMEM((ROW_TILE, d_ssm), F32)],
        compiler_params=_cparams(("parallel", "arbitrary")),
        name="ssd_prompt",
    )(main, main, small, cw, cb, dtb, aneg, dfull, ng)


WIDX_LANE = IDX_DIM + 16


def _sortable_key(score):
    score = jnp.where(score == 0.0, 0.0, score)
    bits = pltpu.bitcast(score, I32)
    return jnp.where(bits < 0, bits ^ jnp.int32(0x7FFFFFFF), bits)


def _lane_blocks(x):
    return [x[:, u * LANES:(u + 1) * LANES] for u in range(x.shape[1] // LANES)]


def _row_total(x):
    return jnp.broadcast_to(jnp.sum(x, axis=1, keepdims=True), x.shape)


def _select_threshold(count, like, ktop, col_bits, bits_per_step=1):
    zero = jnp.zeros(like.shape, I32)
    assert 32 % bits_per_step == 0

    def t_step(it, t):
        shift = 32 - bits_per_step * (it + 1)
        passed = zero
        for j in range(1, 2 ** bits_per_step):
            cand = t + lax.shift_left(jnp.int32(j), shift)
            passed = passed + jnp.where(count(lambda keys, idx: keys >= cand) >= ktop, 1, 0)
        return t + lax.shift_left(passed, shift)

    t = lax.fori_loop(0, 32 // bits_per_step, t_step, jnp.full(like.shape, INT_MIN, I32))
    return t, _tie_cutoff(count, t, ktop, col_bits)


def _tie_cutoff(count, t, ktop, col_bits):
    zero = jnp.zeros(t.shape, I32)
    n_ge = count(lambda keys, idx: keys >= t)
    n_gt = count(lambda keys, idx: keys > t)
    live = t != INT_MIN
    excess = live & (n_ge > ktop)
    need = ktop - n_gt

    def j_search():
        def j_step(it, j):
            cand = j + lax.shift_left(jnp.int32(1), col_bits - 1 - it)
            n_tie = count(lambda keys, idx: (keys == t) & (idx < cand))
            return jnp.where(n_tie <= need, cand, j)
        return lax.fori_loop(0, col_bits, j_step, zero)

    any_excess = jnp.max(jnp.where(excess, 1, 0)) > 0
    j_cut = lax.cond(any_excess, j_search, lambda: zero)
    big = jnp.int32(2 ** 30)
    return jnp.where(excess, j_cut, jnp.where(live, big, 0))


def _selected(keys, idx, t, j_cut):
    return (keys > t) | ((keys == t) & (idx < j_cut))


def _index_scores(s, w):
    rows, n_h = w.shape
    s = jnp.maximum(s, 0.0)
    acc = w[:, 0:1] * s[0:rows]
    for h in range(1, n_h):
        acc = acc + w[:, h:h + 1] * s[h * rows:(h + 1) * rows]
    return acc


def _dsa_prompt_kernel(q_ref, qi_ref, small_ref, kb_ref, vbt_ref, kidx_ref, o_ref,
                       key_scr, qaug_scr, m_scr, acc_scr, *, ktop, pad, tkc, col_bits):
    i = pl.program_id(1)
    n_q, tq, hd = q_ref.shape
    n_idx = qi_ref.shape[0]
    n_kv = kb_ref.shape[0]
    qpk = n_q // n_kv
    r4 = qpk * tq
    q0 = i * tq
    n_chunks = (q0 + tq + tkc - 1) // tkc
    nt_dims = (((1,), (1,)), ((), ()))

    wt = small_ref[...].T[WIDX_LANE:WIDX_LANE + n_idx, :] * ((n_idx * IDX_DIM) ** -0.5)
    qi_all = qi_ref[...].reshape(n_idx * tq, qi_ref.shape[2])
    qcol = q0 + lax.broadcasted_iota(I32, (tkc, tq), 1)
    krow = lax.broadcasted_iota(I32, (tkc, tq), 0)

    def idx_products(c):
        k0 = pl.multiple_of(c * tkc, tkc)
        return lax.dot_general(kidx_ref[pl.ds(k0, tkc), :], qi_all, nt_dims, preferred_element_type=F32)

    def store_keys(c, s):
        acc = wt[0:1] * jnp.maximum(s[:, 0:tq], 0.0)
        for h in range(1, n_idx):
            acc = acc + wt[h:h + 1] * jnp.maximum(s[:, h * tq:(h + 1) * tq], 0.0)
        kidx = krow + c * tkc
        admissible = (kidx >= pad) & (kidx <= qcol)
        key_scr[c] = jnp.where(admissible, _sortable_key(acc), INT_MIN)

    def score_pair(i2, carry):
        ca = 2 * i2
        cb = jnp.minimum(ca + 1, n_chunks - 1)
        sa = idx_products(ca)
        sb = idx_products(cb)
        store_keys(ca, sa)
        store_keys(cb, sb)
        return carry

    lax.fori_loop(0, (n_chunks + 1) // 2, score_pair, 0)

    fold = 8 * 8

    def count(pred):
        def body(c, cnt):
            hit = jnp.where(pred(key_scr[c], krow + c * tkc), 1, 0)
            return cnt + jnp.sum(hit.reshape(tkc // fold, fold, tq), axis=0)
        cnt = lax.fori_loop(0, n_chunks, body, jnp.zeros((fold, tq), I32))
        return jnp.sum(cnt, axis=0, keepdims=True)

    t, j_cut = _select_threshold(count, jnp.zeros((1, tq), I32), ktop, col_bits)

    rr = lax.broadcasted_iota(I32, (tq, tq), 0)
    cc = lax.broadcasted_iota(I32, (tq, tq), 1)
    onehot = jnp.where(rr == cc, 1.0, 0.0).astype(BF16)
    zeros_tail = jnp.zeros((tq, qaug_scr.shape[1] - tq - hd), BF16)
    for h in range(n_q):
        qaug_scr[h * tq:(h + 1) * tq, :] = jnp.concatenate([onehot, q_ref[h], zeros_tail], axis=1)
    m_scr[...] = jnp.full(m_scr.shape, NEG_BIG, F32)
    acc_scr[...] = jnp.zeros(acc_scr.shape, F32)

    def mask_bias(c):
        idx = krow + pl.multiple_of(c * tkc, tkc)
        return jnp.where(_selected(key_scr[c], idx, t, j_cut), 0.0, NEG_BIG).astype(BF16)

    def qk(g, c, bias_t):
        k0 = pl.multiple_of(c * tkc, tkc)
        lhs = jnp.concatenate([bias_t, kb_ref[g, pl.ds(k0, tkc), :]], axis=1)
        return lax.dot_general(lhs, qaug_scr[g * r4:(g + 1) * r4, :], nt_dims,
                               preferred_element_type=F32)

    def attend(c, carry):
        k0 = pl.multiple_of(c * tkc, tkc)
        bias_t = mask_bias(c)

        def softmax(g, s):
            m_prev = m_scr[g * 8:g * 8 + 1, :]
            m_new = jnp.maximum(m_prev, jnp.max(s, axis=0, keepdims=True))
            m_scr[g * 8:g * 8 + 1, :] = m_new
            return jnp.exp2(s - m_new).astype(BF16), jnp.exp2(m_prev - m_new)

        def pv(g, p, alpha):
            gs = slice(g * VT_ROWS, (g + 1) * VT_ROWS)
            acc_scr[gs, :] = alpha * acc_scr[gs, :] + jnp.dot(vbt_ref[g, :, pl.ds(k0, tkc)], p,
                                                               preferred_element_type=F32)

        s_q = {0: qk(0, c, bias_t)}
        p_q = {}
        for step in range(n_kv + 1):
            if step + 1 < n_kv:
                s_q[step + 1] = qk(step + 1, c, bias_t)
            if step < n_kv:
                p_q[step] = softmax(step, s_q.pop(step))
            if 1 <= step <= n_kv:
                pv(step - 1, *p_q.pop(step - 1))
        return carry

    lax.fori_loop(0, n_chunks, attend, 0)
    for pair in range(n_q // 2):
        g, hh = (2 * pair) // qpk, (2 * pair) % qpk
        cols = slice(hh * tq, (hh + 2) * tq)
        denom = acc_scr[g * VT_ROWS + hd:g * VT_ROWS + hd + 1, cols]
        denom = jnp.where(denom > 0.0, denom, 1.0)
        out_t = acc_scr[g * VT_ROWS:g * VT_ROWS + hd, cols] / denom
        both = jnp.concatenate([out_t[:, :tq], out_t[:, tq:]], axis=0).T
        o_ref[:, 2 * pair * hd:(2 * pair + 2) * hd] = both.astype(BF16)


def dsa_prompt(q, qi, small, kb, vbt, kidxb, nb, tp, pad, ktop, tkc):
    n_q, _, hd = q.shape
    n_idx = qi.shape[0]
    n_kv = kb.shape[0]
    tq = ROW_TILE
    nt = tp // tq
    r4 = (n_q // n_kv) * tq
    col_bits = max(1, int(math.ceil(math.log2(tp))))
    return pl.pallas_call(
        functools.partial(_dsa_prompt_kernel, ktop=ktop, pad=pad, tkc=tkc, col_bits=col_bits),
        grid=(nb, nt),
        in_specs=[pl.BlockSpec((n_q, tq, hd), lambda b, i: (0, b * nt + i, 0)),
                  pl.BlockSpec((n_idx, tq, IDX_DIM), lambda b, i: (0, b * nt + i, 0)),
                  pl.BlockSpec((tq, LANES), lambda b, i: (b * nt + i, 0)),
                  pl.BlockSpec((n_kv, tp, LANES), lambda b, i: (0, b, 0)),
                  pl.BlockSpec((n_kv, VT_ROWS, tp), lambda b, i: (0, 0, b)),
                  pl.BlockSpec((tp, IDX_DIM), lambda b, i: (b, 0))],
        out_specs=pl.BlockSpec((tq, n_q * hd), lambda b, i: (b * nt + i, 0)),
        out_shape=jax.ShapeDtypeStruct((nb * tp, n_q * hd), BF16),
        scratch_shapes=[pltpu.VMEM((tp // tkc, tkc, tq), I32),
                        pltpu.VMEM((n_q * tq, 2 * LANES), BF16),
                        pltpu.VMEM((n_kv * 8, r4), F32),
                        pltpu.VMEM((n_kv * VT_ROWS, r4), F32)],
        compiler_params=_cparams(("parallel", "arbitrary")),
        name="dsa_prompt",
    )(q, qi, small, kb, vbt, kidxb)


def _lru_sample_kernel(x_ref, g_ref, hist_ref, h0_ref, cw_ref, cb_ref, wg_ref, ba_ref, bx_ref, clam_ref,
                       y_ref, hout_ref, convout_ref, hist_scr, h_scr, *, n_grp):
    t = pl.program_id(0)

    @pl.when(t == 0)
    def _():
        hist_scr[...] = hist_ref[...]
        h_scr[...] = h0_ref[...]

    x = x_ref[...]
    u = cb_ref[...] + cw_ref[CONV_W - 1:CONV_W, :] * x
    for j in range(CONV_W - 1):
        u = u + cw_ref[j:j + 1, :] * hist_scr[j]
    for j in range(CONV_W - 2):
        hist_scr[j] = hist_scr[j + 1]
    hist_scr[CONV_W - 2] = x
    a, b = _lru_gates(u, wg_ref, ba_ref, bx_ref, clam_ref, n_grp)
    h = a * h_scr[...] + b
    h_scr[...] = h
    y_ref[...] = (h * _gelu_tanh(g_ref[...])).astype(BF16)
    hout_ref[...] = h
    convout_ref[...] = hist_scr[...]


def lru_sample(main, xcol, gcol, row0, db, ds, hist, h0, cw, cb, wg, ba, bx, clam):
    d = cw.shape[1]
    n_grp = wg.shape[0]
    blk0 = row0 // db
    const2 = lambda t: (0, 0)
    const3 = lambda t: (0, 0, 0)
    return pl.pallas_call(
        functools.partial(_lru_sample_kernel, n_grp=n_grp),
        grid=(ds,),
        in_specs=[pl.BlockSpec((db, d), lambda t: (blk0 + t, xcol)),
                  pl.BlockSpec((db, d), lambda t: (blk0 + t, gcol)),
                  pl.BlockSpec((CONV_W - 1, db, d), const3),
                  pl.BlockSpec((db, d), const2),
                  pl.BlockSpec((CONV_W, d), const2),
                  pl.BlockSpec((1, d), const2),
                  pl.BlockSpec(wg.shape, const3),
                  pl.BlockSpec((1, d), const2),
                  pl.BlockSpec((1, d), const2),
                  pl.BlockSpec((1, d), const2)],
        out_specs=[pl.BlockSpec((db, d), lambda t: (t, 0)),
                   pl.BlockSpec((db, d), const2),
                   pl.BlockSpec((CONV_W - 1, db, d), const3)],
        out_shape=[jax.ShapeDtypeStruct((ds * db, d), BF16),
                   jax.ShapeDtypeStruct((db, d), F32),
                   jax.ShapeDtypeStruct((CONV_W - 1, db, d), F32)],
        scratch_shapes=[pltpu.VMEM((CONV_W - 1, db, d), F32), pltpu.VMEM((db, d), F32)],
        compiler_params=_cparams(("arbitrary",)),
        name="lru_sample",
    )(main, main, hist, h0, cw, cb, wg, ba, bx, clam)


def _ssd_sample_kernel(*refs, ds, n_heads, n_grp):
    xbc_refs = refs[0:ds]
    z_refs = refs[ds:2 * ds]
    small_refs = refs[2 * ds:3 * ds]
    (hist_ref, h0_ref, cw_ref, cb_ref, dtb_ref, aneg_ref, dfull_ref, ng_ref, exp_ref,
     y_ref, hout_ref, convout_ref, yint_scr) = refs[3 * ds:]
    bb = xbc_refs[0].shape[0]
    d_ssm = z_refs[0].shape[1]
    hpg = n_heads // n_grp
    P = SSM_HEAD_DIM
    N = SSM_STATE
    gw = hpg * P

    ext = [hist_ref[j] for j in range(CONV_W - 1)] + [r[...] for r in xbc_refs]
    for j in range(CONV_W - 1):
        convout_ref[j] = ext[ds + j]
    xbc = []
    for t in range(ds):
        u = cb_ref[...]
        for j in range(CONV_W):
            u = u + cw_ref[j:j + 1, :] * ext[t + j]
        xbc.append(_silu(u))
    xs = [v[:, :d_ssm] for v in xbc]
    bm = [v[:, d_ssm:d_ssm + n_grp * N] for v in xbc]
    cm = [v[:, d_ssm + n_grp * N:] for v in xbc]
    dt = [_softplus(r[...] + dtb_ref[...]) for r in small_refs]
    cum = []
    run = jnp.zeros_like(dt[0])
    for t in range(ds):
        run = run + dt[t] * aneg_ref[...]
        cum.append(run)
    lane = lax.broadcasted_iota(I32, (bb, LANES), 1)
    coefs = []
    for t in range(ds):
        for j in range(t + 1):
            cbh = jnp.zeros((bb, LANES), F32)
            for g in range(n_grp):
                dotg = jnp.sum(cm[t][:, g * N:(g + 1) * N] * bm[j][:, g * N:(g + 1) * N], axis=1, keepdims=True)
                in_g = (lane >= DT_LANE + g * hpg) & (lane < DT_LANE + (g + 1) * hpg)
                cbh = cbh + jnp.where(in_g, dotg, 0.0)
            coefs.append(cbh * jnp.exp(cum[t] - cum[j]) * dt[j])
    n_intra = len(coefs)
    coefs += [jnp.exp(c) for c in cum]
    coefs += [jnp.exp(cum[ds - 1] - cum[j]) * dt[j] for j in range(ds)]
    coefs.append(jnp.exp(cum[ds - 1]))
    stack = jnp.concatenate(coefs, axis=0)
    wide = sum(jnp.dot(part, exp_ref[...], preferred_element_type=F32) for part in _split3_bf16(stack))
    wide = [wide[k * bb:(k + 1) * bb] for k in range(len(coefs))]
    intra_w = wide[:n_intra]
    ecum_w = wide[n_intra:n_intra + ds]
    dend_w = wide[n_intra + ds:n_intra + 2 * ds]
    cdec_w = wide[n_intra + 2 * ds]
    xw = [dend_w[j] * xs[j] for j in range(ds)]
    zpad = 8 - ds
    for b in range(bb):
        for g in range(n_grp):
            def rows_of(arrs, lo, width):
                parts = [a[b:b + 1, lo:lo + width] for a in arrs]
                if zpad:
                    parts.append(jnp.zeros((zpad, width), F32))
                return jnp.concatenate(parts, axis=0).astype(BF16)
            hg = h0_ref[b, g * hpg:(g + 1) * hpg].reshape(gw, N)
            yint = lax.dot_general(rows_of(cm, g * N, N), hg.astype(BF16), (((1,), (1,)), ((), ())),
                                   preferred_element_type=F32)
            for t in range(ds):
                yint_scr[t, b:b + 1, g * gw:(g + 1) * gw] = yint[t:t + 1]
            upd = lax.dot_general(rows_of(xw, g * gw, gw), rows_of(bm, g * N, N), (((0,), (0,)), ((), ())),
                                  preferred_element_type=F32)
            for hh in range(hpg):
                h = g * hpg + hh
                cd = cdec_w[b:b + 1, h * P:h * P + 1]
                hout_ref[b, h] = cd * hg[hh * P:(hh + 1) * P] + upd[hh * P:(hh + 1) * P]
    k = 0
    for t in range(ds):
        y = ecum_w[t] * yint_scr[t] + dfull_ref[...] * xs[t]
        for j in range(t + 1):
            y = y + intra_w[k] * xs[j]
            k += 1
        y = y * _silu(z_refs[t][...])
        y_ref[t] = _rmsnorm_rows(y, ng_ref[...]).astype(BF16)


def ssd_sample(main, xbc_col, z_col, small, row0, db, ds, bb, hist, h0, cw, cb, dtb, aneg, dfull, ng, expand,
               n_heads, n_grp):
    d_xbc = cw.shape[1]
    d_ssm = ng.shape[1]
    const2 = lambda i: (0, 0)

    def at_t(t, width, col):
        blk0 = (row0 + t * db) // bb
        return pl.BlockSpec((bb, width), lambda i: (blk0 + i, col))

    in_specs = ([at_t(t, d_xbc, xbc_col) for t in range(ds)] + [at_t(t, d_ssm, z_col) for t in range(ds)]
                + [at_t(t, LANES, 0) for t in range(ds)]
                + [pl.BlockSpec((CONV_W - 1, bb, d_xbc), lambda i: (0, i, 0)),
                   pl.BlockSpec((bb, n_heads, SSM_HEAD_DIM, SSM_STATE), lambda i: (i, 0, 0, 0)),
                   pl.BlockSpec((CONV_W, d_xbc), const2),
                   pl.BlockSpec((1, d_xbc), const2),
                   pl.BlockSpec((1, LANES), const2),
                   pl.BlockSpec((1, LANES), const2),
                   pl.BlockSpec((1, d_ssm), const2),
                   pl.BlockSpec((1, d_ssm), const2),
                   pl.BlockSpec((LANES, d_ssm), const2)])
    return pl.pallas_call(
        functools.partial(_ssd_sample_kernel, ds=ds, n_heads=n_heads, n_grp=n_grp),
        grid=(db // bb,),
        in_specs=in_specs,
        out_specs=[pl.BlockSpec((ds, bb, d_ssm), lambda i: (0, i, 0)),
                   pl.BlockSpec((bb, n_heads, SSM_HEAD_DIM, SSM_STATE), lambda i: (i, 0, 0, 0)),
                   pl.BlockSpec((CONV_W - 1, bb, d_xbc), lambda i: (0, i, 0))],
        out_shape=[jax.ShapeDtypeStruct((ds, db, d_ssm), BF16),
                   jax.ShapeDtypeStruct((db, n_heads, SSM_HEAD_DIM, SSM_STATE), F32),
                   jax.ShapeDtypeStruct((CONV_W - 1, db, d_xbc), F32)],
        scratch_shapes=[pltpu.VMEM((ds, bb, d_ssm), F32)],
        compiler_params=_cparams(("parallel",)),
        name="ssd_sample",
    )(*([main] * ds + [main] * ds + [small] * ds + [hist, h0, cw, cb, dtb, aneg, dfull, ng, expand]))


SAMPLE_BATCH_PER_STEP = 2


def _dsa_sample_kernel(pt_ref, *refs, n_pages, ds, ktop, bpb):
    del pt_ref
    q_ref, qi_ref, w_ref, knew_ref, vnew_ref, kxnew_ref = refs[0:6]
    pages = refs[6:6 + 3 * bpb * n_pages]
    page = lambda kind, j, p: pages[(kind * bpb + j) * n_pages + p]
    o_ref, key_scr, logit_scr, new_scr = refs[6 + 3 * bpb * n_pages:]
    n_rows, hd = q_ref.shape[1:]
    n_q = n_rows // ds
    n_idx, rq = qi_ref.shape[1:3]
    kw = knew_ref.shape[2]
    n_kv = kw // hd
    qpk = n_q // n_kv
    n_chunks = n_pages + 1
    nt_dims = (((1,), (1,)), ((), ()))
    assert rq & (rq - 1) == 0

    qrow = lax.broadcasted_iota(I32, (rq, PAGE_SIZE), 0)
    qlane = lax.broadcasted_iota(I32, (rq, PAGE_SIZE), 1)
    new_scr[...] = jnp.zeros(new_scr.shape, F32)
    for j in range(bpb):
        rows = slice(j * rq, (j + 1) * rq)
        w = w_ref[j] * ((n_idx * IDX_DIM) ** -0.5)
        qi = qi_ref[j].reshape(n_idx * rq, qi_ref.shape[3]).astype(BF16)
        for p in range(n_pages):
            sc = _index_scores(jnp.dot(qi, page(0, j, p)[0].astype(BF16), preferred_element_type=F32), w)
            key_scr[0, rows, p * PAGE_SIZE:(p + 1) * PAGE_SIZE] = jnp.where(qrow < ds, _sortable_key(sc), INT_MIN)
        new_scr[j, 0:rq, 0:kxnew_ref.shape[2]] = kxnew_ref[j]
        sc = _index_scores(lax.dot_general(qi, new_scr[j, :, 0:kxnew_ref.shape[2]].astype(BF16), nt_dims,
                                           preferred_element_type=F32), w)
        admissible = (qlane <= qrow) & (qrow < ds)
        key_scr[0, rows, n_pages * PAGE_SIZE:] = jnp.where(admissible, _sortable_key(sc), INT_MIN)
    col_bits = max(1, int(math.ceil(math.log2(n_chunks * PAGE_SIZE))))
    lane = lax.broadcasted_iota(I32, (bpb * rq, PAGE_SIZE), 1)

    def count(pred):
        cnt = jnp.zeros((bpb * rq, PAGE_SIZE), I32)
        for p in range(n_chunks):
            kk = key_scr[0, :, p * PAGE_SIZE:(p + 1) * PAGE_SIZE]
            cnt = cnt + jnp.where(pred(kk, lane + p * PAGE_SIZE), 1, 0)
        return _row_total(cnt)

    t_all, j_all = _select_threshold(count, lane, ktop, col_bits, bits_per_step=4)

    assert n_q & (n_q - 1) == 0 and qpk & (qpk - 1) == 0 and hd & (hd - 1) == 0
    row_group = (lax.broadcasted_iota(I32, (n_rows, kw), 0) & (n_q - 1)) >> (qpk.bit_length() - 1)
    own_lanes = (lax.broadcasted_iota(I32, (n_rows, kw), 1) >> (hd.bit_length() - 1)) == row_group
    for j in range(bpb):
        rows = slice(j * rq, (j + 1) * rq)
        t, j_cut = t_all[rows], j_all[rows]
        qexp = jnp.where(own_lanes, jnp.concatenate([q_ref[j]] * n_kv, axis=1), 0.0).astype(BF16)
        new_scr[j, 0:rq, :] = knew_ref[j]
        knew = new_scr[j].astype(BF16)
        new_scr[j, 0:rq, :] = vnew_ref[j]
        vnew = new_scr[j].astype(BF16)
        for p in range(n_chunks):
            kk = key_scr[0, rows, p * PAGE_SIZE:(p + 1) * PAGE_SIZE]
            bias = jnp.where(_selected(kk, qlane + p * PAGE_SIZE, t, j_cut), 0.0, NEG_BIG)
            bias_rows = jnp.concatenate([jnp.broadcast_to(bias[ts:ts + 1], (n_q, PAGE_SIZE)) for ts in range(ds)],
                                        axis=0)
            if p < n_pages:
                qk = jnp.dot(qexp, page(1, j, p)[0].astype(BF16), preferred_element_type=F32)
            else:
                qk = lax.dot_general(qexp, knew, nt_dims, preferred_element_type=F32)
            logit_scr[j, :, p * PAGE_SIZE:(p + 1) * PAGE_SIZE] = bias_rows + qk
        logits = logit_scr[j]
        prob = jnp.exp2(logits - jnp.max(logits, axis=1, keepdims=True))
        denom = jnp.sum(prob, axis=1, keepdims=True)
        prob = prob.astype(BF16)
        acc = jnp.zeros((n_rows, kw), F32)
        for p in range(n_chunks):
            pp = prob[:, p * PAGE_SIZE:(p + 1) * PAGE_SIZE]
            if p < n_pages:
                acc = acc + lax.dot_general(pp, page(2, j, p)[0].astype(BF16), nt_dims, preferred_element_type=F32)
            else:
                acc = acc + jnp.dot(pp, vnew, preferred_element_type=F32)
        out = jnp.zeros((n_rows, hd), F32)
        for g in range(n_kv):
            out = out + jnp.where(row_group[:, :hd] == g, acc[:, g * hd:(g + 1) * hd], 0.0)
        o_ref[j] = out / denom


def dsa_sample(page_table, q, qi, w, knew, vnew, kxnew, cache_k, cache_v, cache_kidx, layer, n_pool, ds, ktop):
    db, n_rows, hd = q.shape
    n_idx, rq = qi.shape[1:3]
    kw = knew.shape[2]
    n_pages = page_table.shape[1]
    bpb = SAMPLE_BATCH_PER_STEP
    assert db % bpb == 0

    def page_spec(width, j, p):
        return pl.BlockSpec((1, width, PAGE_SIZE), lambda b, pt: (layer * n_pool + pt[b * bpb + j, p], 0, 0))

    own3 = lambda b, pt: (b, 0, 0)
    own4 = lambda b, pt: (b, 0, 0, 0)
    in_specs = ([pl.BlockSpec((bpb, n_rows, hd), own3),
                 pl.BlockSpec((bpb, n_idx, rq, IDX_DIM), own4),
                 pl.BlockSpec((bpb, rq, n_idx), own3),
                 pl.BlockSpec((bpb, rq, kw), own3),
                 pl.BlockSpec((bpb, rq, kw), own3),
                 pl.BlockSpec((bpb, rq, IDX_DIM), own3)]
                + [page_spec(IDX_DIM, j, p) for j in range(bpb) for p in range(n_pages)]
                + [page_spec(kw, j, p) for j in range(bpb) for p in range(n_pages)]
                + [page_spec(kw, j, p) for j in range(bpb) for p in range(n_pages)])
    grid_spec = pltpu.PrefetchScalarGridSpec(
        num_scalar_prefetch=1,
        grid=(db // bpb,),
        in_specs=in_specs,
        out_specs=pl.BlockSpec((bpb, n_rows, hd), own3),
        scratch_shapes=[pltpu.VMEM((1, bpb * rq, (n_pages + 1) * PAGE_SIZE), I32),
                        pltpu.VMEM((bpb, n_rows, (n_pages + 1) * PAGE_SIZE), F32),
                        pltpu.VMEM((bpb, PAGE_SIZE, kw), F32)])
    return pl.pallas_call(
        functools.partial(_dsa_sample_kernel, n_pages=n_pages, ds=ds, ktop=ktop, bpb=bpb),
        grid_spec=grid_spec,
        out_shape=jax.ShapeDtypeStruct((db, n_rows, hd), F32),
        compiler_params=_cparams(("arbitrary",)),
        name="dsa_sample",
    )(page_table, q, qi, w, knew, vnew, kxnew,
      *([cache_kidx] * (bpb * n_pages) + [cache_k] * (bpb * n_pages) + [cache_v] * (bpb * n_pages)))


def _round_up(x, m):
    return (x + m - 1) // m * m


def _largest_tile(total, unit, cap):
    best = unit
    for k in range(1, cap // unit + 1):
        if total % (k * unit) == 0:
            best = k * unit
    return best


def _block_diag_groups(w, per_group):
    nb, r, _ = w.shape
    ng = nb // per_group
    w = w.reshape(ng, per_group, r, r)
    eye = jnp.eye(per_group, dtype=w.dtype)
    return jnp.einsum('gbij,bc->gbicj', w, eye).reshape(ng, per_group * r, per_group * r)


def _pad_rows_to(a, axis, n):
    pad = [(0, 0)] * a.ndim
    pad[axis] = (0, n - a.shape[axis])
    return jnp.pad(a, pad)


def kernel(x_prompt, x_sample, cache_k, cache_v, cache_kidx, state_ssm, state_ssm_conv, state_lru_h, state_lru_conv, page_table, meta_tokens, norm1, w_in, lru_conv_w, lru_conv_b, lru_wa, lru_ba, lru_wx, lru_bx, lru_lambda, ssm_conv_w, ssm_conv_b, ssm_dt_bias, ssm_a_log, ssm_d, ssm_norm, p_lru, p_ssm, p_attn, w_o, norm2, w_up, w_down, final_norm):
    nb, seq, d = x_prompt.shape
    db, ds, _ = x_sample.shape
    depth = w_in.shape[0]
    n_pages = page_table.shape[1]
    past = n_pages * PAGE_SIZE
    n_pool = cache_k.shape[1]
    n_kv = cache_k.shape[3]
    kw = n_kv * HEAD_DIM
    d_rnn = lru_conv_w.shape[2]
    d_xbc = ssm_conv_w.shape[2]
    d_ssm = ssm_norm.shape[1]
    n_ssm_heads = ssm_d.shape[1]
    n_heads = p_attn.shape[1] // HEAD_DIM
    n_idx = N_IDX_HEADS
    d_ff = w_up.shape[2]
    assert d_rnn == d and d_ssm == d and d_xbc == 2 * d and n_heads * HEAD_DIM == d
    assert seq % DSA_KEY_UNIT == 0 and ds <= 8 and n_ssm_heads <= LANES - DT_LANE

    pad = (-N_META) % DSA_KEY_UNIT
    tp = pad + N_META + seq
    mp = nb * tp
    ms = db * ds
    mt = _round_up(mp + ms, 2 * DENSE_TM)
    bb = 8
    assert mp % db == 0 and db % bb == 0
    ktop_p = min(TOP_K_MAX, seq // TOP_K_FRACTION)
    ktop_s = min(TOP_K_MAX, (past + ds) // TOP_K_FRACTION)
    tkc = _largest_tile(tp, DSA_KEY_UNIT, DSA_KEY_CHUNK_MAX)

    xp = jnp.concatenate([jnp.zeros((nb, pad, d), F32),
                          jnp.broadcast_to(meta_tokens[None].astype(F32), (nb, N_META, d)),
                          x_prompt], axis=1).reshape(mp, d)
    xs = jnp.swapaxes(x_sample, 0, 1).reshape(ms, d)
    x = jnp.concatenate([xp, xs, jnp.zeros((mt - mp - ms, d), F32)], axis=0)
    r = np.arange(mt)
    in_prompt = r < mp
    rp = r % tp
    valid_np = np.where(in_prompt, rp >= pad, r < mp + ms)
    pos_np = np.where(in_prompt, np.maximum(rp - pad, 0), np.where(r < mp + ms, past + (r - mp) // db, 0))
    valid = jnp.asarray(valid_np.astype(np.float32)).reshape(mt, 1)
    half = HEAD_DIM // 2
    freq = ROPE_THETA ** (-jnp.arange(half, dtype=F32) / half)
    ang = jnp.asarray(pos_np).astype(F32)[:, None] * freq[None, :]
    cos = jnp.cos(ang)
    sin = jnp.sin(ang)
    cos_t = jnp.concatenate([cos] * (LANES // half), axis=1)
    sin_t = jnp.concatenate([-sin, sin] * (LANES // HEAD_DIM), axis=1)

    splits = (d_rnn, d_rnn, d_ssm, d_xbc, n_ssm_heads, n_heads * HEAD_DIM, kw, kw, n_idx * IDX_DIM, IDX_DIM,
              n_idx, N_BRANCH * d)
    assert sum(splits) == w_in.shape[2]
    off = np.concatenate([[0], np.cumsum(splits)])
    seg = lambda w, i: w[:, off[i]:off[i + 1]]
    LRU_X, LRU_G, SSM_Z, SSM_XBC, SSM_DT, Q, K, V, QI, KIDX, WIDX, GATES = range(12)
    XBC_COL, X_COL, G_COL, Z_COL, GATE_COL = 0, 2, 3, 4, 5
    lane_pos = np.arange(LANES)
    head_lanes = (lane_pos >= DT_LANE) & (lane_pos < DT_LANE + n_ssm_heads)
    expand_np = np.zeros((LANES, d_ssm), np.float32)
    for h in range(n_ssm_heads):
        expand_np[DT_LANE + h, h * SSM_HEAD_DIM:(h + 1) * SSM_HEAD_DIM] = 1.0
    expand = jnp.asarray(expand_np).astype(BF16)

    def to_head_lanes(v):
        return jnp.zeros((1, LANES), F32).at[0, DT_LANE:DT_LANE + n_ssm_heads].set(v)

    def pages_t(cache):
        c = jnp.moveaxis(cache, 2, -1)
        return c.reshape(depth * n_pool, -1, PAGE_SIZE)

    outs_p = [[] for _ in range(7)]
    outs_s = [[] for _ in range(7)]
    for l in range(depth):
        w = w_in[l]
        w_main = jnp.concatenate([seg(w, SSM_XBC), seg(w, LRU_X), seg(w, LRU_G), seg(w, SSM_Z), seg(w, GATES)],
                                 axis=1).astype(BF16)
        tail = LANES - IDX_DIM - n_ssm_heads - n_idx
        w_attn = jnp.concatenate([seg(w, Q), seg(w, QI), seg(w, K), seg(w, V), seg(w, KIDX), seg(w, SSM_DT),
                                  seg(w, WIDX), jnp.zeros((d, tail), F32)], axis=1).astype(BF16)
        per_group = 256 // (d_rnn // N_RNN_BLOCKS)
        wg = jnp.concatenate([_block_diag_groups(lru_wa[l], per_group), _block_diag_groups(lru_wx[l], per_group)],
                             axis=2).astype(BF16)
        clam = (-LRU_C * jax.nn.softplus(-lru_lambda[l])).reshape(1, d_rnn)
        lru_args = (lru_conv_w[l], lru_conv_b[l].reshape(1, -1), wg, lru_ba[l].reshape(1, -1),
                    lru_bx[l].reshape(1, -1), clam)
        dtb = to_head_lanes(ssm_dt_bias[l])
        aneg = to_head_lanes(-jnp.exp(ssm_a_log[l]))
        dfull = jnp.repeat(ssm_d[l], SSM_HEAD_DIM).reshape(1, d_ssm)
        ssm_args = (ssm_conv_w[l], ssm_conv_b[l].reshape(1, -1), dtb, aneg, dfull, ssm_norm[l].reshape(1, -1))

        main = norm_matmul(x, norm1[l].reshape(1, d), w_main, 2 * DENSE_TM, 2048)
        q, qi, kf, vf, small, kb, vbt, kidxb = attn_proj(x, norm1[l].reshape(1, d), w_attn, cos_t, sin_t,
                                                        n_heads, n_idx, n_kv, DENSE_TM)
        ya_p, hlru_p = lru_prompt(main, X_COL, G_COL, nb, tp, pad, *lru_args)
        yb_p, hssm_p = ssd_prompt(main, XBC_COL, Z_COL, small, nb, tp, pad, *ssm_args, n_ssm_heads, N_SSM_GROUPS)
        yc_p = dsa_prompt(q, qi, small, kb, vbt, kidxb, nb, tp, pad, ktop_p, tkc)
        ya_s, hlru_s, conv_lru_s = lru_sample(main, X_COL, G_COL, mp, db, ds,
                                              jnp.swapaxes(state_lru_conv[l], 0, 1), state_lru_h[l], *lru_args)
        yb_s, hssm_s, conv_ssm_s = ssd_sample(main, XBC_COL, Z_COL, small, mp, db, ds, bb,
                                              jnp.swapaxes(state_ssm_conv[l], 0, 1), state_ssm[l], *ssm_args,
                                              expand, n_ssm_heads, N_SSM_GROUPS)

        def batch_major(a):
            lead = a.shape[:-2]
            a = a.astype(F32).reshape(lead + (ds, db, a.shape[-1]))
            a = jnp.moveaxis(a, len(lead) + 1, 0)
            return _pad_rows_to(a, a.ndim - 2, 8)

        small_s = batch_major(small[mp:mp + ms])
        q_s = jnp.transpose(q[:, mp:mp + ms].astype(F32).reshape(n_heads, ds, db, HEAD_DIM), (2, 1, 0, 3))
        yc_s = dsa_sample(page_table, q_s.reshape(db, ds * n_heads, HEAD_DIM), batch_major(qi[:, mp:mp + ms]),
                          small_s[:, :, WIDX_LANE:WIDX_LANE + n_idx], batch_major(kf[mp:mp + ms]),
                          batch_major(vf[mp:mp + ms]), small_s[:, :, :IDX_DIM],
                          pages_t(cache_k), pages_t(cache_v), pages_t(cache_kidx), l, n_pool, ds, ktop_s)
        yc_s = jnp.swapaxes(yc_s.reshape(db, ds, n_heads * HEAD_DIM), 0, 1).reshape(ms, n_heads * HEAD_DIM).astype(BF16)

        to_tail = lambda y: _pad_rows_to(y, 0, mt - mp)
        x = merge((ya_p, yb_p, yc_p), (to_tail(ya_s), to_tail(yb_s.reshape(ms, d_ssm)), to_tail(yc_s)),
                  main, GATE_COL, x, valid, p_lru[l].astype(BF16), p_ssm[l].astype(BF16),
                  p_attn[l].astype(BF16), w_o[l].astype(BF16), DENSE_TM)
        x = mlp(x, norm2[l].reshape(1, d), w_up[l].astype(BF16), w_down[l].astype(BF16), 2 * DENSE_TM, 1024)

        prm = lambda a: a[:mp].reshape((nb, tp) + a.shape[1:])
        smp = lambda a: jnp.swapaxes(a[mp:mp + ms].reshape((ds, db) + a.shape[1:]), 0, 1)
        last = CONV_W - 1
        conv_tail = lambda c0, width: jnp.stack([main[(b + 1) * tp - last:(b + 1) * tp, c0:c0 + width]
                                                 for b in range(nb)])
        rows_p = (prm(kf)[:, pad:].reshape(nb, tp - pad, n_kv, HEAD_DIM),
                  prm(vf)[:, pad:].reshape(nb, tp - pad, n_kv, HEAD_DIM),
                  prm(small)[:, pad:, :IDX_DIM],
                  jnp.swapaxes(hssm_p, 2, 3),
                  conv_tail(XBC_COL * d, d_xbc),
                  hlru_p[:, 0],
                  conv_tail(X_COL * d, d_rnn))
        rows_s = (smp(kf).reshape(db, ds, n_kv, HEAD_DIM),
                  smp(vf).reshape(db, ds, n_kv, HEAD_DIM),
                  smp(small)[:, :, :IDX_DIM],
                  hssm_s,
                  jnp.swapaxes(conv_ssm_s, 0, 1),
                  hlru_s,
                  jnp.swapaxes(conv_lru_s, 0, 1))
        for acc, val in zip(outs_p, rows_p):
            acc.append(val)
        for acc, val in zip(outs_s, rows_s):
            acc.append(val)

    y = final_rmsnorm(x, final_norm.reshape(1, d), DENSE_TM)
    y_prompt = y[:mp].reshape(nb, tp, d)[:, pad + N_META:]
    y_sample = jnp.swapaxes(y[mp:mp + ms].reshape(ds, db, d), 0, 1)
    return (y_prompt, y_sample) + tuple(jnp.stack(a) for a in outs_p) + tuple(jnp.stack(a) for a in outs_s)
```

```python
import functools
import math

import numpy as np
import jax
import jax.numpy as jnp
from jax import lax
from jax.experimental import pallas as pl
from jax.experimental.pallas import tpu as pltpu

F32 = jnp.float32
BF16 = jnp.bfloat16
I32 = jnp.int32

N_META = 16
CONV_W = 4
EPS = 1e-6
LRU_C = 8.0
N_RNN_BLOCKS = 16
SSM_HEAD_DIM = 64
N_SSM_GROUPS = 4
SSM_STATE = 128
SSM_CHUNK = 128
HEAD_DIM = 64
N_KV_HEADS = 4
N_IDX_HEADS = 8
IDX_DIM = 64
TOP_K_MAX = 256
TOP_K_FRACTION = 4
ROPE_THETA = 10000.0
PAGE_SIZE = 128
N_BRANCH = 3

LANES = 128
ROW_TILE = 128
MXU_DIM = 256
DSA_KEY_UNIT = MXU_DIM
DSA_KEY_CHUNK_MAX = 3 * MXU_DIM
VT_ROWS = HEAD_DIM + 16
DENSE_TM = 512
VMEM_LIMIT = 56 * 1024 * 1024
INT_MIN = -(2 ** 31)
NEG_BIG = -1e30


def _cparams(sem):
    return pltpu.CompilerParams(dimension_semantics=sem, vmem_limit_bytes=VMEM_LIMIT)


def _rmsnorm_rows(x, g):
    return x * lax.rsqrt(jnp.mean(x * x, axis=-1, keepdims=True) + EPS) * g


def _norm_matmul_kernel(x_ref, g_ref, w_ref, o_ref, h_scr):
    @pl.when(pl.program_id(1) == 0)
    def _():
        h_scr[...] = _rmsnorm_rows(x_ref[...], g_ref[...]).astype(BF16)

    o_ref[...] = jnp.dot(h_scr[...], w_ref[...], preferred_element_type=F32)


def norm_matmul(x, g, w, tm, tn):
    m, d = x.shape
    n = w.shape[1]
    return pl.pallas_call(
        _norm_matmul_kernel,
        grid=(m // tm, n // tn),
        in_specs=[pl.BlockSpec((tm, d), lambda i, j: (i, 0)),
                  pl.BlockSpec((1, d), lambda i, j: (0, 0)),
                  pl.BlockSpec((d, tn), lambda i, j: (0, j))],
        out_specs=pl.BlockSpec((tm, tn), lambda i, j: (i, j)),
        out_shape=jax.ShapeDtypeStruct((m, n), F32),
        scratch_shapes=[pltpu.VMEM((tm, d), BF16)],
        compiler_params=_cparams(("parallel", "arbitrary")),
        name="inproj_main",
    )(x, g, w)


def _attn_proj_kernel(x_ref, g_ref, w_ref, cos_ref, sin_ref,
                      q_ref, qi_ref, k_ref, v_ref, small_ref, kb_ref, vbt_ref, kidxb_ref,
                      *, n_heads, n_idx, n_kv):
    h = _rmsnorm_rows(x_ref[...], g_ref[...]).astype(BF16)
    acc = jnp.dot(h, w_ref[...], preferred_element_type=F32)
    cos = cos_ref[...]
    sin = sin_ref[...]
    lane = lax.broadcasted_iota(I32, cos.shape, 1)
    lo_half = (lane & (HEAD_DIM // 2)) == 0

    def rope(xb, c, s):
        partner = jnp.where(lo_half, pltpu.roll(xb, LANES - HEAD_DIM // 2, 1), pltpu.roll(xb, HEAD_DIM // 2, 1))
        return xb * c + partner * s

    col = 0
    scale = HEAD_DIM ** -0.5 * math.log2(math.e)
    for p in range(n_heads // 2):
        blk = rope(acc[:, col:col + LANES], cos, sin) * scale
        q_ref[2 * p] = blk[:, :HEAD_DIM].astype(BF16)
        q_ref[2 * p + 1] = blk[:, HEAD_DIM:].astype(BF16)
        col += LANES
    for p in range(n_idx // 2):
        blk = rope(acc[:, col:col + LANES], cos, sin)
        qi_ref[2 * p] = blk[:, :IDX_DIM].astype(BF16)
        qi_ref[2 * p + 1] = blk[:, IDX_DIM:].astype(BF16)
        col += LANES
    kw = n_kv * HEAD_DIM
    first = lane < HEAD_DIM
    ones_lane = jnp.where(lane == HEAD_DIM, 1.0, 0.0)
    for p in range(kw // LANES):
        blk = rope(acc[:, col:col + LANES], cos, sin)
        k_ref[:, p * LANES:(p + 1) * LANES] = blk
        kb_ref[2 * p] = jnp.where(first, blk, 0.0).astype(BF16)
        kb_ref[2 * p + 1] = jnp.where(first, pltpu.roll(blk, HEAD_DIM, 1), 0.0).astype(BF16)
        col += LANES
    v_ref[...] = acc[:, col:col + kw]
    for p in range(kw // LANES):
        blk = acc[:, col:col + LANES]
        vbt_ref[2 * p] = jnp.where(first, blk, ones_lane).T[:VT_ROWS].astype(BF16)
        vbt_ref[2 * p + 1] = jnp.where(first, pltpu.roll(blk, HEAD_DIM, 1), ones_lane).T[:VT_ROWS].astype(BF16)
        col += LANES
    is_kidx = lane < IDX_DIM
    blk = rope(acc[:, col:col + LANES], jnp.where(is_kidx, cos, 1.0), jnp.where(is_kidx, sin, 0.0))
    small_ref[...] = blk
    kidxb_ref[...] = blk[:, :IDX_DIM].astype(BF16)


def attn_proj(x, g, w, cos, sin, n_heads, n_idx, n_kv, tm):
    m, d = x.shape
    n = w.shape[1]
    kw = n_kv * HEAD_DIM
    row = lambda i: (i, 0)
    return pl.pallas_call(
        functools.partial(_attn_proj_kernel, n_heads=n_heads, n_idx=n_idx, n_kv=n_kv),
        grid=(m // tm,),
        in_specs=[pl.BlockSpec((tm, d), row),
                  pl.BlockSpec((1, d), lambda i: (0, 0)),
                  pl.BlockSpec((d, n), lambda i: (0, 0)),
                  pl.BlockSpec((tm, LANES), row),
                  pl.BlockSpec((tm, LANES), row)],
        out_specs=[pl.BlockSpec((n_heads, tm, HEAD_DIM), lambda i: (0, i, 0)),
                   pl.BlockSpec((n_idx, tm, IDX_DIM), lambda i: (0, i, 0)),
                   pl.BlockSpec((tm, kw), row),
                   pl.BlockSpec((tm, kw), row),
                   pl.BlockSpec((tm, LANES), row),
                   pl.BlockSpec((n_kv, tm, LANES), lambda i: (0, i, 0)),
                   pl.BlockSpec((n_kv, VT_ROWS, tm), lambda i: (0, 0, i)),
                   pl.BlockSpec((tm, IDX_DIM), row)],
        out_shape=[jax.ShapeDtypeStruct((n_heads, m, HEAD_DIM), BF16),
                   jax.ShapeDtypeStruct((n_idx, m, IDX_DIM), BF16),
                   jax.ShapeDtypeStruct((m, kw), F32),
                   jax.ShapeDtypeStruct((m, kw), F32),
                   jax.ShapeDtypeStruct((m, LANES), F32),
                   jax.ShapeDtypeStruct((n_kv, m, LANES), BF16),
                   jax.ShapeDtypeStruct((n_kv, VT_ROWS, m), BF16),
                   jax.ShapeDtypeStruct((m, IDX_DIM), BF16)],
        compiler_params=_cparams(("parallel",)),
        name="inproj_attn",
    )(x, g, w, cos, sin)


def _merge_kernel(yap_ref, ybp_ref, ycp_ref, yas_ref, ybs_ref, ycs_ref, g0_ref, g1_ref, g2_ref, x_ref, valid_ref,
                  pa_ref, pb_ref, pc_ref, wo_ref, o_ref, *, n_prompt_tiles):
    in_prompt = pl.program_id(0) < n_prompt_tiles

    def branch(yp_ref, ys_ref, p_ref, g_ref):
        y = jnp.where(in_prompt, yp_ref[...], ys_ref[...])
        return jax.nn.sigmoid(g_ref[...]) * jnp.dot(y, p_ref[...], preferred_element_type=F32)

    merged = (branch(yap_ref, yas_ref, pa_ref, g0_ref) + branch(ybp_ref, ybs_ref, pb_ref, g1_ref)
              + branch(ycp_ref, ycs_ref, pc_ref, g2_ref))
    xn = x_ref[...] + jnp.dot(merged.astype(BF16), wo_ref[...], preferred_element_type=F32)
    o_ref[...] = jnp.where(valid_ref[...] > 0.0, xn, 0.0)


def merge(y_prompt, y_sample, main, gate_col, x, valid, pa, pb, pc, wo, tm):
    m, d = x.shape
    mp = y_prompt[0].shape[0]
    assert mp % tm == 0 and (m - mp) % tm == 0 and all(y.shape[0] == m - mp for y in y_sample)
    n_p = mp // tm
    row = lambda i: (i, 0)
    prow = lambda i: (jnp.minimum(i, n_p - 1), 0)
    srow = lambda i: (jnp.maximum(i - n_p, 0), 0)
    wspec = pl.BlockSpec((d, d), lambda i: (0, 0))
    return pl.pallas_call(
        functools.partial(_merge_kernel, n_prompt_tiles=n_p),
        grid=(m // tm,),
        in_specs=[pl.BlockSpec((tm, d), prow), pl.BlockSpec((tm, d), prow), pl.BlockSpec((tm, d), prow),
                  pl.BlockSpec((tm, d), srow), pl.BlockSpec((tm, d), srow), pl.BlockSpec((tm, d), srow),
                  pl.BlockSpec((tm, d), lambda i: (i, gate_col)),
                  pl.BlockSpec((tm, d), lambda i: (i, gate_col + 1)),
                  pl.BlockSpec((tm, d), lambda i: (i, gate_col + 2)),
                  pl.BlockSpec((tm, d), row),
                  pl.BlockSpec((tm, 1), row),
                  wspec, wspec, wspec, wspec],
        out_specs=pl.BlockSpec((tm, d), row),
        out_shape=jax.ShapeDtypeStruct((m, d), F32),
        compiler_params=_cparams(("parallel",)),
        name="merge",
    )(*y_prompt, *y_sample, main, main, main, x, valid, pa, pb, pc, wo)


def _mlp_kernel(x_ref, g_ref, wu_ref, wd_ref, o_ref, h_scr, acc_scr):
    j = pl.program_id(1)

    @pl.when(j == 0)
    def _():
        h_scr[...] = _rmsnorm_rows(x_ref[...], g_ref[...]).astype(BF16)
        acc_scr[...] = x_ref[...]

    u = jnp.dot(h_scr[...], wu_ref[...], preferred_element_type=F32)
    u = jnp.square(jnp.maximum(u, 0.0)).astype(BF16)
    acc_scr[...] += jnp.dot(u, wd_ref[...], preferred_element_type=F32)

    @pl.when(j == pl.num_programs(1) - 1)
    def _():
        o_ref[...] = acc_scr[...]


def mlp(x, g, wu, wd, tm, tf):
    m, d = x.shape
    f = wu.shape[1]
    return pl.pallas_call(
        _mlp_kernel,
        grid=(m // tm, f // tf),
        in_specs=[pl.BlockSpec((tm, d), lambda i, j: (i, 0)),
                  pl.BlockSpec((1, d), lambda i, j: (0, 0)),
                  pl.BlockSpec((d, tf), lambda i, j: (0, j)),
                  pl.BlockSpec((tf, d), lambda i, j: (j, 0))],
        out_specs=pl.BlockSpec((tm, d), lambda i, j: (i, 0)),
        out_shape=jax.ShapeDtypeStruct((m, d), F32),
        scratch_shapes=[pltpu.VMEM((tm, d), BF16), pltpu.VMEM((tm, d), F32)],
        compiler_params=_cparams(("parallel", "arbitrary")),
        name="mlp",
    )(x, g, wu, wd)


def _final_norm_kernel(x_ref, g_ref, o_ref):
    o_ref[...] = _rmsnorm_rows(x_ref[...], g_ref[...])


def final_rmsnorm(x, g, tm):
    m, d = x.shape
    return pl.pallas_call(
        _final_norm_kernel,
        grid=(m // tm,),
        in_specs=[pl.BlockSpec((tm, d), lambda i: (i, 0)), pl.BlockSpec((1, d), lambda i: (0, 0))],
        out_specs=pl.BlockSpec((tm, d), lambda i: (i, 0)),
        out_shape=jax.ShapeDtypeStruct((m, d), F32),
        compiler_params=_cparams(("parallel",)),
        name="final_norm",
    )(x, g)


def _softplus(x):
    return jnp.maximum(x, 0.0) + jnp.log(1.0 + jnp.exp(-jnp.abs(x)))


def _silu(x):
    return x * jax.nn.sigmoid(x)


def _gelu_tanh(x):
    return 0.5 * x * (1.0 + jnp.tanh(math.sqrt(2.0 / math.pi) * (x + 0.044715 * (x * x * x))))


def _causal_conv_rows(ext_ref, x, w_ref, b_ref, rows):
    ext_ref[8:8 + rows, :] = x
    out = b_ref[...] + w_ref[CONV_W - 1:CONV_W, :] * x
    for j in range(CONV_W - 1):
        out = out + w_ref[j:j + 1, :] * ext_ref[5 + j:5 + j + rows, :]
    ext_ref[5:8, :] = ext_ref[5 + rows:8 + rows, :]
    return out


def _lru_gates(u, wg_ref, ba_ref, bx_ref, clam_ref, n_grp):
    gw = u.shape[1] // n_grp
    rs, is_ = [], []
    for g in range(n_grp):
        rg = jnp.dot(u[:, g * gw:(g + 1) * gw].astype(BF16), wg_ref[g], preferred_element_type=F32)
        rs.append(rg[:, :gw])
        is_.append(rg[:, gw:])
    r = jax.nn.sigmoid(jnp.concatenate(rs, axis=1) + ba_ref[...])
    i = jax.nn.sigmoid(jnp.concatenate(is_, axis=1) + bx_ref[...])
    log_a = clam_ref[...] * r
    a = jnp.exp(log_a)
    b = jnp.sqrt(1.0 - jnp.exp(2.0 * log_a)) * (i * u)
    return a, b


def _lru_prompt_kernel(x_ref, g_ref, cw_ref, cb_ref, wg_ref, ba_ref, bx_ref, clam_ref,
                       y_ref, hout_ref, ext_scr, h_scr, *, pad, n_grp):
    c = pl.program_id(1)
    rows = x_ref.shape[0]

    @pl.when(c == 0)
    def _():
        ext_scr[0:8, :] = jnp.zeros((8, ext_scr.shape[1]), F32)
        h_scr[...] = jnp.zeros(h_scr.shape, F32)

    u = _causal_conv_rows(ext_scr, x_ref[...], cw_ref, cb_ref, rows)
    a, b = _lru_gates(u, wg_ref, ba_ref, bx_ref, clam_ref, n_grp)
    ridx = lax.broadcasted_iota(I32, (rows, 1), 0)
    b = jnp.where(c * rows + ridx >= pad, b, 0.0)
    s = 1
    while s < rows:
        keep = ridx >= s
        a_sh = jnp.where(keep, pltpu.roll(a, s, 0), 1.0)
        b_sh = jnp.where(keep, pltpu.roll(b, s, 0), 0.0)
        b = a * b_sh + b
        a = a * a_sh
        s *= 2
    h = a * h_scr[...] + b
    h_scr[...] = h[rows - 1:rows, :]
    hout_ref[0] = h[rows - 1:rows, :]
    y_ref[...] = (h * _gelu_tanh(g_ref[...])).astype(BF16)


def lru_prompt(main, xcol, gcol, nb, tp, pad, cw, cb, wg, ba, bx, clam):
    d = cw.shape[1]
    n_grp = wg.shape[0]
    nt = tp // ROW_TILE
    const2 = lambda b, c: (0, 0)
    return pl.pallas_call(
        functools.partial(_lru_prompt_kernel, pad=pad, n_grp=n_grp),
        grid=(nb, nt),
        in_specs=[pl.BlockSpec((ROW_TILE, d), lambda b, c: (b * nt + c, xcol)),
                  pl.BlockSpec((ROW_TILE, d), lambda b, c: (b * nt + c, gcol)),
                  pl.BlockSpec((CONV_W, d), const2),
                  pl.BlockSpec((1, d), const2),
                  pl.BlockSpec(wg.shape, lambda b, c: (0, 0, 0)),
                  pl.BlockSpec((1, d), const2),
                  pl.BlockSpec((1, d), const2),
                  pl.BlockSpec((1, d), const2)],
        out_specs=[pl.BlockSpec((ROW_TILE, d), lambda b, c: (b * nt + c, 0)),
                   pl.BlockSpec((1, 1, d), lambda b, c: (b, 0, 0))],
        out_shape=[jax.ShapeDtypeStruct((nb * tp, d), BF16),
                   jax.ShapeDtypeStruct((nb, 1, d), F32)],
        scratch_shapes=[pltpu.VMEM((8 + ROW_TILE, d), F32), pltpu.VMEM((1, d), F32)],
        compiler_params=_cparams(("parallel", "arbitrary")),
        name="lru_prompt",
    )(main, main, cw, cb, wg, ba, bx, clam)


DT_LANE = IDX_DIM


def _split3_bf16(x):
    hi = x.astype(BF16)
    r1 = x - hi.astype(F32)
    mid = r1.astype(BF16)
    lo = (r1 - mid.astype(F32)).astype(BF16)
    return hi, mid, lo


def _ssd_prompt_kernel(xbc_ref, z_ref, small_ref, cw_ref, cb_ref, dtb_ref, aneg_ref, dfull_ref, ng_ref,
                       y_ref, hout_ref, ext_scr, ht_scr, y_scr, *, pad, n_heads, n_grp):
    c = pl.program_id(1)
    rows = xbc_ref.shape[0]
    d_ssm = z_ref.shape[1]
    hpg = n_heads // n_grp
    P = SSM_HEAD_DIM
    N = SSM_STATE

    @pl.when(c == 0)
    def _():
        ext_scr[0:8, :] = jnp.zeros((8, ext_scr.shape[1]), F32)
        ht_scr[...] = jnp.zeros(ht_scr.shape, F32)

    xbc = _silu(_causal_conv_rows(ext_scr, xbc_ref[...], cw_ref, cb_ref, rows))
    xs = xbc[:, :d_ssm]
    ridx = lax.broadcasted_iota(I32, (rows, 1), 0)
    dt = _softplus(small_ref[...] + dtb_ref[...])
    dt = jnp.where(c * rows + ridx >= pad, dt, 0.0)
    da = dt * aneg_ref[...]
    ii = lax.broadcasted_iota(I32, (rows, rows), 0)
    jj = lax.broadcasted_iota(I32, (rows, rows), 1)
    causal = jj <= ii
    tri = jnp.where(causal, 1.0, 0.0).astype(BF16)
    cum = sum(jnp.dot(tri, part, preferred_element_type=F32) for part in _split3_bf16(da))
    cum_t = cum.T
    dt_t = dt.T
    cum_last = cum[rows - 1:rows, :]
    dec_end = jnp.exp(cum_last - cum) * dt
    ecum = jnp.exp(cum)
    chunk_dec = jnp.exp(cum_last)
    for g in range(n_grp):
        bg = xbc[:, d_ssm + g * N:d_ssm + (g + 1) * N]
        cg = xbc[:, d_ssm + n_grp * N + g * N:d_ssm + n_grp * N + (g + 1) * N]
        bg16 = bg.astype(BF16)
        cg16 = cg.astype(BF16)
        cb = lax.dot_general(cg16, bg16, (((1,), (1,)), ((), ())), preferred_element_type=F32)
        bgt16 = bg.T.astype(BF16)
        for hh in range(hpg):
            h = g * hpg + hh
            ln = DT_LANE + h
            x_h = xs[:, h * P:(h + 1) * P]
            seg = cum[:, ln:ln + 1] - cum_t[ln:ln + 1, :]
            decay = jnp.where(causal, jnp.exp(jnp.where(causal, seg, 0.0)), 0.0)
            w_intra = (cb * decay * dt_t[ln:ln + 1, :]).astype(BF16)
            y_h = jnp.dot(w_intra, x_h.astype(BF16), preferred_element_type=F32)
            ht = ht_scr[h]
            y_h = y_h + jnp.dot(cg16, ht.astype(BF16), preferred_element_type=F32) * ecum[:, ln:ln + 1]
            xw = (x_h * dec_end[:, ln:ln + 1]).astype(BF16)
            ht_new = chunk_dec[:, ln:ln + 1] * ht + jnp.dot(bgt16, xw, preferred_element_type=F32)
            ht_scr[h] = ht_new
            hout_ref[0, h] = ht_new
            y_scr[:, h * P:(h + 1) * P] = y_h
    y = y_scr[...] + dfull_ref[...] * xs
    y = y * _silu(z_ref[...])
    y_ref[...] = _rmsnorm_rows(y, ng_ref[...]).astype(BF16)


def ssd_prompt(main, xbc_col, z_col, small, nb, tp, pad, cw, cb, dtb, aneg, dfull, ng, n_heads, n_grp):
    d_xbc = cw.shape[1]
    d_ssm = ng.shape[1]
    nt = tp // ROW_TILE
    const2 = lambda b, c: (0, 0)
    return pl.pallas_call(
        functools.partial(_ssd_prompt_kernel, pad=pad, n_heads=n_heads, n_grp=n_grp),
        grid=(nb, nt),
        in_specs=[pl.BlockSpec((ROW_TILE, d_xbc), lambda b, c: (b * nt + c, xbc_col)),
                  pl.BlockSpec((ROW_TILE, d_ssm), lambda b, c: (b * nt + c, z_col)),
                  pl.BlockSpec((ROW_TILE, LANES), lambda b, c: (b * nt + c, 0)),
                  pl.BlockSpec((CONV_W, d_xbc), const2),
                  pl.BlockSpec((1, d_xbc), const2),
                  pl.BlockSpec((1, LANES), const2),
                  pl.BlockSpec((1, LANES), const2),
                  pl.BlockSpec((1, d_ssm), const2),
                  pl.BlockSpec((1, d_ssm), const2)],
        out_specs=[pl.BlockSpec((ROW_TILE, d_ssm), lambda b, c: (b * nt + c, 0)),
                   pl.BlockSpec((1, n_heads, SSM_STATE, SSM_HEAD_DIM), lambda b, c: (b, 0, 0, 0))],
        out_shape=[jax.ShapeDtypeStruct((nb * tp, d_ssm), BF16),
                   jax.ShapeDtypeStruct((nb, n_heads, SSM_STATE, SSM_HEAD_DIM), F32)],
        scratch_shapes=[pltpu.VMEM((8 + ROW_TILE, d_xbc), F32),
                        pltpu.VMEM((n_heads, SSM_STATE, SSM_HEAD_DIM), F32),
                        pltpu.VMEM((ROW_TILE, d_ssm), F32)],
        compiler_params=_cparams(("parallel", "arbitrary")),
        name="ssd_prompt",
    )(main, main, small, cw, cb, dtb, aneg, dfull, ng)


WIDX_LANE = IDX_DIM + 16


def _sortable_key(score):
    score = jnp.where(score == 0.0, 0.0, score)
    bits = pltpu.bitcast(score, I32)
    return jnp.where(bits < 0, bits ^ jnp.int32(0x7FFFFFFF), bits)


def _lane_blocks(x):
    return [x[:, u * LANES:(u + 1) * LANES] for u in range(x.shape[1] // LANES)]


def _row_total(x):
    return jnp.broadcast_to(jnp.sum(x, axis=1, keepdims=True), x.shape)


def _select_threshold(count, like, ktop, col_bits, bits_per_step=1):
    zero = jnp.zeros(like.shape, I32)
    assert 32 % bits_per_step == 0

    def t_step(it, t):
        shift = 32 - bits_per_step * (it + 1)
        passed = zero
        for j in range(1, 2 ** bits_per_step):
            cand = t + lax.shift_left(jnp.int32(j), shift)
            passed = passed + jnp.where(count(lambda keys, idx: keys >= cand) >= ktop, 1, 0)
        return t + lax.shift_left(passed, shift)

    t = lax.fori_loop(0, 32 // bits_per_step, t_step, jnp.full(like.shape, INT_MIN, I32))
    return t, _tie_cutoff(count, t, ktop, col_bits)


def _tie_cutoff(count, t, ktop, col_bits):
    zero = jnp.zeros(t.shape, I32)
    n_ge = count(lambda keys, idx: keys >= t)
    n_gt = count(lambda keys, idx: keys > t)
    live = t != INT_MIN
    excess = live & (n_ge > ktop)
    need = ktop - n_gt

    def j_search():
        def j_step(it, j):
            cand = j + lax.shift_left(jnp.int32(1), col_bits - 1 - it)
            n_tie = count(lambda keys, idx: (keys == t) & (idx < cand))
            return jnp.where(n_tie <= need, cand, j)
        return lax.fori_loop(0, col_bits, j_step, zero)

    any_excess = jnp.max(jnp.where(excess, 1, 0)) > 0
    j_cut = lax.cond(any_excess, j_search, lambda: zero)
    big = jnp.int32(2 ** 30)
    return jnp.where(excess, j_cut, jnp.where(live, big, 0))


def _selected(keys, idx, t, j_cut):
    return (keys > t) | ((keys == t) & (idx < j_cut))


def _index_scores(s, w):
    rows, n_h = w.shape
    s = jnp.maximum(s, 0.0)
    acc = w[:, 0:1] * s[0:rows]
    for h in range(1, n_h):
        acc = acc + w[:, h:h + 1] * s[h * rows:(h + 1) * rows]
    return acc


def _dsa_prompt_kernel(q_ref, qi_ref, small_ref, kb_ref, vbt_ref, kidx_ref, o_ref,
                       key_scr, qaug_scr, m_scr, acc_scr, *, ktop, pad, tkc, col_bits):
    i = pl.program_id(1)
    n_q, tq, hd = q_ref.shape
    n_idx = qi_ref.shape[0]
    n_kv = kb_ref.shape[0]
    qpk = n_q // n_kv
    r4 = qpk * tq
    q0 = i * tq
    n_chunks = (q0 + tq + tkc - 1) // tkc
    nt_dims = (((1,), (1,)), ((), ()))

    wt = small_ref[...].T[WIDX_LANE:WIDX_LANE + n_idx, :] * ((n_idx * IDX_DIM) ** -0.5)
    qi_all = qi_ref[...].reshape(n_idx * tq, qi_ref.shape[2])
    qcol = q0 + lax.broadcasted_iota(I32, (tkc, tq), 1)
    krow = lax.broadcasted_iota(I32, (tkc, tq), 0)

    def idx_products(c):
        k0 = pl.multiple_of(c * tkc, tkc)
        return lax.dot_general(kidx_ref[pl.ds(k0, tkc), :], qi_all, nt_dims, preferred_element_type=F32)

    def store_keys(c, s):
        acc = wt[0:1] * jnp.maximum(s[:, 0:tq], 0.0)
        for h in range(1, n_idx):
            acc = acc + wt[h:h + 1] * jnp.maximum(s[:, h * tq:(h + 1) * tq], 0.0)
        kidx = krow + c * tkc
        admissible = (kidx >= pad) & (kidx <= qcol)
        key_scr[c] = jnp.where(admissible, _sortable_key(acc), INT_MIN)

    def score_pair(i2, carry):
        ca = 2 * i2
        cb = jnp.minimum(ca + 1, n_chunks - 1)
        sa = idx_products(ca)
        sb = idx_products(cb)
        store_keys(ca, sa)
        store_keys(cb, sb)
        return carry

    lax.fori_loop(0, (n_chunks + 1) // 2, score_pair, 0)

    fold = 8 * 8

    def count(pred):
        def body(c, cnt):
            hit = jnp.where(pred(key_scr[c], krow + c * tkc), 1, 0)
            return cnt + jnp.sum(hit.reshape(tkc // fold, fold, tq), axis=0)
        cnt = lax.fori_loop(0, n_chunks, body, jnp.zeros((fold, tq), I32))
        return jnp.sum(cnt, axis=0, keepdims=True)

    t, j_cut = _select_threshold(count, jnp.zeros((1, tq), I32), ktop, col_bits)

    rr = lax.broadcasted_iota(I32, (tq, tq), 0)
    cc = lax.broadcasted_iota(I32, (tq, tq), 1)
    onehot = jnp.where(rr == cc, 1.0, 0.0).astype(BF16)
    zeros_tail = jnp.zeros((tq, qaug_scr.shape[1] - tq - hd), BF16)
    for h in range(n_q):
        qaug_scr[h * tq:(h + 1) * tq, :] = jnp.concatenate([onehot, q_ref[h], zeros_tail], axis=1)
    m_scr[...] = jnp.full(m_scr.shape, NEG_BIG, F32)
    acc_scr[...] = jnp.zeros(acc_scr.shape, F32)

    def mask_bias(c):
        idx = krow + pl.multiple_of(c * tkc, tkc)
        return jnp.where(_selected(key_scr[c], idx, t, j_cut), 0.0, NEG_BIG).astype(BF16)

    def qk(g, c, bias_t):
        k0 = pl.multiple_of(c * tkc, tkc)
        lhs = jnp.concatenate([bias_t, kb_ref[g, pl.ds(k0, tkc), :]], axis=1)
        return lax.dot_general(lhs, qaug_scr[g * r4:(g + 1) * r4, :], nt_dims,
                               preferred_element_type=F32)

    def softmax(g, s):
        m_prev = m_scr[g * 8:g * 8 + 1, :]
        m_new = jnp.maximum(m_prev, jnp.max(s, axis=0, keepdims=True))
        m_scr[g * 8:g * 8 + 1, :] = m_new
        return jnp.exp2(s - m_new).astype(BF16), jnp.exp2(m_prev - m_new)

    def pv(g, c, p, alpha):
        k0 = pl.multiple_of(c * tkc, tkc)
        gs = slice(g * VT_ROWS, (g + 1) * VT_ROWS)
        acc_scr[gs, :] = alpha * acc_scr[gs, :] + jnp.dot(vbt_ref[g, :, pl.ds(k0, tkc)], p,
                                                           preferred_element_type=F32)

    def attend_chunks(chunks):
        biases = [mask_bias(c) for c in chunks]
        units = [(c, b, g) for c, b in zip(chunks, biases) for g in range(n_kv)]
        s_q = {0: qk(units[0][2], units[0][0], units[0][1])}
        p_q = {}
        for step in range(len(units) + 1):
            if step + 1 < len(units):
                c, b, g = units[step + 1]
                s_q[step + 1] = qk(g, c, b)
            if step < len(units):
                p_q[step] = softmax(units[step][2], s_q.pop(step))
            if 1 <= step <= len(units):
                c, _, g = units[step - 1]
                pv(g, c, *p_q.pop(step - 1))

    def attend_pair(i2, carry):
        attend_chunks([2 * i2, 2 * i2 + 1])
        return carry

    lax.fori_loop(0, n_chunks // 2, attend_pair, 0)

    @pl.when(n_chunks % 2 == 1)
    def _():
        attend_chunks([n_chunks - 1])

    for pair in range(n_q // 2):
        g, hh = (2 * pair) // qpk, (2 * pair) % qpk
        cols = slice(hh * tq, (hh + 2) * tq)
        denom = acc_scr[g * VT_ROWS + hd:g * VT_ROWS + hd + 1, cols]
        denom = jnp.where(denom > 0.0, denom, 1.0)
        out_t = acc_scr[g * VT_ROWS:g * VT_ROWS + hd, cols] / denom
        both = jnp.concatenate([out_t[:, :tq], out_t[:, tq:]], axis=0).T
        o_ref[:, 2 * pair * hd:(2 * pair + 2) * hd] = both.astype(BF16)


def dsa_prompt(q, qi, small, kb, vbt, kidxb, nb, tp, pad, ktop, tkc):
    n_q, _, hd = q.shape
    n_idx = qi.shape[0]
    n_kv = kb.shape[0]
    tq = ROW_TILE
    nt = tp // tq
    r4 = (n_q // n_kv) * tq
    col_bits = max(1, int(math.ceil(math.log2(tp))))
    return pl.pallas_call(
        functools.partial(_dsa_prompt_kernel, ktop=ktop, pad=pad, tkc=tkc, col_bits=col_bits),
        grid=(nb, nt),
        in_specs=[pl.BlockSpec((n_q, tq, hd), lambda b, i: (0, b * nt + i, 0)),
                  pl.BlockSpec((n_idx, tq, IDX_DIM), lambda b, i: (0, b * nt + i, 0)),
                  pl.BlockSpec((tq, LANES), lambda b, i: (b * nt + i, 0)),
                  pl.BlockSpec((n_kv, tp, LANES), lambda b, i: (0, b, 0)),
                  pl.BlockSpec((n_kv, VT_ROWS, tp), lambda b, i: (0, 0, b)),
                  pl.BlockSpec((tp, IDX_DIM), lambda b, i: (b, 0))],
        out_specs=pl.BlockSpec((tq, n_q * hd), lambda b, i: (b * nt + i, 0)),
        out_shape=jax.ShapeDtypeStruct((nb * tp, n_q * hd), BF16),
        scratch_shapes=[pltpu.VMEM((tp // tkc, tkc, tq), I32),
                        pltpu.VMEM((n_q * tq, 2 * LANES), BF16),
                        pltpu.VMEM((n_kv * 8, r4), F32),
                        pltpu.VMEM((n_kv * VT_ROWS, r4), F32)],
        compiler_params=_cparams(("parallel", "arbitrary")),
        name="dsa_prompt",
    )(q, qi, small, kb, vbt, kidxb)


def _lru_sample_kernel(x_ref, g_ref, hist_ref, h0_ref, cw_ref, cb_ref, wg_ref, ba_ref, bx_ref, clam_ref,
                       y_ref, hout_ref, convout_ref, hist_scr, h_scr, *, n_grp):
    t = pl.program_id(0)

    @pl.when(t == 0)
    def _():
        hist_scr[...] = hist_ref[...]
        h_scr[...] = h0_ref[...]

    x = x_ref[...]
    u = cb_ref[...] + cw_ref[CONV_W - 1:CONV_W, :] * x
    for j in range(CONV_W - 1):
        u = u + cw_ref[j:j + 1, :] * hist_scr[j]
    for j in range(CONV_W - 2):
        hist_scr[j] = hist_scr[j + 1]
    hist_scr[CONV_W - 2] = x
    a, b = _lru_gates(u, wg_ref, ba_ref, bx_ref, clam_ref, n_grp)
    h = a * h_scr[...] + b
    h_scr[...] = h
    y_ref[...] = (h * _gelu_tanh(g_ref[...])).astype(BF16)
    hout_ref[...] = h
    convout_ref[...] = hist_scr[...]


def lru_sample(main, xcol, gcol, row0, db, ds, hist, h0, cw, cb, wg, ba, bx, clam):
    d = cw.shape[1]
    n_grp = wg.shape[0]
    blk0 = row0 // db
    const2 = lambda t: (0, 0)
    const3 = lambda t: (0, 0, 0)
    return pl.pallas_call(
        functools.partial(_lru_sample_kernel, n_grp=n_grp),
        grid=(ds,),
        in_specs=[pl.BlockSpec((db, d), lambda t: (blk0 + t, xcol)),
                  pl.BlockSpec((db, d), lambda t: (blk0 + t, gcol)),
                  pl.BlockSpec((CONV_W - 1, db, d), const3),
                  pl.BlockSpec((db, d), const2),
                  pl.BlockSpec((CONV_W, d), const2),
                  pl.BlockSpec((1, d), const2),
                  pl.BlockSpec(wg.shape, const3),
                  pl.BlockSpec((1, d), const2),
                  pl.BlockSpec((1, d), const2),
                  pl.BlockSpec((1, d), const2)],
        out_specs=[pl.BlockSpec((db, d), lambda t: (t, 0)),
                   pl.BlockSpec((db, d), const2),
                   pl.BlockSpec((CONV_W - 1, db, d), const3)],
        out_shape=[jax.ShapeDtypeStruct((ds * db, d), BF16),
                   jax.ShapeDtypeStruct((db, d), F32),
                   jax.ShapeDtypeStruct((CONV_W - 1, db, d), F32)],
        scratch_shapes=[pltpu.VMEM((CONV_W - 1, db, d), F32), pltpu.VMEM((db, d), F32)],
        compiler_params=_cparams(("arbitrary",)),
        name="lru_sample",
    )(main, main, hist, h0, cw, cb, wg, ba, bx, clam)


def _ssd_sample_kernel(*refs, ds, n_heads, n_grp):
    xbc_refs = refs[0:ds]
    z_refs = refs[ds:2 * ds]
    small_refs = refs[2 * ds:3 * ds]
    (hist_ref, h0_ref, cw_ref, cb_ref, dtb_ref, aneg_ref, dfull_ref, ng_ref, exp_ref,
     y_ref, hout_ref, convout_ref, yint_scr) = refs[3 * ds:]
    bb = xbc_refs[0].shape[0]
    d_ssm = z_refs[0].shape[1]
    hpg = n_heads // n_grp
    P = SSM_HEAD_DIM
    N = SSM_STATE
    gw = hpg * P

    ext = [hist_ref[j] for j in range(CONV_W - 1)] + [r[...] for r in xbc_refs]
    for j in range(CONV_W - 1):
        convout_ref[j] = ext[ds + j]
    xbc = []
    for t in range(ds):
        u = cb_ref[...]
        for j in range(CONV_W):
            u = u + cw_ref[j:j + 1, :] * ext[t + j]
        xbc.append(_silu(u))
    xs = [v[:, :d_ssm] for v in xbc]
    bm = [v[:, d_ssm:d_ssm + n_grp * N] for v in xbc]
    cm = [v[:, d_ssm + n_grp * N:] for v in xbc]
    dt = [_softplus(r[...] + dtb_ref[...]) for r in small_refs]
    cum = []
    run = jnp.zeros_like(dt[0])
    for t in range(ds):
        run = run + dt[t] * aneg_ref[...]
        cum.append(run)
    lane = lax.broadcasted_iota(I32, (bb, LANES), 1)
    coefs = []
    for t in range(ds):
        for j in range(t + 1):
            cbh = jnp.zeros((bb, LANES), F32)
            for g in range(n_grp):
                dotg = jnp.sum(cm[t][:, g * N:(g + 1) * N] * bm[j][:, g * N:(g + 1) * N], axis=1, keepdims=True)
                in_g = (lane >= DT_LANE + g * hpg) & (lane < DT_LANE + (g + 1) * hpg)
                cbh = cbh + jnp.where(in_g, dotg, 0.0)
            coefs.append(cbh * jnp.exp(cum[t] - cum[j]) * dt[j])
    n_intra = len(coefs)
    coefs += [jnp.exp(c) for c in cum]
    coefs += [jnp.exp(cum[ds - 1] - cum[j]) * dt[j] for j in range(ds)]
    coefs.append(jnp.exp(cum[ds - 1]))
    stack = jnp.concatenate(coefs, axis=0)
    wide = sum(jnp.dot(part, exp_ref[...], preferred_element_type=F32) for part in _split3_bf16(stack))
    wide = [wide[k * bb:(k + 1) * bb] for k in range(len(coefs))]
    intra_w = wide[:n_intra]
    ecum_w = wide[n_intra:n_intra + ds]
    dend_w = wide[n_intra + ds:n_intra + 2 * ds]
    cdec_w = wide[n_intra + 2 * ds]
    xw = [dend_w[j] * xs[j] for j in range(ds)]
    zpad = 8 - ds
    for b in range(bb):
        for g in range(n_grp):
            def rows_of(arrs, lo, width):
                parts = [a[b:b + 1, lo:lo + width] for a in arrs]
                if zpad:
                    parts.append(jnp.zeros((zpad, width), F32))
                return jnp.concatenate(parts, axis=0).astype(BF16)
            hg = h0_ref[b, g * hpg:(g + 1) * hpg].reshape(gw, N)
            yint = lax.dot_general(rows_of(cm, g * N, N), hg.astype(BF16), (((1,), (1,)), ((), ())),
                                   preferred_element_type=F32)
            for t in range(ds):
                yint_scr[t, b:b + 1, g * gw:(g + 1) * gw] = yint[t:t + 1]
            upd = lax.dot_general(rows_of(xw, g * gw, gw), rows_of(bm, g * N, N), (((0,), (0,)), ((), ())),
                                  preferred_element_type=F32)
            for hh in range(hpg):
                h = g * hpg + hh
                cd = cdec_w[b:b + 1, h * P:h * P + 1]
                hout_ref[b, h] = cd * hg[hh * P:(hh + 1) * P] + upd[hh * P:(hh + 1) * P]
    k = 0
    for t in range(ds):
        y = ecum_w[t] * yint_scr[t] + dfull_ref[...] * xs[t]
        for j in range(t + 1):
            y = y + intra_w[k] * xs[j]
            k += 1
        y = y * _silu(z_refs[t][...])
        y_ref[t] = _rmsnorm_rows(y, ng_ref[...]).astype(BF16)


def ssd_sample(main, xbc_col, z_col, small, row0, db, ds, bb, hist, h0, cw, cb, dtb, aneg, dfull, ng, expand,
               n_heads, n_grp):
    d_xbc = cw.shape[1]
    d_ssm = ng.shape[1]
    const2 = lambda i: (0, 0)

    def at_t(t, width, col):
        blk0 = (row0 + t * db) // bb
        return pl.BlockSpec((bb, width), lambda i: (blk0 + i, col))

    in_specs = ([at_t(t, d_xbc, xbc_col) for t in range(ds)] + [at_t(t, d_ssm, z_col) for t in range(ds)]
                + [at_t(t, LANES, 0) for t in range(ds)]
                + [pl.BlockSpec((CONV_W - 1, bb, d_xbc), lambda i: (0, i, 0)),
                   pl.BlockSpec((bb, n_heads, SSM_HEAD_DIM, SSM_STATE), lambda i: (i, 0, 0, 0)),
                   pl.BlockSpec((CONV_W, d_xbc), const2),
                   pl.BlockSpec((1, d_xbc), const2),
                   pl.BlockSpec((1, LANES), const2),
                   pl.BlockSpec((1, LANES), const2),
                   pl.BlockSpec((1, d_ssm), const2),
                   pl.BlockSpec((1, d_ssm), const2),
                   pl.BlockSpec((LANES, d_ssm), const2)])
    return pl.pallas_call(
        functools.partial(_ssd_sample_kernel, ds=ds, n_heads=n_heads, n_grp=n_grp),
        grid=(db // bb,),
        in_specs=in_specs,
        out_specs=[pl.BlockSpec((ds, bb, d_ssm), lambda i: (0, i, 0)),
                   pl.BlockSpec((bb, n_heads, SSM_HEAD_DIM, SSM_STATE), lambda i: (i, 0, 0, 0)),
                   pl.BlockSpec((CONV_W - 1, bb, d_xbc), lambda i: (0, i, 0))],
        out_shape=[jax.ShapeDtypeStruct((ds, db, d_ssm), BF16),
                   jax.ShapeDtypeStruct((db, n_heads, SSM_HEAD_DIM, SSM_STATE), F32),
                   jax.ShapeDtypeStruct((CONV_W - 1, db, d_xbc), F32)],
        scratch_shapes=[pltpu.VMEM((ds, bb, d_ssm), F32)],
        compiler_params=_cparams(("parallel",)),
        name="ssd_sample",
    )(*([main] * ds + [main] * ds + [small] * ds + [hist, h0, cw, cb, dtb, aneg, dfull, ng, expand]))


SAMPLE_BATCH_PER_STEP = 2


def _dsa_sample_kernel(pt_ref, *refs, n_pages, ds, ktop, bpb):
    del pt_ref
    q_ref, qi_ref, w_ref, knew_ref, vnew_ref, kxnew_ref = refs[0:6]
    pages = refs[6:6 + 3 * bpb * n_pages]
    page = lambda kind, j, p: pages[(kind * bpb + j) * n_pages + p]
    o_ref, key_scr, logit_scr, new_scr = refs[6 + 3 * bpb * n_pages:]
    n_rows, hd = q_ref.shape[1:]
    n_q = n_rows // ds
    n_idx, rq = qi_ref.shape[1:3]
    kw = knew_ref.shape[2]
    n_kv = kw // hd
    qpk = n_q // n_kv
    n_chunks = n_pages + 1
    nt_dims = (((1,), (1,)), ((), ()))
    assert rq & (rq - 1) == 0

    qrow = lax.broadcasted_iota(I32, (rq, PAGE_SIZE), 0)
    qlane = lax.broadcasted_iota(I32, (rq, PAGE_SIZE), 1)
    new_scr[...] = jnp.zeros(new_scr.shape, F32)
    for j in range(bpb):
        rows = slice(j * rq, (j + 1) * rq)
        w = w_ref[j] * ((n_idx * IDX_DIM) ** -0.5)
        qi = qi_ref[j].reshape(n_idx * rq, qi_ref.shape[3]).astype(BF16)
        for p in range(n_pages):
            sc = _index_scores(jnp.dot(qi, page(0, j, p)[0].astype(BF16), preferred_element_type=F32), w)
            key_scr[0, rows, p * PAGE_SIZE:(p + 1) * PAGE_SIZE] = jnp.where(qrow < ds, _sortable_key(sc), INT_MIN)
        new_scr[j, 0:rq, 0:kxnew_ref.shape[2]] = kxnew_ref[j]
        sc = _index_scores(lax.dot_general(qi, new_scr[j, :, 0:kxnew_ref.shape[2]].astype(BF16), nt_dims,
                                           preferred_element_type=F32), w)
        admissible = (qlane <= qrow) & (qrow < ds)
        key_scr[0, rows, n_pages * PAGE_SIZE:] = jnp.where(admissible, _sortable_key(sc), INT_MIN)
    col_bits = max(1, int(math.ceil(math.log2(n_chunks * PAGE_SIZE))))
    lane = lax.broadcasted_iota(I32, (bpb * rq, PAGE_SIZE), 1)

    def count(pred):
        cnt = jnp.zeros((bpb * rq, PAGE_SIZE), I32)
        for p in range(n_chunks):
            kk = key_scr[0, :, p * PAGE_SIZE:(p + 1) * PAGE_SIZE]
            cnt = cnt + jnp.where(pred(kk, lane + p * PAGE_SIZE), 1, 0)
        return _row_total(cnt)

    t_all, j_all = _select_threshold(count, lane, ktop, col_bits, bits_per_step=4)

    assert n_q & (n_q - 1) == 0 and qpk & (qpk - 1) == 0 and hd & (hd - 1) == 0
    row_group = (lax.broadcasted_iota(I32, (n_rows, kw), 0) & (n_q - 1)) >> (qpk.bit_length() - 1)
    own_lanes = (lax.broadcasted_iota(I32, (n_rows, kw), 1) >> (hd.bit_length() - 1)) == row_group
    for j in range(bpb):
        rows = slice(j * rq, (j + 1) * rq)
        t, j_cut = t_all[rows], j_all[rows]
        qexp = jnp.where(own_lanes, jnp.concatenate([q_ref[j]] * n_kv, axis=1), 0.0).astype(BF16)
        new_scr[j, 0:rq, :] = knew_ref[j]
        knew = new_scr[j].astype(BF16)
        new_scr[j, 0:rq, :] = vnew_ref[j]
        vnew = new_scr[j].astype(BF16)
        for p in range(n_chunks):
            kk = key_scr[0, rows, p * PAGE_SIZE:(p + 1) * PAGE_SIZE]
            bias = jnp.where(_selected(kk, qlane + p * PAGE_SIZE, t, j_cut), 0.0, NEG_BIG)
            bias_rows = jnp.concatenate([jnp.broadcast_to(bias[ts:ts + 1], (n_q, PAGE_SIZE)) for ts in range(ds)],
                                        axis=0)
            if p < n_pages:
                qk = jnp.dot(qexp, page(1, j, p)[0].astype(BF16), preferred_element_type=F32)
            else:
                qk = lax.dot_general(qexp, knew, nt_dims, preferred_element_type=F32)
            logit_scr[j, :, p * PAGE_SIZE:(p + 1) * PAGE_SIZE] = bias_rows + qk
        logits = logit_scr[j]
        prob = jnp.exp2(logits - jnp.max(logits, axis=1, keepdims=True))
        denom = jnp.sum(prob, axis=1, keepdims=True)
        prob = prob.astype(BF16)
        acc = jnp.zeros((n_rows, kw), F32)
        for p in range(n_chunks):
            pp = prob[:, p * PAGE_SIZE:(p + 1) * PAGE_SIZE]
            if p < n_pages:
                acc = acc + lax.dot_general(pp, page(2, j, p)[0].astype(BF16), nt_dims, preferred_element_type=F32)
            else:
                acc = acc + jnp.dot(pp, vnew, preferred_element_type=F32)
        out = jnp.zeros((n_rows, hd), F32)
        for g in range(n_kv):
            out = out + jnp.where(row_group[:, :hd] == g, acc[:, g * hd:(g + 1) * hd], 0.0)
        o_ref[j] = out / denom


def dsa_sample(page_table, q, qi, w, knew, vnew, kxnew, cache_k, cache_v, cache_kidx, layer, n_pool, ds, ktop):
    db, n_rows, hd = q.shape
    n_idx, rq = qi.shape[1:3]
    kw = knew.shape[2]
    n_pages = page_table.shape[1]
    bpb = SAMPLE_BATCH_PER_STEP
    assert db % bpb == 0

    def page_spec(width, j, p):
        return pl.BlockSpec((1, width, PAGE_SIZE), lambda b, pt: (layer * n_pool + pt[b * bpb + j, p], 0, 0))

    own3 = lambda b, pt: (b, 0, 0)
    own4 = lambda b, pt: (b, 0, 0, 0)
    in_specs = ([pl.BlockSpec((bpb, n_rows, hd), own3),
                 pl.BlockSpec((bpb, n_idx, rq, IDX_DIM), own4),
                 pl.BlockSpec((bpb, rq, n_idx), own3),
                 pl.BlockSpec((bpb, rq, kw), own3),
                 pl.BlockSpec((bpb, rq, kw), own3),
                 pl.BlockSpec((bpb, rq, IDX_DIM), own3)]
                + [page_spec(IDX_DIM, j, p) for j in range(bpb) for p in range(n_pages)]
                + [page_spec(kw, j, p) for j in range(bpb) for p in range(n_pages)]
                + [page_spec(kw, j, p) for j in range(bpb) for p in range(n_pages)])
    grid_spec = pltpu.PrefetchScalarGridSpec(
        num_scalar_prefetch=1,
        grid=(db // bpb,),
        in_specs=in_specs,
        out_specs=pl.BlockSpec((bpb, n_rows, hd), own3),
        scratch_shapes=[pltpu.VMEM((1, bpb * rq, (n_pages + 1) * PAGE_SIZE), I32),
                        pltpu.VMEM((bpb, n_rows, (n_pages + 1) * PAGE_SIZE), F32),
                        pltpu.VMEM((bpb, PAGE_SIZE, kw), F32)])
    return pl.pallas_call(
        functools.partial(_dsa_sample_kernel, n_pages=n_pages, ds=ds, ktop=ktop, bpb=bpb),
        grid_spec=grid_spec,
        out_shape=jax.ShapeDtypeStruct((db, n_rows, hd), F32),
        compiler_params=_cparams(("arbitrary",)),
        name="dsa_sample",
    )(page_table, q, qi, w, knew, vnew, kxnew,
      *([cache_kidx] * (bpb * n_pages) + [cache_k] * (bpb * n_pages) + [cache_v] * (bpb * n_pages)))


def _round_up(x, m):
    return (x + m - 1) // m * m


def _largest_tile(total, unit, cap):
    best = unit
    for k in range(1, cap // unit + 1):
        if total % (k * unit) == 0:
            best = k * unit
    return best


def _block_diag_groups(w, per_group):
    nb, r, _ = w.shape
    ng = nb // per_group
    w = w.reshape(ng, per_group, r, r)
    eye = jnp.eye(per_group, dtype=w.dtype)
    return jnp.einsum('gbij,bc->gbicj', w, eye).reshape(ng, per_group * r, per_group * r)


def _pad_rows_to(a, axis, n):
    pad = [(0, 0)] * a.ndim
    pad[axis] = (0, n - a.shape[axis])
    return jnp.pad(a, pad)


def kernel(x_prompt, x_sample, cache_k, cache_v, cache_kidx, state_ssm, state_ssm_conv, state_lru_h, state_lru_conv, page_table, meta_tokens, norm1, w_in, lru_conv_w, lru_conv_b, lru_wa, lru_ba, lru_wx, lru_bx, lru_lambda, ssm_conv_w, ssm_conv_b, ssm_dt_bias, ssm_a_log, ssm_d, ssm_norm, p_lru, p_ssm, p_attn, w_o, norm2, w_up, w_down, final_norm):
    nb, seq, d = x_prompt.shape
    db, ds, _ = x_sample.shape
    depth = w_in.shape[0]
    n_pages = page_table.shape[1]
    past = n_pages * PAGE_SIZE
    n_pool = cache_k.shape[1]
    n_kv = cache_k.shape[3]
    kw = n_kv * HEAD_DIM
    d_rnn = lru_conv_w.shape[2]
    d_xbc = ssm_conv_w.shape[2]
    d_ssm = ssm_norm.shape[1]
    n_ssm_heads = ssm_d.shape[1]
    n_heads = p_attn.shape[1] // HEAD_DIM
    n_idx = N_IDX_HEADS
    d_ff = w_up.shape[2]
    assert d_rnn == d and d_ssm == d and d_xbc == 2 * d and n_heads * HEAD_DIM == d
    assert seq % DSA_KEY_UNIT == 0 and ds <= 8 and n_ssm_heads <= LANES - DT_LANE

    pad = (-N_META) % DSA_KEY_UNIT
    tp = pad + N_META + seq
    mp = nb * tp
    ms = db * ds
    mt = _round_up(mp + ms, 2 * DENSE_TM)
    bb = 8
    assert mp % db == 0 and db % bb == 0
    ktop_p = min(TOP_K_MAX, seq // TOP_K_FRACTION)
    ktop_s = min(TOP_K_MAX, (past + ds) // TOP_K_FRACTION)
    tkc = _largest_tile(tp, DSA_KEY_UNIT, DSA_KEY_CHUNK_MAX)

    xp = jnp.concatenate([jnp.zeros((nb, pad, d), F32),
                          jnp.broadcast_to(meta_tokens[None].astype(F32), (nb, N_META, d)),
                          x_prompt], axis=1).reshape(mp, d)
    xs = jnp.swapaxes(x_sample, 0, 1).reshape(ms, d)
    x = jnp.concatenate([xp, xs, jnp.zeros((mt - mp - ms, d), F32)], axis=0)
    r = np.arange(mt)
    in_prompt = r < mp
    rp = r % tp
    valid_np = np.where(in_prompt, rp >= pad, r < mp + ms)
    pos_np = np.where(in_prompt, np.maximum(rp - pad, 0), np.where(r < mp + ms, past + (r - mp) // db, 0))
    valid = jnp.asarray(valid_np.astype(np.float32)).reshape(mt, 1)
    half = HEAD_DIM // 2
    freq = ROPE_THETA ** (-jnp.arange(half, dtype=F32) / half)
    ang = jnp.asarray(pos_np).astype(F32)[:, None] * freq[None, :]
    cos = jnp.cos(ang)
    sin = jnp.sin(ang)
    cos_t = jnp.concatenate([cos] * (LANES // half), axis=1)
    sin_t = jnp.concatenate([-sin, sin] * (LANES // HEAD_DIM), axis=1)

    splits = (d_rnn, d_rnn, d_ssm, d_xbc, n_ssm_heads, n_heads * HEAD_DIM, kw, kw, n_idx * IDX_DIM, IDX_DIM,
              n_idx, N_BRANCH * d)
    assert sum(splits) == w_in.shape[2]
    off = np.concatenate([[0], np.cumsum(splits)])
    seg = lambda w, i: w[:, off[i]:off[i + 1]]
    LRU_X, LRU_G, SSM_Z, SSM_XBC, SSM_DT, Q, K, V, QI, KIDX, WIDX, GATES = range(12)
    XBC_COL, X_COL, G_COL, Z_COL, GATE_COL = 0, 2, 3, 4, 5
    lane_pos = np.arange(LANES)
    head_lanes = (lane_pos >= DT_LANE) & (lane_pos < DT_LANE + n_ssm_heads)
    expand_np = np.zeros((LANES, d_ssm), np.float32)
    for h in range(n_ssm_heads):
        expand_np[DT_LANE + h, h * SSM_HEAD_DIM:(h + 1) * SSM_HEAD_DIM] = 1.0
    expand = jnp.asarray(expand_np).astype(BF16)

    def to_head_lanes(v):
        return jnp.zeros((1, LANES), F32).at[0, DT_LANE:DT_LANE + n_ssm_heads].set(v)

    def pages_t(cache):
        c = jnp.moveaxis(cache, 2, -1)
        return c.reshape(depth * n_pool, -1, PAGE_SIZE)

    outs_p = [[] for _ in range(7)]
    outs_s = [[] for _ in range(7)]
    for l in range(depth):
        w = w_in[l]
        w_main = jnp.concatenate([seg(w, SSM_XBC), seg(w, LRU_X), seg(w, LRU_G), seg(w, SSM_Z), seg(w, GATES)],
                                 axis=1).astype(BF16)
        tail = LANES - IDX_DIM - n_ssm_heads - n_idx
        w_attn = jnp.concatenate([seg(w, Q), seg(w, QI), seg(w, K), seg(w, V), seg(w, KIDX), seg(w, SSM_DT),
                                  seg(w, WIDX), jnp.zeros((d, tail), F32)], axis=1).astype(BF16)
        per_group = 256 // (d_rnn // N_RNN_BLOCKS)
        wg = jnp.concatenate([_block_diag_groups(lru_wa[l], per_group), _block_diag_groups(lru_wx[l], per_group)],
                             axis=2).astype(BF16)
        clam = (-LRU_C * jax.nn.softplus(-lru_lambda[l])).reshape(1, d_rnn)
        lru_args = (lru_conv_w[l], lru_conv_b[l].reshape(1, -1), wg, lru_ba[l].reshape(1, -1),
                    lru_bx[l].reshape(1, -1), clam)
        dtb = to_head_lanes(ssm_dt_bias[l])
        aneg = to_head_lanes(-jnp.exp(ssm_a_log[l]))
        dfull = jnp.repeat(ssm_d[l], SSM_HEAD_DIM).reshape(1, d_ssm)
        ssm_args = (ssm_conv_w[l], ssm_conv_b[l].reshape(1, -1), dtb, aneg, dfull, ssm_norm[l].reshape(1, -1))

        main = norm_matmul(x, norm1[l].reshape(1, d), w_main, 2 * DENSE_TM, 2048)
        q, qi, kf, vf, small, kb, vbt, kidxb = attn_proj(x, norm1[l].reshape(1, d), w_attn, cos_t, sin_t,
                                                        n_heads, n_idx, n_kv, DENSE_TM)
        ya_p, hlru_p = lru_prompt(main, X_COL, G_COL, nb, tp, pad, *lru_args)
        yb_p, hssm_p = ssd_prompt(main, XBC_COL, Z_COL, small, nb, tp, pad, *ssm_args, n_ssm_heads, N_SSM_GROUPS)
        yc_p = dsa_prompt(q, qi, small, kb, vbt, kidxb, nb, tp, pad, ktop_p, tkc)
        ya_s, hlru_s, conv_lru_s = lru_sample(main, X_COL, G_COL, mp, db, ds,
                                              jnp.swapaxes(state_lru_conv[l], 0, 1), state_lru_h[l], *lru_args)
        yb_s, hssm_s, conv_ssm_s = ssd_sample(main, XBC_COL, Z_COL, small, mp, db, ds, bb,
                                              jnp.swapaxes(state_ssm_conv[l], 0, 1), state_ssm[l], *ssm_args,
                                              expand, n_ssm_heads, N_SSM_GROUPS)

        def batch_major(a):
            lead = a.shape[:-2]
            a = a.astype(F32).reshape(lead + (ds, db, a.shape[-1]))
            a = jnp.moveaxis(a, len(lead) + 1, 0)
            return _pad_rows_to(a, a.ndim - 2, 8)

        small_s = batch_major(small[mp:mp + ms])
        q_s = jnp.transpose(q[:, mp:mp + ms].astype(F32).reshape(n_heads, ds, db, HEAD_DIM), (2, 1, 0, 3))
        yc_s = dsa_sample(page_table, q_s.reshape(db, ds * n_heads, HEAD_DIM), batch_major(qi[:, mp:mp + ms]),
                          small_s[:, :, WIDX_LANE:WIDX_LANE + n_idx], batch_major(kf[mp:mp + ms]),
                          batch_major(vf[mp:mp + ms]), small_s[:, :, :IDX_DIM],
                          pages_t(cache_k), pages_t(cache_v), pages_t(cache_kidx), l, n_pool, ds, ktop_s)
        yc_s = jnp.swapaxes(yc_s.reshape(db, ds, n_heads * HEAD_DIM), 0, 1).reshape(ms, n_heads * HEAD_DIM).astype(BF16)

        to_tail = lambda y: _pad_rows_to(y, 0, mt - mp)
        x = merge((ya_p, yb_p, yc_p), (to_tail(ya_s), to_tail(yb_s.reshape(ms, d_ssm)), to_tail(yc_s)),
                  main, GATE_COL, x, valid, p_lru[l].astype(BF16), p_ssm[l].astype(BF16),
                  p_attn[l].astype(BF16), w_o[l].astype(BF16), DENSE_TM)
        x = mlp(x, norm2[l].reshape(1, d), w_up[l].astype(BF16), w_down[l].astype(BF16), 2 * DENSE_TM, 1024)

        prm = lambda a: a[:mp].reshape((nb, tp) + a.shape[1:])
        smp = lambda a: jnp.swapaxes(a[mp:mp + ms].reshape((ds, db) + a.shape[1:]), 0, 1)
        last = CONV_W - 1
        conv_tail = lambda c0, width: jnp.stack([main[(b + 1) * tp - last:(b + 1) * tp, c0:c0 + width]
                                                 for b in range(nb)])
        rows_p = (prm(kf)[:, pad:].reshape(nb, tp - pad, n_kv, HEAD_DIM),
                  prm(vf)[:, pad:].reshape(nb, tp - pad, n_kv, HEAD_DIM),
                  prm(small)[:, pad:, :IDX_DIM],
                  jnp.swapaxes(hssm_p, 2, 3),
                  conv_tail(XBC_COL * d, d_xbc),
                  hlru_p[:, 0],
                  conv_tail(X_COL * d, d_rnn))
        rows_s = (smp(kf).reshape(db, ds, n_kv, HEAD_DIM),
                  smp(vf).reshape(db, ds, n_kv, HEAD_DIM),
                  smp(small)[:, :, :IDX_DIM],
                  hssm_s,
                  jnp.swapaxes(conv_ssm_s, 0, 1),
                  hlru_s,
                  jnp.swapaxes(conv_lru_s, 0, 1))
        for acc, val in zip(outs_p, rows_p):
            acc.append(val)
        for acc, val in zip(outs_s, rows_s):
            acc.append(val)

    y = final_rmsnorm(x, final_norm.reshape(1, d), DENSE_TM)
    y_prompt = y[:mp].reshape(nb, tp, d)[:, pad + N_META:]
    y_sample = jnp.swapaxes(y[mp:mp + ms].reshape(ds, db, d), 0, 1)
    return (y_prompt, y_sample) + tuple(jnp.stack(a) for a in outs_p) + tuple(jnp.stack(a) for a in outs_s)
```

```python
import functools
import math

import numpy as np
import jax
import jax.numpy as jnp
from jax import lax
from jax.experimental import pallas as pl
from jax.experimental.pallas import tpu as pltpu

F32 = jnp.float32
BF16 = jnp.bfloat16
I32 = jnp.int32

N_META = 16
CONV_W = 4
EPS = 1e-6
LRU_C = 8.0
N_RNN_BLOCKS = 16
SSM_HEAD_DIM = 64
N_SSM_GROUPS = 4
SSM_STATE = 128
SSM_CHUNK = 128
HEAD_DIM = 64
N_KV_HEADS = 4
N_IDX_HEADS = 8
IDX_DIM = 64
TOP_K_MAX = 256
TOP_K_FRACTION = 4
ROPE_THETA = 10000.0
PAGE_SIZE = 128
N_BRANCH = 3

LANES = 128
ROW_TILE = 128
MXU_DIM = 256
DSA_KEY_UNIT = MXU_DIM
DSA_KEY_CHUNK_MAX = 3 * MXU_DIM
VT_ROWS = HEAD_DIM + 16
DENSE_TM = 512
VMEM_LIMIT = 56 * 1024 * 1024
INT_MIN = -(2 ** 31)
NEG_BIG = -1e30


def _cparams(sem):
    return pltpu.CompilerParams(dimension_semantics=sem, vmem_limit_bytes=VMEM_LIMIT)


def _rmsnorm_rows(x, g):
    return x * lax.rsqrt(jnp.mean(x * x, axis=-1, keepdims=True) + EPS) * g


def _norm_matmul_kernel(x_ref, g_ref, w_ref, o_ref, h_scr):
    @pl.when(pl.program_id(1) == 0)
    def _():
        h_scr[...] = _rmsnorm_rows(x_ref[...], g_ref[...]).astype(BF16)

    o_ref[...] = jnp.dot(h_scr[...], w_ref[...], preferred_element_type=F32)


def norm_matmul(x, g, w, tm, tn):
    m, d = x.shape
    n = w.shape[1]
    return pl.pallas_call(
        _norm_matmul_kernel,
        grid=(m // tm, n // tn),
        in_specs=[pl.BlockSpec((tm, d), lambda i, j: (i, 0)),
                  pl.BlockSpec((1, d), lambda i, j: (0, 0)),
                  pl.BlockSpec((d, tn), lambda i, j: (0, j))],
        out_specs=pl.BlockSpec((tm, tn), lambda i, j: (i, j)),
        out_shape=jax.ShapeDtypeStruct((m, n), F32),
        scratch_shapes=[pltpu.VMEM((tm, d), BF16)],
        compiler_params=_cparams(("parallel", "arbitrary")),
        name="inproj_main",
    )(x, g, w)


def _attn_proj_kernel(x_ref, g_ref, w_ref, cos_ref, sin_ref,
                      q_ref, qi_ref, k_ref, v_ref, small_ref, kb_ref, vbt_ref, kidxb_ref,
                      *, n_heads, n_idx, n_kv):
    h = _rmsnorm_rows(x_ref[...], g_ref[...]).astype(BF16)
    acc = jnp.dot(h, w_ref[...], preferred_element_type=F32)
    cos = cos_ref[...]
    sin = sin_ref[...]
    lane = lax.broadcasted_iota(I32, cos.shape, 1)
    lo_half = (lane & (HEAD_DIM // 2)) == 0

    def rope(xb, c, s):
        partner = jnp.where(lo_half, pltpu.roll(xb, LANES - HEAD_DIM // 2, 1), pltpu.roll(xb, HEAD_DIM // 2, 1))
        return xb * c + partner * s

    col = 0
    scale = HEAD_DIM ** -0.5 * math.log2(math.e)
    for p in range(n_heads // 2):
        blk = rope(acc[:, col:col + LANES], cos, sin) * scale
        q_ref[2 * p] = blk[:, :HEAD_DIM].astype(BF16)
        q_ref[2 * p + 1] = blk[:, HEAD_DIM:].astype(BF16)
        col += LANES
    for p in range(n_idx // 2):
        blk = rope(acc[:, col:col + LANES], cos, sin)
        qi_ref[2 * p] = blk[:, :IDX_DIM].astype(BF16)
        qi_ref[2 * p + 1] = blk[:, IDX_DIM:].astype(BF16)
        col += LANES
    kw = n_kv * HEAD_DIM
    first = lane < HEAD_DIM
    ones_lane = jnp.where(lane == HEAD_DIM, 1.0, 0.0)
    for p in range(kw // LANES):
        blk = rope(acc[:, col:col + LANES], cos, sin)
        k_ref[:, p * LANES:(p + 1) * LANES] = blk
        kb_ref[2 * p] = jnp.where(first, blk, 0.0).astype(BF16)
        kb_ref[2 * p + 1] = jnp.where(first, pltpu.roll(blk, HEAD_DIM, 1), 0.0).astype(BF16)
        col += LANES
    v_ref[...] = acc[:, col:col + kw]
    for p in range(kw // LANES):
        blk = acc[:, col:col + LANES]
        vbt_ref[2 * p] = jnp.where(first, blk, ones_lane).T[:VT_ROWS].astype(BF16)
        vbt_ref[2 * p + 1] = jnp.where(first, pltpu.roll(blk, HEAD_DIM, 1), ones_lane).T[:VT_ROWS].astype(BF16)
        col += LANES
    is_kidx = lane < IDX_DIM
    blk = rope(acc[:, col:col + LANES], jnp.where(is_kidx, cos, 1.0), jnp.where(is_kidx, sin, 0.0))
    small_ref[...] = blk
    kidxb_ref[...] = blk[:, :IDX_DIM].astype(BF16)


def attn_proj(x, g, w, cos, sin, n_heads, n_idx, n_kv, tm):
    m, d = x.shape
    n = w.shape[1]
    kw = n_kv * HEAD_DIM
    row = lambda i: (i, 0)
    return pl.pallas_call(
        functools.partial(_attn_proj_kernel, n_heads=n_heads, n_idx=n_idx, n_kv=n_kv),
        grid=(m // tm,),
        in_specs=[pl.BlockSpec((tm, d), row),
                  pl.BlockSpec((1, d), lambda i: (0, 0)),
                  pl.BlockSpec((d, n), lambda i: (0, 0)),
                  pl.BlockSpec((tm, LANES), row),
                  pl.BlockSpec((tm, LANES), row)],
        out_specs=[pl.BlockSpec((n_heads, tm, HEAD_DIM), lambda i: (0, i, 0)),
                   pl.BlockSpec((n_idx, tm, IDX_DIM), lambda i: (0, i, 0)),
                   pl.BlockSpec((tm, kw), row),
                   pl.BlockSpec((tm, kw), row),
                   pl.BlockSpec((tm, LANES), row),
                   pl.BlockSpec((n_kv, tm, LANES), lambda i: (0, i, 0)),
                   pl.BlockSpec((n_kv, VT_ROWS, tm), lambda i: (0, 0, i)),
                   pl.BlockSpec((tm, IDX_DIM), row)],
        out_shape=[jax.ShapeDtypeStruct((n_heads, m, HEAD_DIM), BF16),
                   jax.ShapeDtypeStruct((n_idx, m, IDX_DIM), BF16),
                   jax.ShapeDtypeStruct((m, kw), F32),
                   jax.ShapeDtypeStruct((m, kw), F32),
                   jax.ShapeDtypeStruct((m, LANES), F32),
                   jax.ShapeDtypeStruct((n_kv, m, LANES), BF16),
                   jax.ShapeDtypeStruct((n_kv, VT_ROWS, m), BF16),
                   jax.ShapeDtypeStruct((m, IDX_DIM), BF16)],
        compiler_params=_cparams(("parallel",)),
        name="inproj_attn",
    )(x, g, w, cos, sin)


def _merge_kernel(yap_ref, ybp_ref, ycp_ref, yas_ref, ybs_ref, ycs_ref, g0_ref, g1_ref, g2_ref, x_ref, valid_ref,
                  pa_ref, pb_ref, pc_ref, wo_ref, o_ref, *, n_prompt_tiles):
    in_prompt = pl.program_id(0) < n_prompt_tiles

    def branch(yp_ref, ys_ref, p_ref, g_ref):
        y = jnp.where(in_prompt, yp_ref[...], ys_ref[...])
        return jax.nn.sigmoid(g_ref[...]) * jnp.dot(y, p_ref[...], preferred_element_type=F32)

    merged = (branch(yap_ref, yas_ref, pa_ref, g0_ref) + branch(ybp_ref, ybs_ref, pb_ref, g1_ref)
              + branch(ycp_ref, ycs_ref, pc_ref, g2_ref))
    xn = x_ref[...] + jnp.dot(merged.astype(BF16), wo_ref[...], preferred_element_type=F32)
    o_ref[...] = jnp.where(valid_ref[...] > 0.0, xn, 0.0)


def merge(y_prompt, y_sample, main, gate_col, x, valid, pa, pb, pc, wo, tm):
    m, d = x.shape
    mp = y_prompt[0].shape[0]
    assert mp % tm == 0 and (m - mp) % tm == 0 and all(y.shape[0] == m - mp for y in y_sample)
    n_p = mp // tm
    row = lambda i: (i, 0)
    prow = lambda i: (jnp.minimum(i, n_p - 1), 0)
    srow = lambda i: (jnp.maximum(i - n_p, 0), 0)
    wspec = pl.BlockSpec((d, d), lambda i: (0, 0))
    return pl.pallas_call(
        functools.partial(_merge_kernel, n_prompt_tiles=n_p),
        grid=(m // tm,),
        in_specs=[pl.BlockSpec((tm, d), prow), pl.BlockSpec((tm, d), prow), pl.BlockSpec((tm, d), prow),
                  pl.BlockSpec((tm, d), srow), pl.BlockSpec((tm, d), srow), pl.BlockSpec((tm, d), srow),
                  pl.BlockSpec((tm, d), lambda i: (i, gate_col)),
                  pl.BlockSpec((tm, d), lambda i: (i, gate_col + 1)),
                  pl.BlockSpec((tm, d), lambda i: (i, gate_col + 2)),
                  pl.BlockSpec((tm, d), row),
                  pl.BlockSpec((tm, 1), row),
                  wspec, wspec, wspec, wspec],
        out_specs=pl.BlockSpec((tm, d), row),
        out_shape=jax.ShapeDtypeStruct((m, d), F32),
        compiler_params=_cparams(("parallel",)),
        name="merge",
    )(*y_prompt, *y_sample, main, main, main, x, valid, pa, pb, pc, wo)


def _mlp_kernel(x_ref, g_ref, wu_ref, wd_ref, o_ref, h_scr, acc_scr):
    j = pl.program_id(1)

    @pl.when(j == 0)
    def _():
        h_scr[...] = _rmsnorm_rows(x_ref[...], g_ref[...]).astype(BF16)
        acc_scr[...] = x_ref[...]

    u = jnp.dot(h_scr[...], wu_ref[...], preferred_element_type=F32)
    u = jnp.square(jnp.maximum(u, 0.0)).astype(BF16)
    acc_scr[...] += jnp.dot(u, wd_ref[...], preferred_element_type=F32)

    @pl.when(j == pl.num_programs(1) - 1)
    def _():
        o_ref[...] = acc_scr[...]


def mlp(x, g, wu, wd, tm, tf):
    m, d = x.shape
    f = wu.shape[1]
    return pl.pallas_call(
        _mlp_kernel,
        grid=(m // tm, f // tf),
        in_specs=[pl.BlockSpec((tm, d), lambda i, j: (i, 0)),
                  pl.BlockSpec((1, d), lambda i, j: (0, 0)),
                  pl.BlockSpec((d, tf), lambda i, j: (0, j)),
                  pl.BlockSpec((tf, d), lambda i, j: (j, 0))],
        out_specs=pl.BlockSpec((tm, d), lambda i, j: (i, 0)),
        out_shape=jax.ShapeDtypeStruct((m, d), F32),
        scratch_shapes=[pltpu.VMEM((tm, d), BF16), pltpu.VMEM((tm, d), F32)],
        compiler_params=_cparams(("parallel", "arbitrary")),
        name="mlp",
    )(x, g, wu, wd)


def _final_norm_kernel(x_ref, g_ref, o_ref):
    o_ref[...] = _rmsnorm_rows(x_ref[...], g_ref[...])


def final_rmsnorm(x, g, tm):
    m, d = x.shape
    return pl.pallas_call(
        _final_norm_kernel,
        grid=(m // tm,),
        in_specs=[pl.BlockSpec((tm, d), lambda i: (i, 0)), pl.BlockSpec((1, d), lambda i: (0, 0))],
        out_specs=pl.BlockSpec((tm, d), lambda i: (i, 0)),
        out_shape=jax.ShapeDtypeStruct((m, d), F32),
        compiler_params=_cparams(("parallel",)),
        name="final_norm",
    )(x, g)


def _softplus(x):
    return jnp.maximum(x, 0.0) + jnp.log(1.0 + jnp.exp(-jnp.abs(x)))


def _silu(x):
    return x * jax.nn.sigmoid(x)


def _gelu_tanh(x):
    return 0.5 * x * (1.0 + jnp.tanh(math.sqrt(2.0 / math.pi) * (x + 0.044715 * (x * x * x))))


def _causal_conv_rows(ext_ref, x, w_ref, b_ref, rows):
    ext_ref[8:8 + rows, :] = x
    out = b_ref[...] + w_ref[CONV_W - 1:CONV_W, :] * x
    for j in range(CONV_W - 1):
        out = out + w_ref[j:j + 1, :] * ext_ref[5 + j:5 + j + rows, :]
    ext_ref[5:8, :] = ext_ref[5 + rows:8 + rows, :]
    return out


def _lru_gates(u, wg_ref, ba_ref, bx_ref, clam_ref, n_grp):
    gw = u.shape[1] // n_grp
    rs, is_ = [], []
    for g in range(n_grp):
        rg = jnp.dot(u[:, g * gw:(g + 1) * gw].astype(BF16), wg_ref[g], preferred_element_type=F32)
        rs.append(rg[:, :gw])
        is_.append(rg[:, gw:])
    r = jax.nn.sigmoid(jnp.concatenate(rs, axis=1) + ba_ref[...])
    i = jax.nn.sigmoid(jnp.concatenate(is_, axis=1) + bx_ref[...])
    log_a = clam_ref[...] * r
    a = jnp.exp(log_a)
    b = jnp.sqrt(1.0 - jnp.exp(2.0 * log_a)) * (i * u)
    return a, b


def _lru_prompt_kernel(x_ref, g_ref, cw_ref, cb_ref, wg_ref, ba_ref, bx_ref, clam_ref,
                       y_ref, hout_ref, ext_scr, h_scr, *, pad, n_grp):
    c = pl.program_id(1)
    rows = x_ref.shape[0]

    @pl.when(c == 0)
    def _():
        ext_scr[0:8, :] = jnp.zeros((8, ext_scr.shape[1]), F32)
        h_scr[...] = jnp.zeros(h_scr.shape, F32)

    u = _causal_conv_rows(ext_scr, x_ref[...], cw_ref, cb_ref, rows)
    a, b = _lru_gates(u, wg_ref, ba_ref, bx_ref, clam_ref, n_grp)
    ridx = lax.broadcasted_iota(I32, (rows, 1), 0)
    b = jnp.where(c * rows + ridx >= pad, b, 0.0)
    s = 1
    while s < rows:
        keep = ridx >= s
        a_sh = jnp.where(keep, pltpu.roll(a, s, 0), 1.0)
        b_sh = jnp.where(keep, pltpu.roll(b, s, 0), 0.0)
        b = a * b_sh + b
        a = a * a_sh
        s *= 2
    h = a * h_scr[...] + b
    h_scr[...] = h[rows - 1:rows, :]
    hout_ref[0] = h[rows - 1:rows, :]
    y_ref[...] = (h * _gelu_tanh(g_ref[...])).astype(BF16)


def lru_prompt(main, xcol, gcol, nb, tp, pad, cw, cb, wg, ba, bx, clam):
    d = cw.shape[1]
    n_grp = wg.shape[0]
    nt = tp // ROW_TILE
    const2 = lambda b, c: (0, 0)
    return pl.pallas_call(
        functools.partial(_lru_prompt_kernel, pad=pad, n_grp=n_grp),
        grid=(nb, nt),
        in_specs=[pl.BlockSpec((ROW_TILE, d), lambda b, c: (b * nt + c, xcol)),
                  pl.BlockSpec((ROW_TILE, d), lambda b, c: (b * nt + c, gcol)),
                  pl.BlockSpec((CONV_W, d), const2),
                  pl.BlockSpec((1, d), const2),
                  pl.BlockSpec(wg.shape, lambda b, c: (0, 0, 0)),
                  pl.BlockSpec((1, d), const2),
                  pl.BlockSpec((1, d), const2),
                  pl.BlockSpec((1, d), const2)],
        out_specs=[pl.BlockSpec((ROW_TILE, d), lambda b, c: (b * nt + c, 0)),
                   pl.BlockSpec((1, 1, d), lambda b, c: (b, 0, 0))],
        out_shape=[jax.ShapeDtypeStruct((nb * tp, d), BF16),
                   jax.ShapeDtypeStruct((nb, 1, d), F32)],
        scratch_shapes=[pltpu.VMEM((8 + ROW_TILE, d), F32), pltpu.VMEM((1, d), F32)],
        compiler_params=_cparams(("parallel", "arbitrary")),
        name="lru_prompt",
    )(main, main, cw, cb, wg, ba, bx, clam)


DT_LANE = IDX_DIM


def _split3_bf16(x):
    hi = x.astype(BF16)
    r1 = x - hi.astype(F32)
    mid = r1.astype(BF16)
    lo = (r1 - mid.astype(F32)).astype(BF16)
    return hi, mid, lo


def _ssd_prompt_kernel(xbc_ref, z_ref, small_ref, cw_ref, cb_ref, dtb_ref, aneg_ref, dfull_ref, ng_ref,
                       y_ref, hout_ref, ext_scr, ht_scr, y_scr, *, pad, n_heads, n_grp):
    c = pl.program_id(1)
    rows = xbc_ref.shape[0]
    d_ssm = z_ref.shape[1]
    hpg = n_heads // n_grp
    P = SSM_HEAD_DIM
    N = SSM_STATE

    @pl.when(c == 0)
    def _():
        ext_scr[0:8, :] = jnp.zeros((8, ext_scr.shape[1]), F32)
        ht_scr[...] = jnp.zeros(ht_scr.shape, F32)

    xbc = _silu(_causal_conv_rows(ext_scr, xbc_ref[...], cw_ref, cb_ref, rows))
    xs = xbc[:, :d_ssm]
    ridx = lax.broadcasted_iota(I32, (rows, 1), 0)
    dt = _softplus(small_ref[...] + dtb_ref[...])
    dt = jnp.where(c * rows + ridx >= pad, dt, 0.0)
    da = dt * aneg_ref[...]
    ii = lax.broadcasted_iota(I32, (rows, rows), 0)
    jj = lax.broadcasted_iota(I32, (rows, rows), 1)
    causal = jj <= ii
    tri = jnp.where(causal, 1.0, 0.0).astype(BF16)
    cum = sum(jnp.dot(tri, part, preferred_element_type=F32) for part in _split3_bf16(da))
    cum_t = cum.T
    dt_t = dt.T
    cum_last = cum[rows - 1:rows, :]
    dec_end = jnp.exp(cum_last - cum) * dt
    ecum = jnp.exp(cum)
    chunk_dec = jnp.exp(cum_last)
    for g in range(n_grp):
        bg = xbc[:, d_ssm + g * N:d_ssm + (g + 1) * N]
        cg = xbc[:, d_ssm + n_grp * N + g * N:d_ssm + n_grp * N + (g + 1) * N]
        bg16 = bg.astype(BF16)
        cg16 = cg.astype(BF16)
        cb = lax.dot_general(cg16, bg16, (((1,), (1,)), ((), ())), preferred_element_type=F32)
        bgt16 = bg.T.astype(BF16)
        for hh in range(hpg):
            h = g * hpg + hh
            ln = DT_LANE + h
            x_h = xs[:, h * P:(h + 1) * P]
            seg = cum[:, ln:ln + 1] - cum_t[ln:ln + 1, :]
            decay = jnp.where(causal, jnp.exp(jnp.where(causal, seg, 0.0)), 0.0)
            w_intra = (cb * decay * dt_t[ln:ln + 1, :]).astype(BF16)
            y_h = jnp.dot(w_intra, x_h.astype(BF16), preferred_element_type=F32)
            ht = ht_scr[h]
            y_h = y_h + jnp.dot(cg16, ht.astype(BF16), preferred_element_type=F32) * ecum[:, ln:ln + 1]
            xw = (x_h * dec_end[:, ln:ln + 1]).astype(BF16)
            ht_new = chunk_dec[:, ln:ln + 1] * ht + jnp.dot(bgt16, xw, preferred_element_type=F32)
            ht_scr[h] = ht_new
            hout_ref[0, h] = ht_new
            y_scr[:, h * P:(h + 1) * P] = y_h
    y = y_scr[...] + dfull_ref[...] * xs
    y = y * _silu(z_ref[...])
    y_ref[...] = _rmsnorm_rows(y, ng_ref[...]).astype(BF16)


def ssd_prompt(main, xbc_col, z_col, small, nb, tp, pad, cw, cb, dtb, aneg, dfull, ng, n_heads, n_grp):
    d_xbc = cw.shape[1]
    d_ssm = ng.shape[1]
    nt = tp // ROW_TILE
    const2 = lambda b, c: (0, 0)
    return pl.pallas_call(
        functools.partial(_ssd_prompt_kernel, pad=pad, n_heads=n_heads, n_grp=n_grp),
        grid=(nb, nt),
        in_specs=[pl.BlockSpec((ROW_TILE, d_xbc), lambda b, c: (b * nt + c, xbc_col)),
                  pl.BlockSpec((ROW_TILE, d_ssm), lambda b, c: (b * nt + c, z_col)),
                  pl.BlockSpec((ROW_TILE, LANES), lambda b, c: (b * nt + c, 0)),
                  pl.BlockSpec((CONV_W, d_xbc), const2),
                  pl.BlockSpec((1, d_xbc), const2),
                  pl.BlockSpec((1, LANES), const2),
                  pl.BlockSpec((1, LANES), const2),
                  pl.BlockSpec((1, d_ssm), const2),
                  pl.BlockSpec((1, d_ssm), const2)],
        out_specs=[pl.BlockSpec((ROW_TILE, d_ssm), lambda b, c: (b * nt + c, 0)),
                   pl.BlockSpec((1, n_heads, SSM_STATE, SSM_HEAD_DIM), lambda b, c: (b, 0, 0, 0))],
        out_shape=[jax.ShapeDtypeStruct((nb * tp, d_ssm), BF16),
                   jax.ShapeDtypeStruct((nb, n_heads, SSM_STATE, SSM_HEAD_DIM), F32)],
        scratch_shapes=[pltpu.VMEM((8 + ROW_TILE, d_xbc), F32),
                        pltpu.VMEM((n_heads, SSM_STATE, SSM_HEAD_DIM), F32),
                        pltpu.VMEM((ROW_TILE, d_ssm), F32)],
        compiler_params=_cparams(("parallel", "arbitrary")),
        name="ssd_prompt",
    )(main, main, small, cw, cb, dtb, aneg, dfull, ng)


WIDX_LANE = IDX_DIM + 16


def _sortable_key(score):
    score = jnp.where(score == 0.0, 0.0, score)
    bits = pltpu.bitcast(score, I32)
    return jnp.where(bits < 0, bits ^ jnp.int32(0x7FFFFFFF), bits)


def _lane_blocks(x):
    return [x[:, u * LANES:(u + 1) * LANES] for u in range(x.shape[1] // LANES)]


def _row_total(x):
    return jnp.broadcast_to(jnp.sum(x, axis=1, keepdims=True), x.shape)


def _select_threshold(count, like, ktop, col_bits, bits_per_step=1):
    zero = jnp.zeros(like.shape, I32)
    assert 32 % bits_per_step == 0

    def t_step(it, carry):
        t, n_ge = carry
        shift = 32 - bits_per_step * (it + 1)
        passed = zero
        for j in range(1, 2 ** bits_per_step):
            cand = t + lax.shift_left(jnp.int32(j), shift)
            n = count(lambda keys, idx: keys >= cand)
            ok = n >= ktop
            passed = passed + jnp.where(ok, 1, 0)
            n_ge = jnp.where(ok, n, n_ge)
        return t + lax.shift_left(passed, shift), n_ge

    t, n_ge = lax.fori_loop(0, 32 // bits_per_step, t_step, (jnp.full(like.shape, INT_MIN, I32), zero))
    return t, _tie_cutoff(count, t, n_ge, ktop, col_bits)


def _tie_cutoff(count, t, n_ge, ktop, col_bits):
    zero = jnp.zeros(t.shape, I32)
    live = t != INT_MIN
    excess = live & (n_ge > ktop)

    def j_search():
        need = ktop - count(lambda keys, idx: keys > t)

        def j_step(it, j):
            cand = j + lax.shift_left(jnp.int32(1), col_bits - 1 - it)
            n_tie = count(lambda keys, idx: (keys == t) & (idx < cand))
            return jnp.where(n_tie <= need, cand, j)
        return lax.fori_loop(0, col_bits, j_step, zero)

    any_excess = jnp.max(jnp.where(excess, 1, 0)) > 0
    j_cut = lax.cond(any_excess, j_search, lambda: zero)
    big = jnp.int32(2 ** 30)
    return jnp.where(excess, j_cut, jnp.where(live, big, 0))


def _selected(keys, idx, t, j_cut):
    return (keys > t) | ((keys == t) & (idx < j_cut))


def _index_scores(s, w):
    rows, n_h = w.shape
    s = jnp.maximum(s, 0.0)
    acc = w[:, 0:1] * s[0:rows]
    for h in range(1, n_h):
        acc = acc + w[:, h:h + 1] * s[h * rows:(h + 1) * rows]
    return acc


def _dsa_prompt_kernel(q_ref, qi_ref, small_ref, kb_ref, vbt_ref, kidx_ref, o_ref,
                       key_scr, qaug_scr, m_scr, acc_scr, *, ktop, pad, tkc, col_bits):
    i = pl.program_id(1)
    n_q, tq, hd = q_ref.shape
    n_idx = qi_ref.shape[0]
    n_kv = kb_ref.shape[0]
    qpk = n_q // n_kv
    r4 = qpk * tq
    q0 = i * tq
    n_chunks = (q0 + tq + tkc - 1) // tkc
    nt_dims = (((1,), (1,)), ((), ()))

    wt = small_ref[...].T[WIDX_LANE:WIDX_LANE + n_idx, :] * ((n_idx * IDX_DIM) ** -0.5)
    qi_all = qi_ref[...].reshape(n_idx * tq, qi_ref.shape[2])
    qcol = q0 + lax.broadcasted_iota(I32, (tkc, tq), 1)
    krow = lax.broadcasted_iota(I32, (tkc, tq), 0)

    def idx_products(c):
        k0 = pl.multiple_of(c * tkc, tkc)
        return lax.dot_general(kidx_ref[pl.ds(k0, tkc), :], qi_all, nt_dims, preferred_element_type=F32)

    def store_keys(c, s):
        acc = wt[0:1] * jnp.maximum(s[:, 0:tq], 0.0)
        for h in range(1, n_idx):
            acc = acc + wt[h:h + 1] * jnp.maximum(s[:, h * tq:(h + 1) * tq], 0.0)
        kidx = krow + c * tkc
        admissible = (kidx >= pad) & (kidx <= qcol)
        key_scr[c] = jnp.where(admissible, _sortable_key(acc), INT_MIN)

    def score_pair(i2, carry):
        ca = 2 * i2
        cb = jnp.minimum(ca + 1, n_chunks - 1)
        sa = idx_products(ca)
        sb = idx_products(cb)
        store_keys(ca, sa)
        store_keys(cb, sb)
        return carry

    lax.fori_loop(0, (n_chunks + 1) // 2, score_pair, 0)

    fold = 8 * 8

    def count(pred):
        def body(c, cnt):
            hit = jnp.where(pred(key_scr[c], krow + c * tkc), 1, 0)
            return cnt + jnp.sum(hit.reshape(tkc // fold, fold, tq), axis=0)
        cnt = lax.fori_loop(0, n_chunks, body, jnp.zeros((fold, tq), I32))
        return jnp.sum(cnt, axis=0, keepdims=True)

    t, j_cut = _select_threshold(count, jnp.zeros((1, tq), I32), ktop, col_bits)

    rr = lax.broadcasted_iota(I32, (tq, tq), 0)
    cc = lax.broadcasted_iota(I32, (tq, tq), 1)
    onehot = jnp.where(rr == cc, 1.0, 0.0).astype(BF16)
    zeros_tail = jnp.zeros((tq, qaug_scr.shape[1] - tq - hd), BF16)
    for h in range(n_q):
        qaug_scr[h * tq:(h + 1) * tq, :] = jnp.concatenate([onehot, q_ref[h], zeros_tail], axis=1)
    m_scr[...] = jnp.full(m_scr.shape, NEG_BIG, F32)
    acc_scr[...] = jnp.zeros(acc_scr.shape, F32)

    def mask_bias(c):
        idx = krow + pl.multiple_of(c * tkc, tkc)
        return jnp.where(_selected(key_scr[c], idx, t, j_cut), 0.0, NEG_BIG).astype(BF16)

    def qk(g, c, bias_t):
        k0 = pl.multiple_of(c * tkc, tkc)
        lhs = jnp.concatenate([bias_t, kb_ref[g, pl.ds(k0, tkc), :]], axis=1)
        return lax.dot_general(lhs, qaug_scr[g * r4:(g + 1) * r4, :], nt_dims,
                               preferred_element_type=F32)

    def softmax(g, s):
        m_prev = m_scr[g * 8:g * 8 + 1, :]
        m_new = jnp.maximum(m_prev, jnp.max(s, axis=0, keepdims=True))
        m_scr[g * 8:g * 8 + 1, :] = m_new
        return jnp.exp2(s - m_new).astype(BF16), jnp.exp2(m_prev - m_new)

    def pv(g, c, p, alpha):
        k0 = pl.multiple_of(c * tkc, tkc)
        gs = slice(g * VT_ROWS, (g + 1) * VT_ROWS)
        acc_scr[gs, :] = alpha * acc_scr[gs, :] + jnp.dot(vbt_ref[g, :, pl.ds(k0, tkc)], p,
                                                           preferred_element_type=F32)

    def attend_chunks(chunks):
        biases = [mask_bias(c) for c in chunks]
        units = [(c, b, g) for c, b in zip(chunks, biases) for g in range(n_kv)]
        s_q = {0: qk(units[0][2], units[0][0], units[0][1])}
        p_q = {}
        for step in range(len(units) + 1):
            if step + 1 < len(units):
                c, b, g = units[step + 1]
                s_q[step + 1] = qk(g, c, b)
            if step < len(units):
                p_q[step] = softmax(units[step][2], s_q.pop(step))
            if 1 <= step <= len(units):
                c, _, g = units[step - 1]
                pv(g, c, *p_q.pop(step - 1))

    def attend_pair(i2, carry):
        attend_chunks([2 * i2, 2 * i2 + 1])
        return carry

    lax.fori_loop(0, n_chunks // 2, attend_pair, 0)

    @pl.when(n_chunks % 2 == 1)
    def _():
        attend_chunks([n_chunks - 1])

    for pair in range(n_q // 2):
        g, hh = (2 * pair) // qpk, (2 * pair) % qpk
        cols = slice(hh * tq, (hh + 2) * tq)
        denom = acc_scr[g * VT_ROWS + hd:g * VT_ROWS + hd + 1, cols]
        denom = jnp.where(denom > 0.0, denom, 1.0)
        out_t = acc_scr[g * VT_ROWS:g * VT_ROWS + hd, cols] / denom
        both = jnp.concatenate([out_t[:, :tq], out_t[:, tq:]], axis=0).T
        o_ref[:, 2 * pair * hd:(2 * pair + 2) * hd] = both.astype(BF16)


def dsa_prompt(q, qi, small, kb, vbt, kidxb, nb, tp, pad, ktop, tkc):
    n_q, _, hd = q.shape
    n_idx = qi.shape[0]
    n_kv = kb.shape[0]
    tq = ROW_TILE
    nt = tp // tq
    r4 = (n_q // n_kv) * tq
    col_bits = max(1, int(math.ceil(math.log2(tp))))
    return pl.pallas_call(
        functools.partial(_dsa_prompt_kernel, ktop=ktop, pad=pad, tkc=tkc, col_bits=col_bits),
        grid=(nb, nt),
        in_specs=[pl.BlockSpec((n_q, tq, hd), lambda b, i: (0, b * nt + i, 0)),
                  pl.BlockSpec((n_idx, tq, IDX_DIM), lambda b, i: (0, b * nt + i, 0)),
                  pl.BlockSpec((tq, LANES), lambda b, i: (b * nt + i, 0)),
                  pl.BlockSpec((n_kv, tp, LANES), lambda b, i: (0, b, 0)),
                  pl.BlockSpec((n_kv, VT_ROWS, tp), lambda b, i: (0, 0, b)),
                  pl.BlockSpec((tp, IDX_DIM), lambda b, i: (b, 0))],
        out_specs=pl.BlockSpec((tq, n_q * hd), lambda b, i: (b * nt + i, 0)),
        out_shape=jax.ShapeDtypeStruct((nb * tp, n_q * hd), BF16),
        scratch_shapes=[pltpu.VMEM((tp // tkc, tkc, tq), I32),
                        pltpu.VMEM((n_q * tq, 2 * LANES), BF16),
                        pltpu.VMEM((n_kv * 8, r4), F32),
                        pltpu.VMEM((n_kv * VT_ROWS, r4), F32)],
        compiler_params=_cparams(("parallel", "arbitrary")),
        name="dsa_prompt",
    )(q, qi, small, kb, vbt, kidxb)


def _lru_sample_kernel(x_ref, g_ref, hist_ref, h0_ref, cw_ref, cb_ref, wg_ref, ba_ref, bx_ref, clam_ref,
                       y_ref, hout_ref, convout_ref, hist_scr, h_scr, *, n_grp):
    t = pl.program_id(0)

    @pl.when(t == 0)
    def _():
        hist_scr[...] = hist_ref[...]
        h_scr[...] = h0_ref[...]

    x = x_ref[...]
    u = cb_ref[...] + cw_ref[CONV_W - 1:CONV_W, :] * x
    for j in range(CONV_W - 1):
        u = u + cw_ref[j:j + 1, :] * hist_scr[j]
    for j in range(CONV_W - 2):
        hist_scr[j] = hist_scr[j + 1]
    hist_scr[CONV_W - 2] = x
    a, b = _lru_gates(u, wg_ref, ba_ref, bx_ref, clam_ref, n_grp)
    h = a * h_scr[...] + b
    h_scr[...] = h
    y_ref[...] = (h * _gelu_tanh(g_ref[...])).astype(BF16)
    hout_ref[...] = h
    convout_ref[...] = hist_scr[...]


def lru_sample(main, xcol, gcol, row0, db, ds, hist, h0, cw, cb, wg, ba, bx, clam):
    d = cw.shape[1]
    n_grp = wg.shape[0]
    blk0 = row0 // db
    const2 = lambda t: (0, 0)
    const3 = lambda t: (0, 0, 0)
    return pl.pallas_call(
        functools.partial(_lru_sample_kernel, n_grp=n_grp),
        grid=(ds,),
        in_specs=[pl.BlockSpec((db, d), lambda t: (blk0 + t, xcol)),
                  pl.BlockSpec((db, d), lambda t: (blk0 + t, gcol)),
                  pl.BlockSpec((CONV_W - 1, db, d), const3),
                  pl.BlockSpec((db, d), const2),
                  pl.BlockSpec((CONV_W, d), const2),
                  pl.BlockSpec((1, d), const2),
                  pl.BlockSpec(wg.shape, const3),
                  pl.BlockSpec((1, d), const2),
                  pl.BlockSpec((1, d), const2),
                  pl.BlockSpec((1, d), const2)],
        out_specs=[pl.BlockSpec((db, d), lambda t: (t, 0)),
                   pl.BlockSpec((db, d), const2),
                   pl.BlockSpec((CONV_W - 1, db, d), const3)],
        out_shape=[jax.ShapeDtypeStruct((ds * db, d), BF16),
                   jax.ShapeDtypeStruct((db, d), F32),
                   jax.ShapeDtypeStruct((CONV_W - 1, db, d), F32)],
        scratch_shapes=[pltpu.VMEM((CONV_W - 1, db, d), F32), pltpu.VMEM((db, d), F32)],
        compiler_params=_cparams(("arbitrary",)),
        name="lru_sample",
    )(main, main, hist, h0, cw, cb, wg, ba, bx, clam)


def _ssd_sample_kernel(*refs, ds, n_heads, n_grp):
    xbc_refs = refs[0:ds]
    z_refs = refs[ds:2 * ds]
    small_refs = refs[2 * ds:3 * ds]
    (hist_ref, h0_ref, cw_ref, cb_ref, dtb_ref, aneg_ref, dfull_ref, ng_ref, exp_ref,
     y_ref, hout_ref, convout_ref, yint_scr) = refs[3 * ds:]
    bb = xbc_refs[0].shape[0]
    d_ssm = z_refs[0].shape[1]
    hpg = n_heads // n_grp
    P = SSM_HEAD_DIM
    N = SSM_STATE
    gw = hpg * P

    ext = [hist_ref[j] for j in range(CONV_W - 1)] + [r[...] for r in xbc_refs]
    for j in range(CONV_W - 1):
        convout_ref[j] = ext[ds + j]
    xbc = []
    for t in range(ds):
        u = cb_ref[...]
        for j in range(CONV_W):
            u = u + cw_ref[j:j + 1, :] * ext[t + j]
        xbc.append(_silu(u))
    xs = [v[:, :d_ssm] for v in xbc]
    bm = [v[:, d_ssm:d_ssm + n_grp * N] for v in xbc]
    cm = [v[:, d_ssm + n_grp * N:] for v in xbc]
    dt = [_softplus(r[...] + dtb_ref[...]) for r in small_refs]
    cum = []
    run = jnp.zeros_like(dt[0])
    for t in range(ds):
        run = run + dt[t] * aneg_ref[...]
        cum.append(run)
    lane = lax.broadcasted_iota(I32, (bb, LANES), 1)
    coefs = []
    for t in range(ds):
        for j in range(t + 1):
            cbh = jnp.zeros((bb, LANES), F32)
            for g in range(n_grp):
                dotg = jnp.sum(cm[t][:, g * N:(g + 1) * N] * bm[j][:, g * N:(g + 1) * N], axis=1, keepdims=True)
                in_g = (lane >= DT_LANE + g * hpg) & (lane < DT_LANE + (g + 1) * hpg)
                cbh = cbh + jnp.where(in_g, dotg, 0.0)
            coefs.append(cbh * jnp.exp(cum[t] - cum[j]) * dt[j])
    n_intra = len(coefs)
    coefs += [jnp.exp(c) for c in cum]
    coefs += [jnp.exp(cum[ds - 1] - cum[j]) * dt[j] for j in range(ds)]
    coefs.append(jnp.exp(cum[ds - 1]))
    stack = jnp.concatenate(coefs, axis=0)
    wide = sum(jnp.dot(part, exp_ref[...], preferred_element_type=F32) for part in _split3_bf16(stack))
    wide = [wide[k * bb:(k + 1) * bb] for k in range(len(coefs))]
    intra_w = wide[:n_intra]
    ecum_w = wide[n_intra:n_intra + ds]
    dend_w = wide[n_intra + ds:n_intra + 2 * ds]
    cdec_w = wide[n_intra + 2 * ds]
    xw = [dend_w[j] * xs[j] for j in range(ds)]
    zpad = 8 - ds
    for b in range(bb):
        for g in range(n_grp):
            def rows_of(arrs, lo, width):
                parts = [a[b:b + 1, lo:lo + width] for a in arrs]
                if zpad:
                    parts.append(jnp.zeros((zpad, width), F32))
                return jnp.concatenate(parts, axis=0).astype(BF16)
            hg = h0_ref[b, g * hpg:(g + 1) * hpg].reshape(gw, N)
            yint = lax.dot_general(rows_of(cm, g * N, N), hg.astype(BF16), (((1,), (1,)), ((), ())),
                                   preferred_element_type=F32)
            for t in range(ds):
                yint_scr[t, b:b + 1, g * gw:(g + 1) * gw] = yint[t:t + 1]
            upd = lax.dot_general(rows_of(xw, g * gw, gw), rows_of(bm, g * N, N), (((0,), (0,)), ((), ())),
                                  preferred_element_type=F32)
            for hh in range(hpg):
                h = g * hpg + hh
                cd = cdec_w[b:b + 1, h * P:h * P + 1]
                hout_ref[b, h] = cd * hg[hh * P:(hh + 1) * P] + upd[hh * P:(hh + 1) * P]
    k = 0
    for t in range(ds):
        y = ecum_w[t] * yint_scr[t] + dfull_ref[...] * xs[t]
        for j in range(t + 1):
            y = y + intra_w[k] * xs[j]
            k += 1
        y = y * _silu(z_refs[t][...])
        y_ref[t] = _rmsnorm_rows(y, ng_ref[...]).astype(BF16)


def ssd_sample(main, xbc_col, z_col, small, row0, db, ds, bb, hist, h0, cw, cb, dtb, aneg, dfull, ng, expand,
               n_heads, n_grp):
    d_xbc = cw.shape[1]
    d_ssm = ng.shape[1]
    const2 = lambda i: (0, 0)

    def at_t(t, width, col):
        blk0 = (row0 + t * db) // bb
        return pl.BlockSpec((bb, width), lambda i: (blk0 + i, col))

    in_specs = ([at_t(t, d_xbc, xbc_col) for t in range(ds)] + [at_t(t, d_ssm, z_col) for t in range(ds)]
                + [at_t(t, LANES, 0) for t in range(ds)]
                + [pl.BlockSpec((CONV_W - 1, bb, d_xbc), lambda i: (0, i, 0)),
                   pl.BlockSpec((bb, n_heads, SSM_HEAD_DIM, SSM_STATE), lambda i: (i, 0, 0, 0)),
                   pl.BlockSpec((CONV_W, d_xbc), const2),
                   pl.BlockSpec((1, d_xbc), const2),
                   pl.BlockSpec((1, LANES), const2),
                   pl.BlockSpec((1, LANES), const2),
                   pl.BlockSpec((1, d_ssm), const2),
                   pl.BlockSpec((1, d_ssm), const2),
                   pl.BlockSpec((LANES, d_ssm), const2)])
    return pl.pallas_call(
        functools.partial(_ssd_sample_kernel, ds=ds, n_heads=n_heads, n_grp=n_grp),
        grid=(db // bb,),
        in_specs=in_specs,
        out_specs=[pl.BlockSpec((ds, bb, d_ssm), lambda i: (0, i, 0)),
                   pl.BlockSpec((bb, n_heads, SSM_HEAD_DIM, SSM_STATE), lambda i: (i, 0, 0, 0)),
                   pl.BlockSpec((CONV_W - 1, bb, d_xbc), lambda i: (0, i, 0))],
        out_shape=[jax.ShapeDtypeStruct((ds, db, d_ssm), BF16),
                   jax.ShapeDtypeStruct((db, n_heads, SSM_HEAD_DIM, SSM_STATE), F32),
                   jax.ShapeDtypeStruct((CONV_W - 1, db, d_xbc), F32)],
        scratch_shapes=[pltpu.VMEM((ds, bb, d_ssm), F32)],
        compiler_params=_cparams(("parallel",)),
        name="ssd_sample",
    )(*([main] * ds + [main] * ds + [small] * ds + [hist, h0, cw, cb, dtb, aneg, dfull, ng, expand]))


SAMPLE_BATCH_PER_STEP = 2


def _dsa_sample_kernel(pt_ref, *refs, n_pages, ds, ktop, bpb):
    del pt_ref
    q_ref, qi_ref, w_ref, knew_ref, vnew_ref, kxnew_ref = refs[0:6]
    pages = refs[6:6 + 3 * bpb * n_pages]
    page = lambda kind, j, p: pages[(kind * bpb + j) * n_pages + p]
    o_ref, key_scr, logit_scr, new_scr = refs[6 + 3 * bpb * n_pages:]
    n_rows, hd = q_ref.shape[1:]
    n_q = n_rows // ds
    n_idx, rq = qi_ref.shape[1:3]
    kw = knew_ref.shape[2]
    n_kv = kw // hd
    qpk = n_q // n_kv
    n_chunks = n_pages + 1
    nt_dims = (((1,), (1,)), ((), ()))
    assert rq & (rq - 1) == 0

    qrow = lax.broadcasted_iota(I32, (rq, PAGE_SIZE), 0)
    qlane = lax.broadcasted_iota(I32, (rq, PAGE_SIZE), 1)
    new_scr[...] = jnp.zeros(new_scr.shape, F32)
    for j in range(bpb):
        rows = slice(j * rq, (j + 1) * rq)
        w = w_ref[j] * ((n_idx * IDX_DIM) ** -0.5)
        qi = qi_ref[j].reshape(n_idx * rq, qi_ref.shape[3]).astype(BF16)
        for p in range(n_pages):
            sc = _index_scores(jnp.dot(qi, page(0, j, p)[0].astype(BF16), preferred_element_type=F32), w)
            key_scr[0, rows, p * PAGE_SIZE:(p + 1) * PAGE_SIZE] = jnp.where(qrow < ds, _sortable_key(sc), INT_MIN)
        new_scr[j, 0:rq, 0:kxnew_ref.shape[2]] = kxnew_ref[j]
        sc = _index_scores(lax.dot_general(qi, new_scr[j, :, 0:kxnew_ref.shape[2]].astype(BF16), nt_dims,
                                           preferred_element_type=F32), w)
        admissible = (qlane <= qrow) & (qrow < ds)
        key_scr[0, rows, n_pages * PAGE_SIZE:] = jnp.where(admissible, _sortable_key(sc), INT_MIN)
    col_bits = max(1, int(math.ceil(math.log2(n_chunks * PAGE_SIZE))))
    lane = lax.broadcasted_iota(I32, (bpb * rq, PAGE_SIZE), 1)

    def count(pred):
        cnt = jnp.zeros((bpb * rq, PAGE_SIZE), I32)
        for p in range(n_chunks):
            kk = key_scr[0, :, p * PAGE_SIZE:(p + 1) * PAGE_SIZE]
            cnt = cnt + jnp.where(pred(kk, lane + p * PAGE_SIZE), 1, 0)
        return _row_total(cnt)

    t_all, j_all = _select_threshold(count, lane, ktop, col_bits, bits_per_step=4)

    assert n_q & (n_q - 1) == 0 and qpk & (qpk - 1) == 0 and hd & (hd - 1) == 0
    row_group = (lax.broadcasted_iota(I32, (n_rows, kw), 0) & (n_q - 1)) >> (qpk.bit_length() - 1)
    own_lanes = (lax.broadcasted_iota(I32, (n_rows, kw), 1) >> (hd.bit_length() - 1)) == row_group
    for j in range(bpb):
        rows = slice(j * rq, (j + 1) * rq)
        t, j_cut = t_all[rows], j_all[rows]
        qexp = jnp.where(own_lanes, jnp.concatenate([q_ref[j]] * n_kv, axis=1), 0.0).astype(BF16)
        new_scr[j, 0:rq, :] = knew_ref[j]
        knew = new_scr[j].astype(BF16)
        new_scr[j, 0:rq, :] = vnew_ref[j]
        vnew = new_scr[j].astype(BF16)
        for p in range(n_chunks):
            kk = key_scr[0, rows, p * PAGE_SIZE:(p + 1) * PAGE_SIZE]
            bias = jnp.where(_selected(kk, qlane + p * PAGE_SIZE, t, j_cut), 0.0, NEG_BIG)
            bias_rows = jnp.concatenate([jnp.broadcast_to(bias[ts:ts + 1], (n_q, PAGE_SIZE)) for ts in range(ds)],
                                        axis=0)
            if p < n_pages:
                qk = jnp.dot(qexp, page(1, j, p)[0].astype(BF16), preferred_element_type=F32)
            else:
                qk = lax.dot_general(qexp, knew, nt_dims, preferred_element_type=F32)
            logit_scr[j, :, p * PAGE_SIZE:(p + 1) * PAGE_SIZE] = bias_rows + qk
        logits = logit_scr[j]
        prob = jnp.exp2(logits - jnp.max(logits, axis=1, keepdims=True))
        denom = jnp.sum(prob, axis=1, keepdims=True)
        prob = prob.astype(BF16)
        acc = jnp.zeros((n_rows, kw), F32)
        for p in range(n_chunks):
            pp = prob[:, p * PAGE_SIZE:(p + 1) * PAGE_SIZE]
            if p < n_pages:
                acc = acc + lax.dot_general(pp, page(2, j, p)[0].astype(BF16), nt_dims, preferred_element_type=F32)
            else:
                acc = acc + jnp.dot(pp, vnew, preferred_element_type=F32)
        out = jnp.zeros((n_rows, hd), F32)
        for g in range(n_kv):
            out = out + jnp.where(row_group[:, :hd] == g, acc[:, g * hd:(g + 1) * hd], 0.0)
        o_ref[j] = out / denom


def dsa_sample(page_table, q, qi, w, knew, vnew, kxnew, cache_k, cache_v, cache_kidx, layer, n_pool, ds, ktop):
    db, n_rows, hd = q.shape
    n_idx, rq = qi.shape[1:3]
    kw = knew.shape[2]
    n_pages = page_table.shape[1]
    bpb = SAMPLE_BATCH_PER_STEP
    assert db % bpb == 0

    def page_spec(width, j, p):
        return pl.BlockSpec((1, width, PAGE_SIZE), lambda b, pt: (layer * n_pool + pt[b * bpb + j, p], 0, 0))

    own3 = lambda b, pt: (b, 0, 0)
    own4 = lambda b, pt: (b, 0, 0, 0)
    in_specs = ([pl.BlockSpec((bpb, n_rows, hd), own3),
                 pl.BlockSpec((bpb, n_idx, rq, IDX_DIM), own4),
                 pl.BlockSpec((bpb, rq, n_idx), own3),
                 pl.BlockSpec((bpb, rq, kw), own3),
                 pl.BlockSpec((bpb, rq, kw), own3),
                 pl.BlockSpec((bpb, rq, IDX_DIM), own3)]
                + [page_spec(IDX_DIM, j, p) for j in range(bpb) for p in range(n_pages)]
                + [page_spec(kw, j, p) for j in range(bpb) for p in range(n_pages)]
                + [page_spec(kw, j, p) for j in range(bpb) for p in range(n_pages)])
    grid_spec = pltpu.PrefetchScalarGridSpec(
        num_scalar_prefetch=1,
        grid=(db // bpb,),
        in_specs=in_specs,
        out_specs=pl.BlockSpec((bpb, n_rows, hd), own3),
        scratch_shapes=[pltpu.VMEM((1, bpb * rq, (n_pages + 1) * PAGE_SIZE), I32),
                        pltpu.VMEM((bpb, n_rows, (n_pages + 1) * PAGE_SIZE), F32),
                        pltpu.VMEM((bpb, PAGE_SIZE, kw), F32)])
    return pl.pallas_call(
        functools.partial(_dsa_sample_kernel, n_pages=n_pages, ds=ds, ktop=ktop, bpb=bpb),
        grid_spec=grid_spec,
        out_shape=jax.ShapeDtypeStruct((db, n_rows, hd), F32),
        compiler_params=_cparams(("arbitrary",)),
        name="dsa_sample",
    )(page_table, q, qi, w, knew, vnew, kxnew,
      *([cache_kidx] * (bpb * n_pages) + [cache_k] * (bpb * n_pages) + [cache_v] * (bpb * n_pages)))


def _round_up(x, m):
    return (x + m - 1) // m * m


def _largest_tile(total, unit, cap):
    best = unit
    for k in range(1, cap // unit + 1):
        if total % (k * unit) == 0:
            best = k * unit
    return best


def _block_diag_groups(w, per_group):
    nb, r, _ = w.shape
    ng = nb // per_group
    w = w.reshape(ng, per_group, r, r)
    eye = jnp.eye(per_group, dtype=w.dtype)
    return jnp.einsum('gbij,bc->gbicj', w, eye).reshape(ng, per_group * r, per_group * r)


def _pad_rows_to(a, axis, n):
    pad = [(0, 0)] * a.ndim
    pad[axis] = (0, n - a.shape[axis])
    return jnp.pad(a, pad)


def kernel(x_prompt, x_sample, cache_k, cache_v, cache_kidx, state_ssm, state_ssm_conv, state_lru_h, state_lru_conv, page_table, meta_tokens, norm1, w_in, lru_conv_w, lru_conv_b, lru_wa, lru_ba, lru_wx, lru_bx, lru_lambda, ssm_conv_w, ssm_conv_b, ssm_dt_bias, ssm_a_log, ssm_d, ssm_norm, p_lru, p_ssm, p_attn, w_o, norm2, w_up, w_down, final_norm):
    nb, seq, d = x_prompt.shape
    db, ds, _ = x_sample.shape
    depth = w_in.shape[0]
    n_pages = page_table.shape[1]
    past = n_pages * PAGE_SIZE
    n_pool = cache_k.shape[1]
    n_kv = cache_k.shape[3]
    kw = n_kv * HEAD_DIM
    d_rnn = lru_conv_w.shape[2]
    d_xbc = ssm_conv_w.shape[2]
    d_ssm = ssm_norm.shape[1]
    n_ssm_heads = ssm_d.shape[1]
    n_heads = p_attn.shape[1] // HEAD_DIM
    n_idx = N_IDX_HEADS
    d_ff = w_up.shape[2]
    assert d_rnn == d and d_ssm == d and d_xbc == 2 * d and n_heads * HEAD_DIM == d
    assert seq % DSA_KEY_UNIT == 0 and ds <= 8 and n_ssm_heads <= LANES - DT_LANE

    pad = (-N_META) % DSA_KEY_UNIT
    tp = pad + N_META + seq
    mp = nb * tp
    ms = db * ds
    mt = _round_up(mp + ms, 2 * DENSE_TM)
    bb = 8
    assert mp % db == 0 and db % bb == 0
    ktop_p = min(TOP_K_MAX, seq // TOP_K_FRACTION)
    ktop_s = min(TOP_K_MAX, (past + ds) // TOP_K_FRACTION)
    tkc = _largest_tile(tp, DSA_KEY_UNIT, DSA_KEY_CHUNK_MAX)

    xp = jnp.concatenate([jnp.zeros((nb, pad, d), F32),
                          jnp.broadcast_to(meta_tokens[None].astype(F32), (nb, N_META, d)),
                          x_prompt], axis=1).reshape(mp, d)
    xs = jnp.swapaxes(x_sample, 0, 1).reshape(ms, d)
    x = jnp.concatenate([xp, xs, jnp.zeros((mt - mp - ms, d), F32)], axis=0)
    r = np.arange(mt)
    in_prompt = r < mp
    rp = r % tp
    valid_np = np.where(in_prompt, rp >= pad, r < mp + ms)
    pos_np = np.where(in_prompt, np.maximum(rp - pad, 0), np.where(r < mp + ms, past + (r - mp) // db, 0))
    valid = jnp.asarray(valid_np.astype(np.float32)).reshape(mt, 1)
    half = HEAD_DIM // 2
    freq = ROPE_THETA ** (-jnp.arange(half, dtype=F32) / half)
    ang = jnp.asarray(pos_np).astype(F32)[:, None] * freq[None, :]
    cos = jnp.cos(ang)
    sin = jnp.sin(ang)
    cos_t = jnp.concatenate([cos] * (LANES // half), axis=1)
    sin_t = jnp.concatenate([-sin, sin] * (LANES // HEAD_DIM), axis=1)

    splits = (d_rnn, d_rnn, d_ssm, d_xbc, n_ssm_heads, n_heads * HEAD_DIM, kw, kw, n_idx * IDX_DIM, IDX_DIM,
              n_idx, N_BRANCH * d)
    assert sum(splits) == w_in.shape[2]
    off = np.concatenate([[0], np.cumsum(splits)])
    seg = lambda w, i: w[:, off[i]:off[i + 1]]
    LRU_X, LRU_G, SSM_Z, SSM_XBC, SSM_DT, Q, K, V, QI, KIDX, WIDX, GATES = range(12)
    XBC_COL, X_COL, G_COL, Z_COL, GATE_COL = 0, 2, 3, 4, 5
    lane_pos = np.arange(LANES)
    head_lanes = (lane_pos >= DT_LANE) & (lane_pos < DT_LANE + n_ssm_heads)
    expand_np = np.zeros((LANES, d_ssm), np.float32)
    for h in range(n_ssm_heads):
        expand_np[DT_LANE + h, h * SSM_HEAD_DIM:(h + 1) * SSM_HEAD_DIM] = 1.0
    expand = jnp.asarray(expand_np).astype(BF16)

    def to_head_lanes(v):
        return jnp.zeros((1, LANES), F32).at[0, DT_LANE:DT_LANE + n_ssm_heads].set(v)

    def pages_t(cache):
        c = jnp.moveaxis(cache, 2, -1)
        return c.reshape(depth * n_pool, -1, PAGE_SIZE)

    outs_p = [[] for _ in range(7)]
    outs_s = [[] for _ in range(7)]
    for l in range(depth):
        w = w_in[l]
        w_main = jnp.concatenate([seg(w, SSM_XBC), seg(w, LRU_X), seg(w, LRU_G), seg(w, SSM_Z), seg(w, GATES)],
                                 axis=1).astype(BF16)
        tail = LANES - IDX_DIM - n_ssm_heads - n_idx
        w_attn = jnp.concatenate([seg(w, Q), seg(w, QI), seg(w, K), seg(w, V), seg(w, KIDX), seg(w, SSM_DT),
                                  seg(w, WIDX), jnp.zeros((d, tail), F32)], axis=1).astype(BF16)
        per_group = 256 // (d_rnn // N_RNN_BLOCKS)
        wg = jnp.concatenate([_block_diag_groups(lru_wa[l], per_group), _block_diag_groups(lru_wx[l], per_group)],
                             axis=2).astype(BF16)
        clam = (-LRU_C * jax.nn.softplus(-lru_lambda[l])).reshape(1, d_rnn)
        lru_args = (lru_conv_w[l], lru_conv_b[l].reshape(1, -1), wg, lru_ba[l].reshape(1, -1),
                    lru_bx[l].reshape(1, -1), clam)
        dtb = to_head_lanes(ssm_dt_bias[l])
        aneg = to_head_lanes(-jnp.exp(ssm_a_log[l]))
        dfull = jnp.repeat(ssm_d[l], SSM_HEAD_DIM).reshape(1, d_ssm)
        ssm_args = (ssm_conv_w[l], ssm_conv_b[l].reshape(1, -1), dtb, aneg, dfull, ssm_norm[l].reshape(1, -1))

        main = norm_matmul(x, norm1[l].reshape(1, d), w_main, 2 * DENSE_TM, 2048)
        q, qi, kf, vf, small, kb, vbt, kidxb = attn_proj(x, norm1[l].reshape(1, d), w_attn, cos_t, sin_t,
                                                        n_heads, n_idx, n_kv, DENSE_TM)
        ya_p, hlru_p = lru_prompt(main, X_COL, G_COL, nb, tp, pad, *lru_args)
        yb_p, hssm_p = ssd_prompt(main, XBC_COL, Z_COL, small, nb, tp, pad, *ssm_args, n_ssm_heads, N_SSM_GROUPS)
        yc_p = dsa_prompt(q, qi, small, kb, vbt, kidxb, nb, tp, pad, ktop_p, tkc)
        ya_s, hlru_s, conv_lru_s = lru_sample(main, X_COL, G_COL, mp, db, ds,
                                              jnp.swapaxes(state_lru_conv[l], 0, 1), state_lru_h[l], *lru_args)
        yb_s, hssm_s, conv_ssm_s = ssd_sample(main, XBC_COL, Z_COL, small, mp, db, ds, bb,
                                              jnp.swapaxes(state_ssm_conv[l], 0, 1), state_ssm[l], *ssm_args,
                                              expand, n_ssm_heads, N_SSM_GROUPS)

        def batch_major(a):
            lead = a.shape[:-2]
            a = a.astype(F32).reshape(lead + (ds, db, a.shape[-1]))
            a = jnp.moveaxis(a, len(lead) + 1, 0)
            return _pad_rows_to(a, a.ndim - 2, 8)

        small_s = batch_major(small[mp:mp + ms])
        q_s = jnp.transpose(q[:, mp:mp + ms].astype(F32).reshape(n_heads, ds, db, HEAD_DIM), (2, 1, 0, 3))
        yc_s = dsa_sample(page_table, q_s.reshape(db, ds * n_heads, HEAD_DIM), batch_major(qi[:, mp:mp + ms]),
                          small_s[:, :, WIDX_LANE:WIDX_LANE + n_idx], batch_major(kf[mp:mp + ms]),
                          batch_major(vf[mp:mp + ms]), small_s[:, :, :IDX_DIM],
                          pages_t(cache_k), pages_t(cache_v), pages_t(cache_kidx), l, n_pool, ds, ktop_s)
        yc_s = jnp.swapaxes(yc_s.reshape(db, ds, n_heads * HEAD_DIM), 0, 1).reshape(ms, n_heads * HEAD_DIM).astype(BF16)

        to_tail = lambda y: _pad_rows_to(y, 0, mt - mp)
        x = merge((ya_p, yb_p, yc_p), (to_tail(ya_s), to_tail(yb_s.reshape(ms, d_ssm)), to_tail(yc_s)),
                  main, GATE_COL, x, valid, p_lru[l].astype(BF16), p_ssm[l].astype(BF16),
                  p_attn[l].astype(BF16), w_o[l].astype(BF16), DENSE_TM)
        x = mlp(x, norm2[l].reshape(1, d), w_up[l].astype(BF16), w_down[l].astype(BF16), 2 * DENSE_TM, 1024)

        prm = lambda a: a[:mp].reshape((nb, tp) + a.shape[1:])
        smp = lambda a: jnp.swapaxes(a[mp:mp + ms].reshape((ds, db) + a.shape[1:]), 0, 1)
        last = CONV_W - 1
        conv_tail = lambda c0, width: jnp.stack([main[(b + 1) * tp - last:(b + 1) * tp, c0:c0 + width]
                                                 for b in range(nb)])
        rows_p = (prm(kf)[:, pad:].reshape(nb, tp - pad, n_kv, HEAD_DIM),
                  prm(vf)[:, pad:].reshape(nb, tp - pad, n_kv, HEAD_DIM),
                  prm(small)[:, pad:, :IDX_DIM],
                  jnp.swapaxes(hssm_p, 2, 3),
                  conv_tail(XBC_COL * d, d_xbc),
                  hlru_p[:, 0],
                  conv_tail(X_COL * d, d_rnn))
        rows_s = (smp(kf).reshape(db, ds, n_kv, HEAD_DIM),
                  smp(vf).reshape(db, ds, n_kv, HEAD_DIM),
                  smp(small)[:, :, :IDX_DIM],
                  hssm_s,
                  jnp.swapaxes(conv_ssm_s, 0, 1),
                  hlru_s,
                  jnp.swapaxes(conv_lru_s, 0, 1))
        for acc, val in zip(outs_p, rows_p):
            acc.append(val)
        for acc, val in zip(outs_s, rows_s):
            acc.append(val)

    y = final_rmsnorm(x, final_norm.reshape(1, d), DENSE_TM)
    y_prompt = y[:mp].reshape(nb, tp, d)[:, pad + N_META:]
    y_sample = jnp.swapaxes(y[mp:mp + ms].reshape(ds, db, d), 0, 1)
    return (y_prompt, y_sample) + tuple(jnp.stack(a) for a in outs_p) + tuple(jnp.stack(a) for a in outs_s)
```
